```python
import math
import jax, jax.numpy as jnp
from jax import lax
import numpy as np

D_MODEL = 1024
BATCH = 8
SEQ = 4096
DEPTH = 4

N_MIXERS = 2
N_ATTN_LAYERS = (DEPTH + 1) // 2
N_RWKV_LAYERS = DEPTH // 2
DA_HEADS = 8
DA_HEAD_DIM = D_MODEL // (2 * DA_HEADS)
DA_QBLOCK = 128
RMS_EPS = 1e-5
RW_HEAD = 64
RW_HEADS = D_MODEL // RW_HEAD
RW_DECAY_LORA = max(32, int(round(1.8 * D_MODEL ** 0.5 / 32)) * 32)
RW_AAA_LORA = max(32, int(round(1.8 * D_MODEL ** 0.5 / 32)) * 32)
RW_MV_LORA = max(32, int(round(1.3 * D_MODEL ** 0.5 / 32)) * 32)
RW_GATE_LORA = max(32, int(round(0.6 * D_MODEL ** 0.8 / 32)) * 32)
RW_GN_EPS = 64e-5
MOE_GROUPS = 4
MOE_EPG = 8
MOE_EXPERTS = MOE_GROUPS * MOE_EPG
MOE_TOPK = 2
MOE_HIDDEN = D_MODEL // 2
MOE_BLOCK = 128
DEEPNORM_ALPHA = (2 * DEPTH) ** 0.25
DEEPNORM_BETA = (8 * DEPTH) ** -0.25
LN_EPS = 1e-5

kernel_name = 'hybrid_diffattn_rwkv7_hmoe_deepnorm'


def layer_norm(x, g, b):
    xf = x.astype(jnp.float32)
    mu = jnp.mean(xf, -1, keepdims=True)
    var = jnp.mean(jnp.square(xf - mu), -1, keepdims=True)
    return ((xf - mu) * lax.rsqrt(var + LN_EPS) * g + b).astype(x.dtype)


def diff_attention(x, w_qkv, w_o, lam, subln_g, lambda_init):
    B, S, _ = x.shape
    q, k, v = jnp.split(x @ w_qkv, 3, axis=-1)
    q = q.reshape(B, S, DA_HEADS, 2, DA_HEAD_DIM)
    k = k.reshape(B, S, DA_HEADS, 2, DA_HEAD_DIM)
    v = v.reshape(B, S, DA_HEADS, 2 * DA_HEAD_DIM)
    lamf = lam.astype(jnp.float32)
    lam_full = jnp.exp(jnp.sum(lamf[0] * lamf[1])) - jnp.exp(jnp.sum(lamf[2] * lamf[3])) + lambda_init
    n_blk = S // DA_QBLOCK
    q_blocks = jnp.moveaxis(q.reshape(B, n_blk, DA_QBLOCK, DA_HEADS, 2, DA_HEAD_DIM), 1, 0)
    k_pos = jnp.arange(S)
    scale = DA_HEAD_DIM ** -0.5

    def one_block(args):
        q_blk, blk = args
        s = jnp.einsum('bqhcd,bkhcd->bhcqk', q_blk, k).astype(jnp.float32) * scale
        q_pos = blk * DA_QBLOCK + jnp.arange(DA_QBLOCK)
        causal = k_pos[None, :] <= q_pos[:, None]
        p = jax.nn.softmax(jnp.where(causal, s, -jnp.inf), axis=-1)
        attn = p[:, :, 0] - lam_full * p[:, :, 1]
        return jnp.einsum('bhqk,bkhe->bqhe', attn.astype(v.dtype), v)

    o = lax.map(one_block, (q_blocks, jnp.arange(n_blk)))
    o = jnp.moveaxis(o, 0, 1).reshape(B, S, DA_HEADS, 2 * DA_HEAD_DIM).astype(jnp.float32)
    o = o * lax.rsqrt(jnp.mean(o * o, -1, keepdims=True) + RMS_EPS) * subln_g * (1.0 - lambda_init)
    return o.reshape(B, S, D_MODEL).astype(x.dtype) @ w_o


def wkv7_scan(r, decay, k, v, a_vec, b_vec):
    B, S, H, N = r.shape

    def step(state, inp):
        r_t, w_t, k_t, v_t, a_t, b_t = inp
        sa = jnp.einsum('bhij,bhj->bhi', state, a_t)
        state = (state * w_t[:, :, None, :] + sa[..., None] * b_t[:, :, None, :]
                 + v_t[..., None] * k_t[:, :, None, :])
        return state, jnp.einsum('bhij,bhj->bhi', state, r_t)

    xs = tuple(jnp.moveaxis(t, 1, 0) for t in (r, decay, k, v, a_vec, b_vec))
    _, ys = lax.scan(step, jnp.zeros((B, H, N, N), jnp.float32), xs)
    return jnp.moveaxis(ys, 0, 1)


def rwkv7_time_mix(x, mu, w_rkv, w_o, w0, w1, w2, a0, a1, a2, g1, g2, k_k, k_a, r_k,
                   lnx_g, lnx_b, v_first, value_mix):
    B, S, D = x.shape
    H, N = RW_HEADS, RW_HEAD
    xx = jnp.pad(x, ((0, 0), (1, 0), (0, 0)))[:, :-1] - x
    xm = x[None] + xx[None] * mu[:, None, None, :]
    r, k, v = jnp.einsum('nbsd,nde->nbse', xm[:3], w_rkv)
    xv, xw, xa, xg = xm[2], xm[3], xm[4], xm[5]
    w = -jax.nn.softplus(-(w0 + jnp.tanh(xw @ w1) @ w2)) - 0.5
    if value_mix is None:
        v_first = v
    else:
        v0, v1, v2 = value_mix
        v = v + (v_first - v) * jax.nn.sigmoid(v0 + (xv @ v1) @ v2)
    a = jax.nn.sigmoid(a0 + (xa @ a1) @ a2)
    g = jax.nn.sigmoid(xg @ g1) @ g2
    kk = (k * k_k).reshape(B, S, H, N).astype(jnp.float32)
    kk = kk / jnp.maximum(jnp.sqrt(jnp.sum(kk * kk, -1, keepdims=True)), 1e-12)
    k = k * (1.0 + (a - 1.0) * k_a)
    rh, kh, vh, ah = [t.reshape(B, S, H, N).astype(jnp.float32) for t in (r, k, v, a)]
    decay = jnp.exp(-jnp.exp(w.reshape(B, S, H, N).astype(jnp.float32)))
    y = wkv7_scan(rh, decay, kh, vh, -kk, kk * ah)
    ym = jnp.mean(y, -1, keepdims=True)
    yv = jnp.mean(jnp.square(y - ym), -1, keepdims=True)
    y = (y - ym) * lax.rsqrt(yv + RW_GN_EPS) * lnx_g.reshape(H, N) + lnx_b.reshape(H, N)
    y = y + jnp.sum(rh * kh * r_k, -1, keepdims=True) * vh
    out = (y.reshape(B, S, D).astype(x.dtype) * g) @ w_o
    return out, v_first


def hier_moe(x, rg_w, rg_b, re_w, re_b, w_gu, w_down):
    B, S, D = x.shape
    n_tok = B * S
    t = x.reshape(n_tok, D)
    g_prob = jax.nn.softmax((t @ rg_w + rg_b).astype(jnp.float32), axis=-1)
    g_p, g_idx = lax.top_k(g_prob, 1)
    e_logits = (t @ re_w + re_b).astype(jnp.float32).reshape(n_tok, MOE_GROUPS, MOE_EPG)
    e_logits = jnp.take_along_axis(e_logits, g_idx[:, :, None], axis=1)[:, 0]
    e_p, e_idx = lax.top_k(jax.nn.softmax(e_logits, axis=-1), MOE_TOPK)
    e_p = e_p / jnp.sum(e_p, -1, keepdims=True)
    gate = (g_p * e_p).reshape(-1)
    expert = (g_idx * MOE_EPG + e_idx).reshape(-1).astype(jnp.int32)
    tok = jnp.repeat(jnp.arange(n_tok, dtype=jnp.int32), MOE_TOPK)
    n_assign = n_tok * MOE_TOPK
    order = jnp.argsort(expert)
    e_sorted, tok_sorted, gate_sorted = expert[order], tok[order], gate[order]
    counts = jnp.bincount(expert, length=MOE_EXPERTS).astype(jnp.int32)
    padded = (counts + MOE_BLOCK - 1) // MOE_BLOCK * MOE_BLOCK
    pad_end = jnp.cumsum(padded)
    pad_start = pad_end - padded
    start = jnp.cumsum(counts) - counts
    dest = pad_start[e_sorted] + jnp.arange(n_assign, dtype=jnp.int32) - start[e_sorted]
    n_blocks = -(-n_assign // MOE_BLOCK) + MOE_EXPERTS
    n_rows = n_blocks * MOE_BLOCK
    tok_buf = jnp.full((n_rows,), n_tok, jnp.int32).at[dest].set(tok_sorted)
    gate_buf = jnp.zeros((n_rows,), jnp.float32).at[dest].set(gate_sorted)
    t_pad = jnp.concatenate([t, jnp.zeros((1, D), t.dtype)], axis=0)
    x_buf = t_pad[tok_buf].reshape(n_blocks, MOE_BLOCK, D)
    block_e = jnp.minimum(jnp.searchsorted(pad_end, jnp.arange(n_blocks) * MOE_BLOCK, side='right'),
                          MOE_EXPERTS - 1)

    def expert_block(args):
        xb, e = args
        hg, hu = jnp.split(xb @ w_gu[e], 2, axis=-1)
        return (jax.nn.silu(hg) * hu) @ w_down[e]

    y = lax.map(expert_block, (x_buf, block_e)).reshape(n_rows, D)
    y = y * gate_buf[:, None].astype(y.dtype)
    out = jax.ops.segment_sum(y, tok_buf, num_segments=n_tok + 1)[:n_tok]
    return out.reshape(B, S, D)


def setup_inputs(seed: int = 0) -> dict:
    key = jax.random.key(seed)
    k = jax.random.split(key, 34)
    D, L, NA, NR = D_MODEL, DEPTH, N_ATTN_LAYERS, N_RWKV_LAYERS
    H, N = RW_HEADS, RW_HEAD
    f32 = jnp.float32

    def nrm(i, shape, scale):
        return scale * jax.random.normal(k[i], shape, f32)

    def unif(i, shape, lo, hi):
        return jax.random.uniform(k[i], shape, f32, lo, hi)

    return {
        'x': nrm(0, (BATCH, SEQ, D), 1.0),
        'ln1_g': 1.0 + nrm(1, (L, D), 0.02),
        'ln1_b': nrm(2, (L, D), 0.02),
        'ln2_g': 1.0 + nrm(3, (L, D), 0.02),
        'ln2_b': nrm(4, (L, D), 0.02),
        'attn_w_qkv': nrm(5, (NA, D, 3 * D), D ** -0.5),
        'attn_w_o': nrm(6, (NA, D, D), DEEPNORM_BETA * D ** -0.5),
        'attn_lambda': nrm(7, (NA, 4, DA_HEAD_DIM), 0.1),
        'attn_subln_g': 1.0 + nrm(8, (NA, 2 * DA_HEAD_DIM), 0.02),
        'rw_mu': unif(9, (NR, 6, D), 0.0, 1.0),
        'rw_w_rkv': nrm(10, (NR, 3, D, D), D ** -0.5),
        'rw_w_o': nrm(11, (NR, D, D), DEEPNORM_BETA * D ** -0.5),
        'rw_w0': unif(12, (NR, D), -6.0, -1.0),
        'rw_w1': nrm(13, (NR, D, RW_DECAY_LORA), D ** -0.5),
        'rw_w2': nrm(14, (NR, RW_DECAY_LORA, D), 0.1 * RW_DECAY_LORA ** -0.5),
        'rw_a0': nrm(15, (NR, D), 0.1),
        'rw_a1': nrm(16, (NR, D, RW_AAA_LORA), D ** -0.5),
        'rw_a2': nrm(17, (NR, RW_AAA_LORA, D), 0.5 * RW_AAA_LORA ** -0.5),
        'rw_g1': nrm(18, (NR, D, RW_GATE_LORA), D ** -0.5),
        'rw_g2': nrm(19, (NR, RW_GATE_LORA, D), RW_GATE_LORA ** -0.5),
        'rw_k_k': 0.85 + nrm(20, (NR, D), 0.02),
        'rw_k_a': 1.0 + nrm(21, (NR, D), 0.02),
        'rw_r_k': nrm(22, (NR, H, N), 0.1),
        'rw_lnx_g': 1.0 + nrm(23, (NR, D), 0.02),
        'rw_lnx_b': nrm(24, (NR, D), 0.02),
        'rw_v0': 1.0 + nrm(25, (NR - 1, D), 0.1),
        'rw_v1': nrm(26, (NR - 1, D, RW_MV_LORA), D ** -0.5),
        'rw_v2': nrm(27, (NR - 1, RW_MV_LORA, D), 0.5 * RW_MV_LORA ** -0.5),
        'moe_rg_w': nrm(28, (L, D, MOE_GROUPS), D ** -0.5),
        'moe_rg_b': nrm(29, (L, MOE_GROUPS), 0.01),
        'moe_re_w': nrm(30, (L, D, MOE_EXPERTS), D ** -0.5),
        'moe_re_b': nrm(31, (L, MOE_EXPERTS), 0.01),
        'moe_w_gu': nrm(32, (L, MOE_EXPERTS, D, 2 * MOE_HIDDEN), D ** -0.5),
        'moe_w_down': nrm(33, (L, MOE_EXPERTS, MOE_HIDDEN, D), DEEPNORM_BETA * MOE_HIDDEN ** -0.5),
    }


def reference(x, ln1_g, ln1_b, ln2_g, ln2_b, attn_w_qkv, attn_w_o, attn_lambda, attn_subln_g,
              rw_mu, rw_w_rkv, rw_w_o, rw_w0, rw_w1, rw_w2, rw_a0, rw_a1, rw_a2, rw_g1, rw_g2,
              rw_k_k, rw_k_a, rw_r_k, rw_lnx_g, rw_lnx_b, rw_v0, rw_v1, rw_v2,
              moe_rg_w, moe_rg_b, moe_re_w, moe_re_b, moe_w_gu, moe_w_down):
    v_first = None
    for i in range(DEPTH):
        j = i // N_MIXERS
        if i % N_MIXERS == 0:
            lambda_init = 0.8 - 0.6 * math.exp(-0.3 * i)
            h = diff_attention(x, attn_w_qkv[j], attn_w_o[j], attn_lambda[j], attn_subln_g[j], lambda_init)
        else:
            value_mix = None if j == 0 else (rw_v0[j - 1], rw_v1[j - 1], rw_v2[j - 1])
            h, v_first = rwkv7_time_mix(x, rw_mu[j], rw_w_rkv[j], rw_w_o[j], rw_w0[j], rw_w1[j], rw_w2[j],
                                        rw_a0[j], rw_a1[j], rw_a2[j], rw_g1[j], rw_g2[j], rw_k_k[j],
                                        rw_k_a[j], rw_r_k[j], rw_lnx_g[j], rw_lnx_b[j], v_first, value_mix)
        x = layer_norm(DEEPNORM_ALPHA * x + h, ln1_g[i], ln1_b[i])
        f = hier_moe(x, moe_rg_w[i], moe_rg_b[i], moe_re_w[i], moe_re_b[i], moe_w_gu[i], moe_w_down[i])
        x = layer_norm(DEEPNORM_ALPHA * x + f, ln2_g[i], ln2_b[i])
    return x
```

```python
import functools
import math

import jax
import jax.numpy as jnp
from jax import lax
from jax.experimental import pallas as pl
from jax.experimental.pallas import tpu as pltpu

F32 = jnp.float32
BF16 = jnp.bfloat16

LANES = 128
VMEM_LIMIT = 56 * 1024 * 1024

DA_HEADS = 8
DA_HEAD_DIM = 64
RMS_EPS = 1e-5
RW_HEAD = 64
RW_GN_EPS = 64e-5
MOE_GROUPS = 4
MOE_EPG = 8
MOE_EXPERTS = MOE_GROUPS * MOE_EPG
MOE_TOPK = 2
LN_EPS = 1e-5
N_MIXERS = 2

ROW_TILE = 256
ATTN_TILE = 256
WKV_CHUNK = 64
WKV_TBLOCK = 256
MOE_BLOCK = 256
NEG_BIG = -1e30


def _dot(a, b, precision=None):
    return jnp.dot(a, b, preferred_element_type=F32, precision=precision)


def _dot_nt(a, b):
    return lax.dot_general(a, b, (((1,), (1,)), ((), ())), preferred_element_type=F32)


def _dot_tn(a, b):
    return lax.dot_general(a, b, (((0,), (0,)), ((), ())), preferred_element_type=F32)


def _params(*sem):
    return pltpu.CompilerParams(dimension_semantics=sem, vmem_limit_bytes=VMEM_LIMIT)


def _const_spec(shape):
    nd = len(shape)
    return pl.BlockSpec(shape, lambda *_: (0,) * nd)


def _proj_kernel(x_ref, w_ref, o_ref):
    o_ref[...] = _dot(x_ref[...].astype(BF16), w_ref[...]).astype(o_ref.dtype)


def _proj(x, w, out_dtype):
    n, d = x.shape
    m = w.shape[1]
    tm = 512
    return pl.pallas_call(
        _proj_kernel,
        grid=(n // tm,),
        in_specs=[pl.BlockSpec((tm, d), lambda i: (i, 0)), _const_spec((d, m))],
        out_specs=pl.BlockSpec((tm, m), lambda i: (i, 0)),
        out_shape=jax.ShapeDtypeStruct((n, m), out_dtype),
        compiler_params=_params("parallel"),
        name="qkv_proj",
    )(x, w)


def _attn_kernel(lam_ref, g_ref, q_ref, k_ref, v_ref, o_ref, *, tile, out_scale):
    qi = pl.program_id(2)
    hd = DA_HEAD_DIM
    q = q_ref[0]
    lane = lax.broadcasted_iota(jnp.int32, q.shape, 1)
    zero = jnp.zeros_like(q)
    q_maps = (jnp.where(lane < hd, q, zero), jnp.where(lane >= hd, q, zero))
    row = lax.broadcasted_iota(jnp.int32, (tile, tile), 0)
    col = lax.broadcasted_iota(jnp.int32, (tile, tile), 1)
    causal = col <= row

    def block(j, carry, diagonal):
        start = pl.multiple_of(j * tile, tile)
        kb = k_ref[0, pl.ds(start, tile), :]
        vb = v_ref[0, pl.ds(start, tile), :]
        new = []
        for c in range(2):
            m, l, acc = carry[c]
            s = _dot_nt(q_maps[c], kb)
            if diagonal:
                s = jnp.where(causal, s, NEG_BIG)
            m_new = jnp.maximum(m, jnp.max(s, axis=-1, keepdims=True))
            alpha = jnp.exp(m - m_new)
            p = jnp.exp(s - m_new)
            l = alpha * l + jnp.sum(p, axis=-1, keepdims=True)
            acc = alpha * acc + _dot(p.astype(BF16), vb)
            new.append((m_new, l, acc))
        return tuple(new)

    init_one = (jnp.full((tile, 1), NEG_BIG, F32), jnp.zeros((tile, 1), F32), jnp.zeros((tile, 2 * hd), F32))
    carry = lax.fori_loop(0, qi, lambda j, c: block(j, c, False), (init_one, init_one))
    (_, l0, a0), (_, l1, a1) = block(qi, carry, True)
    o = a0 * (1.0 / l0) - lam_ref[...] * (a1 * (1.0 / l1))
    ms = jnp.mean(o * o, axis=-1, keepdims=True)
    o = o * lax.rsqrt(ms + RMS_EPS) * g_ref[...] * out_scale
    o_ref[0] = o.astype(o_ref.dtype)


def _diff_attention(qkv, lam_row, subln_g, lambda_init, batch, seq):
    d = DA_HEADS * 2 * DA_HEAD_DIM
    qkv = qkv.reshape(batch, seq, 3 * d)
    tile = ATTN_TILE
    hw = 2 * DA_HEAD_DIM
    out = pl.pallas_call(
        functools.partial(_attn_kernel, tile=tile, out_scale=1.0 - lambda_init),
        grid=(batch, DA_HEADS, seq // tile),
        in_specs=[
            _const_spec((1, hw)),
            _const_spec((1, hw)),
            pl.BlockSpec((1, tile, hw), lambda b, h, i: (b, i, h)),
            pl.BlockSpec((1, seq, hw), lambda b, h, i: (b, 0, DA_HEADS + h)),
            pl.BlockSpec((1, seq, hw), lambda b, h, i: (b, 0, 2 * DA_HEADS + h)),
        ],
        out_specs=pl.BlockSpec((1, tile, hw), lambda b, h, i: (b, i, h)),
        out_shape=jax.ShapeDtypeStruct((batch, seq, d), BF16),
        compiler_params=_params("parallel", "parallel", "arbitrary"),
        name="diff_attn",
    )(lam_row, subln_g.reshape(1, hw), qkv, qkv, qkv)
    return out.reshape(batch * seq, d)


def _layer_norm_rows(z, g, b):
    mu = jnp.mean(z, axis=-1, keepdims=True)
    zc = z - mu
    var = jnp.mean(zc * zc, axis=-1, keepdims=True)
    return zc * lax.rsqrt(var + LN_EPS) * g + b


def _post_mixer_kernel(o_ref, x_ref, w_ref, lg_ref, lb_ref, wr_ref, br_ref, x1_ref, x1b_ref, lo_ref, *, alpha):
    h = _dot(o_ref[...], w_ref[...])
    x1 = _layer_norm_rows(alpha * x_ref[...] + h, lg_ref[...], lb_ref[...])
    x1_ref[...] = x1
    x1b_ref[...] = x1.astype(BF16)
    lo_ref[...] = _dot(x1, wr_ref[...], precision=lax.Precision.HIGHEST) + br_ref[...]


def _post_mixer(o, x, w_o, ln_g, ln_b, w_router, b_router, alpha):
    n, d = x.shape
    tm = ROW_TILE
    row = lambda i: (i, 0)
    return pl.pallas_call(
        functools.partial(_post_mixer_kernel, alpha=alpha),
        grid=(n // tm,),
        in_specs=[
            pl.BlockSpec((tm, d), row), pl.BlockSpec((tm, d), row), _const_spec((d, d)),
            _const_spec((1, d)), _const_spec((1, d)), _const_spec((d, LANES)), _const_spec((1, LANES)),
        ],
        out_specs=[pl.BlockSpec((tm, d), row), pl.BlockSpec((tm, d), row), pl.BlockSpec((tm, LANES), row)],
        out_shape=[jax.ShapeDtypeStruct((n, d), F32), jax.ShapeDtypeStruct((n, d), BF16),
                   jax.ShapeDtypeStruct((n, LANES), F32)],
        compiler_params=_params("parallel"),
        name="post_mixer",
    )(o, x, w_o, ln_g.reshape(1, d), ln_b.reshape(1, d), w_router, b_router)


def _post_moe_kernel(f_ref, x_ref, lg_ref, lb_ref, o_ref, *, alpha):
    o_ref[...] = _layer_norm_rows(alpha * x_ref[...] + f_ref[...], lg_ref[...], lb_ref[...])


def _post_moe(f, x, ln_g, ln_b, alpha):
    n, d = x.shape
    tm = ROW_TILE
    row = lambda i: (i, 0)
    return pl.pallas_call(
        functools.partial(_post_moe_kernel, alpha=alpha),
        grid=(n // tm,),
        in_specs=[pl.BlockSpec((tm, d), row), pl.BlockSpec((tm, d), row), _const_spec((1, d)), _const_spec((1, d))],
        out_specs=pl.BlockSpec((tm, d), row),
        out_shape=jax.ShapeDtypeStruct((n, d), F32),
        compiler_params=_params("parallel"),
        name="post_moe",
    )(f, x, ln_g.reshape(1, d), ln_b.reshape(1, d))


def _expert_kernel(be_ref, x_ref, wgu_ref, wd_ref, y_ref, *, hidden):
    del be_ref
    h = _dot(x_ref[...], wgu_ref[0])
    hg = h[:, :hidden]
    hu = h[:, hidden:]
    act = hg * (1.0 / (1.0 + jnp.exp(-hg))) * hu
    y_ref[...] = _dot(act.astype(BF16), wd_ref[0])


def _experts(x_buf, block_e, w_gu, w_down):
    n_rows, d = x_buf.shape
    hidden = w_down.shape[1]
    tm = MOE_BLOCK
    grid_spec = pltpu.PrefetchScalarGridSpec(
        num_scalar_prefetch=1,
        grid=(n_rows // tm,),
        in_specs=[
            pl.BlockSpec((tm, d), lambda i, be: (i, 0)),
            pl.BlockSpec((1, d, 2 * hidden), lambda i, be: (be[i], 0, 0)),
            pl.BlockSpec((1, hidden, d), lambda i, be: (be[i], 0, 0)),
        ],
        out_specs=pl.BlockSpec((tm, d), lambda i, be: (i, 0)),
    )
    return pl.pallas_call(
        functools.partial(_expert_kernel, hidden=hidden),
        grid_spec=grid_spec,
        out_shape=jax.ShapeDtypeStruct((n_rows, d), F32),
        compiler_params=_params("arbitrary"),
        name="moe_experts",
    )(block_e, x_buf, w_gu, w_down)


def _route(logits, n_tok):
    g_prob = jax.nn.softmax(logits[:, :MOE_GROUPS], axis=-1)
    g_p, g_idx = lax.top_k(g_prob, 1)
    e_logits = logits[:, MOE_GROUPS:MOE_GROUPS + MOE_EXPERTS].reshape(n_tok, MOE_GROUPS, MOE_EPG)
    e_logits = jnp.take_along_axis(e_logits, g_idx[:, :, None], axis=1)[:, 0]
    e_p, e_idx = lax.top_k(jax.nn.softmax(e_logits, axis=-1), MOE_TOPK)
    e_p = e_p / jnp.sum(e_p, -1, keepdims=True)
    gate = g_p * e_p
    expert = (g_idx * MOE_EPG + e_idx).astype(jnp.int32)
    return expert, gate


def _hier_moe(x1b, logits, w_gu, w_down):
    n_tok, d = x1b.shape
    expert, gate = _route(logits, n_tok)
    n_assign = n_tok * MOE_TOPK
    e_flat = expert.reshape(n_assign)
    onehot = (e_flat[:, None] == jnp.arange(MOE_EXPERTS, dtype=jnp.int32)[None, :]).astype(jnp.int32)
    rank = jnp.sum((jnp.cumsum(onehot, axis=0) - onehot) * onehot, axis=1)
    counts = jnp.sum(onehot, axis=0)
    padded = (counts + MOE_BLOCK - 1) // MOE_BLOCK * MOE_BLOCK
    pad_end = jnp.cumsum(padded)
    pad_start = pad_end - padded
    dest = pad_start[e_flat] + rank
    n_blocks = n_assign // MOE_BLOCK + MOE_EXPERTS
    n_rows = n_blocks * MOE_BLOCK
    tok = jnp.repeat(jnp.arange(n_tok, dtype=jnp.int32), MOE_TOPK)
    tok_buf = jnp.full((n_rows,), n_tok, jnp.int32).at[dest].set(tok)
    x_pad = jnp.concatenate([x1b, jnp.zeros((1, d), x1b.dtype)], axis=0)
    x_buf = x_pad[tok_buf]
    block_e = jnp.minimum(
        jnp.searchsorted(pad_end, jnp.arange(n_blocks, dtype=jnp.int32) * MOE_BLOCK, side="right"),
        MOE_EXPERTS - 1).astype(jnp.int32)
    y = _experts(x_buf, block_e, w_gu, w_down)
    picked = y[dest].reshape(n_tok, MOE_TOPK, d) * gate[:, :, None]
    return jnp.sum(picked, axis=1)


def _rwkv_proj_kernel(*refs, seq, has_mix):
    if has_mix:
        (x_ref, xp_ref, mu_ref, wrkv_ref, w0_ref, w1_ref, w2_ref, a0_ref, a1_ref, a2_ref, g1_ref, g2_ref,
         v0_ref, v1_ref, v2_ref, vf_ref, r_ref, k_ref, v_ref, z_ref, a_ref, g_ref) = refs
    else:
        (x_ref, xp_ref, mu_ref, wrkv_ref, w0_ref, w1_ref, w2_ref, a0_ref, a1_ref, a2_ref, g1_ref, g2_ref,
         r_ref, k_ref, v_ref, z_ref, a_ref, g_ref) = refs
    x = x_ref[...]
    tm = x.shape[0]
    row = lax.broadcasted_iota(jnp.int32, x.shape, 0)
    at_seq_start = (pl.program_id(0) * tm) % seq == 0
    prev_last = jnp.where(at_seq_start, 0.0, xp_ref[7:8, :])
    shifted = jnp.where(row == 0, prev_last, pltpu.roll(x, 1, axis=0))
    xx = shifted - x

    def mixed(n):
        return (x + xx * mu_ref[n:n + 1, :]).astype(BF16)

    def sigmoid(t):
        return 1.0 / (1.0 + jnp.exp(-t))

    r_ref[...] = _dot(mixed(0), wrkv_ref[0])
    k_ref[...] = _dot(mixed(1), wrkv_ref[1])
    xv = mixed(2)
    v = _dot(xv, wrkv_ref[2])
    if has_mix:
        mix = sigmoid(v0_ref[...] + _dot(_dot(xv, v1_ref[...]).astype(BF16), v2_ref[...]))
        v = v + (vf_ref[...] - v) * mix
    v_ref[...] = v
    z_ref[...] = w0_ref[...] + _dot(jnp.tanh(_dot(mixed(3), w1_ref[...])).astype(BF16), w2_ref[...])
    a_ref[...] = sigmoid(a0_ref[...] + _dot(_dot(mixed(4), a1_ref[...]).astype(BF16), a2_ref[...]))
    g_ref[...] = _dot(sigmoid(_dot(mixed(5), g1_ref[...])).astype(BF16), g2_ref[...])


def _pad_cols(w, width):
    return jnp.pad(w, ((0, 0), (0, width - w.shape[1])))


def _pad_rows(w, height):
    return jnp.pad(w, ((0, height - w.shape[0]), (0, 0)))


def _rwkv_proj(x, seq, mu, w_rkv, w0, w1, w2, a0, a1, a2, g1, g2, value_mix, v_first):
    n, d = x.shape
    tm = ROW_TILE
    row = lambda i: (i, 0)
    lora = lambda w_in, w_out, width: (_pad_cols(w_in, width).astype(BF16), _pad_rows(w_out, width).astype(BF16))
    w1p, w2p = lora(w1, w2, LANES)
    a1p, a2p = lora(a1, a2, LANES)
    g1p, g2p = lora(g1, g2, 2 * LANES)
    mu8 = _pad_rows(mu, 8)
    ins = [x, x, mu8, w_rkv.astype(BF16), w0.reshape(1, d), w1p, w2p, a0.reshape(1, d), a1p, a2p, g1p, g2p]
    specs = [
        pl.BlockSpec((tm, d), row),
        pl.BlockSpec((8, d), lambda i: (jnp.maximum(i * (tm // 8) - 1, 0), 0)),
        _const_spec((8, d)), _const_spec((3, d, d)), _const_spec((1, d)),
        _const_spec((d, LANES)), _const_spec((LANES, d)), _const_spec((1, d)),
        _const_spec((d, LANES)), _const_spec((LANES, d)),
        _const_spec((d, 2 * LANES)), _const_spec((2 * LANES, d)),
    ]
    has_mix = value_mix is not None
    if has_mix:
        v0, v1, v2 = value_mix
        v1p, v2p = lora(v1, v2, LANES)
        ins += [v0.reshape(1, d), v1p, v2p, v_first]
        specs += [_const_spec((1, d)), _const_spec((d, LANES)), _const_spec((LANES, d)), pl.BlockSpec((tm, d), row)]
    out = jax.ShapeDtypeStruct((n, d), F32)
    return pl.pallas_call(
        functools.partial(_rwkv_proj_kernel, seq=seq, has_mix=has_mix),
        grid=(n // tm,),
        in_specs=specs,
        out_specs=[pl.BlockSpec((tm, d), row)] * 6,
        out_shape=[out] * 6,
        compiler_params=_params("parallel"),
        name="rwkv_proj",
    )(*ins)


def _wkv_kernel(r_ref, k_ref, v_ref, z_ref, a_ref, g_ref, kk_ref, ka_ref, rk_ref, lg_ref, lb_ref, o_ref, s_ref,
                *, tblock):
    c = WKV_CHUNK
    hn = RW_HEAD
    w = 2 * hn
    hi = lax.Precision.HIGHEST

    @pl.when(pl.program_id(2) == 0)
    def _():
        s_ref[...] = jnp.zeros_like(s_ref)

    def iota2(shape, dim):
        return lax.broadcasted_iota(jnp.int32, shape, dim)

    lane_c = iota2((c, w), 1)
    head0 = lane_c < hn
    rw, cw = iota2((w, w), 0), iota2((w, w), 1)
    same_head = (rw // hn) == (cw // hn)
    seg_ones = same_head.astype(F32)
    strict = same_head & ((cw % hn) < (rw % hn))
    incl = same_head & ((cw % hn) <= (rw % hn))
    blk16 = (rw // 16) == (cw // 16)
    blk32 = (rw // 32) == (cw // 32)
    eye = (rw == cw).astype(F32)
    tri = (iota2((c, c), 1) <= iota2((c, c), 0)).astype(F32)

    def seg_sum(t):
        return _dot(t, seg_ones, precision=hi)

    def stack(t):
        zero = jnp.zeros_like(t)
        return jnp.concatenate([jnp.where(head0, t, zero), jnp.where(head0, zero, t)], axis=0)

    def fold(t):
        return t[:c] + t[c:]

    def bf(t):
        return t.astype(BF16)

    kk_p, ka_p, rk_p, lg_p, lb_p = kk_ref[...], ka_ref[...], rk_ref[...], lg_ref[...], lb_ref[...]

    def chunk(ci, carry):
        t0 = pl.multiple_of(ci * c, c)
        sl = pl.ds(t0, c)
        r, k, v, z, a, g = (ref[0, sl, :] for ref in (r_ref, k_ref, v_ref, z_ref, a_ref, g_ref))
        kk = k * kk_p
        kk = kk / jnp.maximum(jnp.sqrt(seg_sum(kk * kk)), 1e-12)
        kmod = k * (1.0 + (a - 1.0) * ka_p)
        bvec = kk * a
        lw = (-math.exp(-0.5)) / (1.0 + jnp.exp(-z))
        cum = _dot(tri, lw, precision=hi)
        total = cum[c - 1:c, :]
        at = -kk * jnp.exp(cum - lw)
        rt = r * jnp.exp(cum)
        inv = jnp.exp(-cum)
        bt = bvec * inv
        kt = kmod * inv
        to_end = jnp.exp(total - cum)
        bh = bvec * to_end
        kh = kmod * to_end

        ast, rst, bst, kst = bf(stack(at)), bf(stack(rt)), bf(stack(bt)), bf(stack(kt))
        l_full = jnp.where(strict, _dot_nt(ast, bst), 0.0)
        a_ak = jnp.where(strict, _dot_nt(ast, kst), 0.0)
        a_rb = jnp.where(incl, _dot_nt(rst, bst), 0.0)
        a_rk = jnp.where(incl, _dot_nt(rst, kst), 0.0)

        x1 = jnp.where(blk16, l_full, 0.0)
        x1b = bf(x1)
        tm = eye + x1
        x2 = _dot(x1b, x1b)
        tm = tm + _dot(bf(tm), bf(x2))
        x4 = _dot(bf(x2), bf(x2))
        tm = tm + _dot(bf(tm), bf(x4))
        x8 = _dot(bf(x4), bf(x4))
        tm = tm + _dot(bf(tm), bf(x8))
        off = bf(jnp.where(blk32 & jnp.logical_not(blk16), l_full, 0.0))
        tmb = bf(tm)
        tm = tm + _dot(bf(_dot(tmb, off)), tmb)
        off = bf(jnp.where(jnp.logical_not(blk32), l_full, 0.0))
        tmb = bf(tm)
        tm = tm + _dot(bf(_dot(tmb, off)), tmb)

        s = s_ref[...]
        sb = bf(s)
        vst = bf(stack(v))
        xr = _dot_nt(bf(at), sb) + _dot(bf(fold(a_ak)), vst)
        u = _dot(bf(fold(tm)), bf(stack(xr)))
        y = _dot_nt(bf(rt), sb) + _dot(bf(fold(a_rb)), bf(stack(u))) + _dot(bf(fold(a_rk)), vst)
        upd = _dot_tn(bf(jnp.concatenate([u, v], axis=0)), bf(jnp.concatenate([bh, kh], axis=0)))
        s_ref[...] = s * jnp.exp(total) + jnp.where(same_head, upd, 0.0)

        mean = seg_sum(y) * (1.0 / hn)
        yc = y - mean
        var = seg_sum(yc * yc) * (1.0 / hn)
        yn = yc * lax.rsqrt(var + RW_GN_EPS) * lg_p + lb_p
        yn = yn + seg_sum(r * kmod * rk_p) * v
        o_ref[0, sl, :] = (yn * g).astype(o_ref.dtype)
        return carry

    lax.fori_loop(0, tblock // c, chunk, 0)


def _wkv(r, k, v, z, a, g, k_k, k_a, r_k, lnx_g, lnx_b, batch, seq):
    n, d = r.shape
    w = 2 * RW_HEAD
    tb = WKV_TBLOCK
    seq_spec = pl.BlockSpec((1, tb, w), lambda b, h, t: (b, t, h))
    par_spec = pl.BlockSpec((1, w), lambda b, h, t: (0, h))
    r3 = lambda t: t.reshape(batch, seq, d)
    out = pl.pallas_call(
        functools.partial(_wkv_kernel, tblock=tb),
        grid=(batch, d // w, seq // tb),
        in_specs=[seq_spec] * 6 + [par_spec] * 5,
        out_specs=seq_spec,
        out_shape=jax.ShapeDtypeStruct((batch, seq, d), BF16),
        scratch_shapes=[pltpu.VMEM((w, w), F32)],
        compiler_params=_params("parallel", "parallel", "arbitrary"),
        name="wkv7_chunked",
    )(r3(r), r3(k), r3(v), r3(z), r3(a), r3(g), k_k.reshape(1, d), k_a.reshape(1, d), r_k.reshape(1, d),
      lnx_g.reshape(1, d), lnx_b.reshape(1, d))
    return out.reshape(n, d)


def kernel(x, ln1_g, ln1_b, ln2_g, ln2_b, attn_w_qkv, attn_w_o, attn_lambda, attn_subln_g, rw_mu, rw_w_rkv, rw_w_o, rw_w0, rw_w1, rw_w2, rw_a0, rw_a1, rw_a2, rw_g1, rw_g2, rw_k_k, rw_k_a, rw_r_k, rw_lnx_g, rw_lnx_b, rw_v0, rw_v1, rw_v2, moe_rg_w, moe_rg_b, moe_re_w, moe_re_b, moe_w_gu, moe_w_down):
    batch, seq, d = x.shape
    depth = ln1_g.shape[0]
    n = batch * seq
    alpha = (2 * depth) ** 0.25
    x = x.reshape(n, d)
    v_first = None
    for i in range(depth):
        j = i // N_MIXERS
        if i % N_MIXERS == 0:
            lambda_init = 0.8 - 0.6 * math.exp(-0.3 * i)
            lam = attn_lambda[j]
            lam_full = jnp.exp(jnp.sum(lam[0] * lam[1])) - jnp.exp(jnp.sum(lam[2] * lam[3])) + lambda_init
            lam_row = jnp.full((1, 2 * DA_HEAD_DIM), lam_full, F32)
            col_scale = jnp.concatenate([jnp.full((d,), DA_HEAD_DIM ** -0.5, F32), jnp.ones((2 * d,), F32)])
            w_qkv = (attn_w_qkv[j] * col_scale[None, :]).astype(BF16)
            qkv = _proj(x, w_qkv, BF16)
            mixed = _diff_attention(qkv, lam_row, attn_subln_g[j], lambda_init, batch, seq)
            w_o = attn_w_o[j]
        else:
            value_mix = None if j == 0 else (rw_v0[j - 1], rw_v1[j - 1], rw_v2[j - 1])
            r, k, v, z, a, g = _rwkv_proj(x, seq, rw_mu[j], rw_w_rkv[j], rw_w0[j], rw_w1[j], rw_w2[j], rw_a0[j],
                                          rw_a1[j], rw_a2[j], rw_g1[j], rw_g2[j], value_mix, v_first)
            if value_mix is None:
                v_first = v
            mixed = _wkv(r, k, v, z, a, g, rw_k_k[j], rw_k_a[j], rw_r_k[j].reshape(d), rw_lnx_g[j], rw_lnx_b[j],
                         batch, seq)
            w_o = rw_w_o[j]
        n_router = MOE_GROUPS + MOE_EXPERTS
        w_router = _pad_cols(jnp.concatenate([moe_rg_w[i], moe_re_w[i]], axis=1), LANES)
        b_router = _pad_cols(jnp.concatenate([moe_rg_b[i], moe_re_b[i]]).reshape(1, n_router), LANES)
        x1, x1b, logits = _post_mixer(mixed, x, w_o.astype(BF16), ln1_g[i], ln1_b[i], w_router, b_router, alpha)
        f = _hier_moe(x1b, logits, moe_w_gu[i].astype(BF16), moe_w_down[i].astype(BF16))
        x = _post_moe(f, x1, ln2_g[i], ln2_b[i], alpha)
    return x.reshape(batch, seq, d)
```

```python
import functools
import math

import jax
import jax.numpy as jnp
from jax import lax
from jax.experimental import pallas as pl
from jax.experimental.pallas import tpu as pltpu

F32 = jnp.float32
BF16 = jnp.bfloat16

LANES = 128
VMEM_LIMIT = 56 * 1024 * 1024

DA_HEADS = 8
DA_HEAD_DIM = 64
RMS_EPS = 1e-5
RW_HEAD = 64
RW_GN_EPS = 64e-5
MOE_GROUPS = 4
MOE_EPG = 8
MOE_EXPERTS = MOE_GROUPS * MOE_EPG
MOE_TOPK = 2
LN_EPS = 1e-5
N_MIXERS = 2

ROW_TILE = 256
ROUTER_TILE = 512
ATTN_TQ = 256
ATTN_TK = 512
WKV_CHUNK = 64
WKV_TBLOCK = 256
MOE_BLOCK = 256
NEG_BIG = -1e30


def _dot(a, b, precision=None):
    return jnp.dot(a, b, preferred_element_type=F32, precision=precision)


def _dot_nt(a, b):
    return lax.dot_general(a, b, (((1,), (1,)), ((), ())), preferred_element_type=F32)


def _dot_tn(a, b):
    return lax.dot_general(a, b, (((0,), (0,)), ((), ())), preferred_element_type=F32)


def _params(*sem):
    return pltpu.CompilerParams(dimension_semantics=sem, vmem_limit_bytes=VMEM_LIMIT)


def _const_spec(shape):
    nd = len(shape)
    return pl.BlockSpec(shape, lambda *_: (0,) * nd)


def _proj_kernel(x_ref, w_ref, o_ref):
    o_ref[...] = _dot(x_ref[...].astype(BF16), w_ref[...]).astype(o_ref.dtype)


def _proj(x, w, out_dtype):
    n, d = x.shape
    m = w.shape[1]
    tm = 512
    return pl.pallas_call(
        _proj_kernel,
        grid=(n // tm,),
        in_specs=[pl.BlockSpec((tm, d), lambda i: (i, 0)), _const_spec((d, m))],
        out_specs=pl.BlockSpec((tm, m), lambda i: (i, 0)),
        out_shape=jax.ShapeDtypeStruct((n, m), out_dtype),
        compiler_params=_params("parallel"),
        name="qkv_proj",
    )(x, w)


def _attn_kernel(lam_ref, g_ref, q_ref, k_ref, v_ref, o_ref, vx_ref, m_ref, acc_ref, sa_ref, sb_ref, *, tq, tk,
                 out_scale):
    qi = pl.program_id(2)
    hd = DA_HEAD_DIM
    hw = 2 * hd

    @pl.when(qi == 0)
    def _():
        vx_ref[:, :hw] = v_ref[0]
        vx_ref[:, hw:] = jnp.ones((vx_ref.shape[0], hw), BF16)

    q = q_ref[0]
    lane = lax.broadcasted_iota(jnp.int32, q.shape, 1)
    zero = jnp.zeros_like(q)
    q_stack = jnp.concatenate([jnp.where(lane < hd, q, zero), jnp.where(lane >= hd, q, zero)], axis=0)
    m_ref[...] = jnp.full(m_ref.shape, NEG_BIG, F32)
    acc_ref[...] = jnp.zeros(acc_ref.shape, F32)
    row0 = qi * tq

    def scores(j, s_ref):
        start = pl.multiple_of(j * tk, tk)
        s_ref[...] = _dot_nt(q_stack, k_ref[0, pl.ds(start, tk), :])

    def consume(j, s_ref, masked):
        start = pl.multiple_of(j * tk, tk)
        vb = vx_ref[pl.ds(start, tk), :]
        if masked:
            row = row0 + lax.broadcasted_iota(jnp.int32, (tq, tk), 0)
            col = start + lax.broadcasted_iota(jnp.int32, (tq, tk), 1)
            keep = col <= row
        for c in range(2):
            s = s_ref[c * tq:(c + 1) * tq, :]
            if masked:
                s = jnp.where(keep, s, NEG_BIG)
            chunks = [s[:, i * LANES:(i + 1) * LANES] for i in range(tk // LANES)]
            m_old = m_ref[c]
            m_new = jnp.maximum(m_old, jnp.max(functools.reduce(jnp.maximum, chunks), axis=-1, keepdims=True))
            alpha = jnp.exp2(m_old - m_new)
            m_ref[c] = m_new
            p = jnp.concatenate([jnp.exp2(ch - m_new).astype(BF16) for ch in chunks], axis=1)
            acc_ref[c] = acc_ref[c] * jnp.concatenate([alpha, alpha], axis=1) + _dot(p, vb)

    n = row0 // tk + 1
    n_pairs = (n - 1) // 2
    scores(0, sa_ref)

    def pair(jj, carry):
        j = 2 * jj
        scores(j + 1, sb_ref)
        consume(j, sa_ref, False)
        scores(j + 2, sa_ref)
        consume(j + 1, sb_ref, False)
        return carry

    lax.fori_loop(0, n_pairs, pair, 0)
    jb = 2 * n_pairs

    @pl.when(n % 2 == 1)
    def _():
        consume(jb, sa_ref, True)

    @pl.when(n % 2 == 0)
    def _():
        scores(jb + 1, sb_ref)
        consume(jb, sa_ref, False)
        consume(jb + 1, sb_ref, True)

    a0, a1 = acc_ref[0], acc_ref[1]
    o = a0[:, :hw] * (1.0 / a0[:, hw:]) - lam_ref[...] * (a1[:, :hw] * (1.0 / a1[:, hw:]))
    ms = jnp.mean(o * o, axis=-1, keepdims=True)
    o = o * lax.rsqrt(ms + RMS_EPS) * g_ref[...] * out_scale
    o_ref[0] = o.astype(o_ref.dtype)


def _diff_attention(qkv, lam_row, subln_g, lambda_init, batch, seq):
    d = DA_HEADS * 2 * DA_HEAD_DIM
    qkv = qkv.reshape(batch, seq, 3 * d)
    tq, tk = ATTN_TQ, ATTN_TK
    assert tq <= tk and tk % tq == 0 and seq % tk == 0
    hw = 2 * DA_HEAD_DIM
    out = pl.pallas_call(
        functools.partial(_attn_kernel, tq=tq, tk=tk, out_scale=1.0 - lambda_init),
        grid=(batch, DA_HEADS, seq // tq),
        in_specs=[
            _const_spec((1, hw)),
            _const_spec((1, hw)),
            pl.BlockSpec((1, tq, hw), lambda b, h, i: (b, i, h)),
            pl.BlockSpec((1, seq, hw), lambda b, h, i: (b, 0, DA_HEADS + h)),
            pl.BlockSpec((1, seq, hw), lambda b, h, i: (b, 0, 2 * DA_HEADS + h)),
        ],
        out_specs=pl.BlockSpec((1, tq, hw), lambda b, h, i: (b, i, h)),
        out_shape=jax.ShapeDtypeStruct((batch, seq, d), BF16),
        scratch_shapes=[pltpu.VMEM((seq, 2 * hw), BF16), pltpu.VMEM((2, tq, hw), F32),
                        pltpu.VMEM((2, tq, 2 * hw), F32), pltpu.VMEM((2 * tq, tk), F32),
                        pltpu.VMEM((2 * tq, tk), F32)],
        compiler_params=_params("parallel", "parallel", "arbitrary"),
        name="diff_attn",
    )(lam_row, subln_g.reshape(1, hw), qkv, qkv, qkv)
    return out.reshape(batch * seq, d)


def _layer_norm_rows(z, g, b):
    mu = jnp.mean(z, axis=-1, keepdims=True)
    zc = z - mu
    var = jnp.mean(zc * zc, axis=-1, keepdims=True)
    return zc * lax.rsqrt(var + LN_EPS) * g + b


def _post_mixer_kernel(o_ref, x_ref, w_ref, lg_ref, lb_ref, wr_ref, br_ref, x1_ref, lo_ref, *, alpha):
    h = _dot(o_ref[...], w_ref[...])
    x1 = _layer_norm_rows(alpha * x_ref[...] + h, lg_ref[...], lb_ref[...])
    x1_ref[...] = x1
    lo_ref[...] = _dot(x1, wr_ref[...], precision=lax.Precision.HIGHEST) + br_ref[...]


def _post_mixer(o, x, w_o, ln_g, ln_b, w_router, b_router, alpha):
    n, d = x.shape
    tm = ROW_TILE
    row = lambda i: (i, 0)
    return pl.pallas_call(
        functools.partial(_post_mixer_kernel, alpha=alpha),
        grid=(n // tm,),
        in_specs=[
            pl.BlockSpec((tm, d), row), pl.BlockSpec((tm, d), row), _const_spec((d, d)),
            _const_spec((1, d)), _const_spec((1, d)), _const_spec((d, LANES)), _const_spec((1, LANES)),
        ],
        out_specs=[pl.BlockSpec((tm, d), row), pl.BlockSpec((tm, LANES), row)],
        out_shape=[jax.ShapeDtypeStruct((n, d), F32), jax.ShapeDtypeStruct((n, LANES), F32)],
        compiler_params=_params("parallel"),
        name="post_mixer",
    )(o, x, w_o, ln_g.reshape(1, d), ln_b.reshape(1, d), w_router, b_router)


def _router_kernel(lo_ref, idx_ref, gate_ref, cnt_ref, carry_ref):
    step = pl.program_id(0)

    @pl.when(step == 0)
    def _():
        carry_ref[...] = jnp.zeros_like(carry_ref)

    lo = lo_ref[...]
    tm = lo.shape[0]
    lane = lax.broadcasted_iota(jnp.int32, lo.shape, 1)
    big = jnp.int32(LANES)

    def softmax_over(mask):
        mx = jnp.max(jnp.where(mask, lo, NEG_BIG), axis=-1, keepdims=True)
        ex = jnp.where(mask, jnp.exp(lo - mx), 0.0)
        return ex / jnp.sum(ex, axis=-1, keepdims=True)

    def top1(p, mask):
        best = jnp.max(jnp.where(mask, p, -1.0), axis=-1, keepdims=True)
        arg = jnp.min(jnp.where(mask & (p == best), lane, big), axis=-1, keepdims=True)
        return best, arg

    g_mask = lane < MOE_GROUPS
    g_p, g_idx = top1(softmax_over(g_mask), g_mask)
    first = MOE_GROUPS + MOE_EPG * g_idx
    e_mask = (lane >= first) & (lane < first + MOE_EPG)
    e_prob = softmax_over(e_mask)
    p1, i1 = top1(e_prob, e_mask)
    rest = e_mask & (lane != i1)
    p2, i2 = top1(e_prob, rest)
    denom = p1 + p2
    gate1 = g_p * (p1 / denom)
    gate2 = g_p * (p2 / denom)
    e1 = i1 - MOE_GROUPS
    e2 = i2 - MOE_GROUPS

    oh1 = (lane == e1).astype(F32)
    oh2 = (lane == e2).astype(F32)
    both = oh1 + oh2
    r_i = lax.broadcasted_iota(jnp.int32, (tm, tm), 0)
    c_i = lax.broadcasted_iota(jnp.int32, (tm, tm), 1)
    before = (c_i < r_i).astype(BF16)
    seen = _dot(before, both.astype(BF16)) + carry_ref[...]
    pos1 = jnp.sum(seen * oh1, axis=-1, keepdims=True).astype(jnp.int32)
    pos2 = jnp.sum(seen * oh2, axis=-1, keepdims=True).astype(jnp.int32)
    carry_ref[...] += jnp.sum(both, axis=0, keepdims=True)
    cnt_ref[...] = carry_ref[...].astype(jnp.int32)

    zero_i = jnp.zeros(lo.shape, jnp.int32)
    idx_ref[...] = (jnp.where(lane == 0, e1, zero_i) + jnp.where(lane == 1, e2, zero_i)
                    + jnp.where(lane == 2, pos1, zero_i) + jnp.where(lane == 3, pos2, zero_i))
    gate_ref[...] = jnp.where(lane == 0, gate1, 0.0) + jnp.where(lane == 1, gate2, 0.0)


def _router(logits):
    n = logits.shape[0]
    tm = ROUTER_TILE
    row = lambda i: (i, 0)
    return pl.pallas_call(
        _router_kernel,
        grid=(n // tm,),
        in_specs=[pl.BlockSpec((tm, LANES), row)],
        out_specs=[pl.BlockSpec((tm, LANES), row), pl.BlockSpec((tm, LANES), row), _const_spec((1, LANES))],
        out_shape=[jax.ShapeDtypeStruct((n, LANES), jnp.int32), jax.ShapeDtypeStruct((n, LANES), F32),
                   jax.ShapeDtypeStruct((1, LANES), jnp.int32)],
        scratch_shapes=[pltpu.VMEM((1, LANES), F32)],
        compiler_params=_params("arbitrary"),
        name="moe_router",
    )(logits)


def _row_copy(src_ref, src_row, dst_ref, dst_row, sem):
    return pltpu.make_async_copy(src_ref.at[pl.ds(src_row, 1)], dst_ref.at[pl.ds(dst_row, 1)], sem)


def _dispatch_kernel(dest_ref, x_ref, buf_in_ref, buf_ref, sem):
    del buf_in_ref
    tm = x_ref.shape[0]

    def copies(r):
        return [_row_copy(x_ref, r, buf_ref, dest_ref[0, 0, MOE_TOPK * r + s], sem) for s in range(MOE_TOPK)]

    def start(r, carry):
        for cp in copies(r):
            cp.start()
        return carry

    def wait(r, carry):
        for cp in copies(r):
            cp.wait()
        return carry

    lax.fori_loop(0, tm, start, 0)
    lax.fori_loop(0, tm, wait, 0)


def _dispatch(x1, dest_tiles, n_rows):
    n, d = x1.shape
    tm = ROW_TILE
    zeros = jnp.zeros((n_rows, d), x1.dtype)
    return pl.pallas_call(
        _dispatch_kernel,
        grid=(n // tm,),
        in_specs=[
            pl.BlockSpec((1, 1, MOE_TOPK * tm), lambda i: (i, 0, 0), memory_space=pltpu.SMEM),
            pl.BlockSpec((tm, d), lambda i: (i, 0)),
            pl.BlockSpec(memory_space=pl.ANY),
        ],
        out_specs=pl.BlockSpec(memory_space=pl.ANY),
        out_shape=jax.ShapeDtypeStruct((n_rows, d), x1.dtype),
        scratch_shapes=[pltpu.SemaphoreType.DMA(())],
        input_output_aliases={2: 0},
        compiler_params=_params("arbitrary"),
        name="moe_dispatch",
    )(dest_tiles, x1, zeros)


def _expert_kernel(be_ref, x_ref, wgu_ref, wd_ref, y_ref, *, hidden):
    del be_ref
    h = _dot(x_ref[...].astype(BF16), wgu_ref[0])
    hg = h[:, :hidden]
    hu = h[:, hidden:]
    act = hg * (1.0 / (1.0 + jnp.exp(-hg))) * hu
    y_ref[...] = _dot(act.astype(BF16), wd_ref[0])


def _experts(x_buf, block_e, w_gu, w_down):
    n_rows, d = x_buf.shape
    hidden = w_down.shape[1]
    tm = MOE_BLOCK
    grid_spec = pltpu.PrefetchScalarGridSpec(
        num_scalar_prefetch=1,
        grid=(n_rows // tm,),
        in_specs=[
            pl.BlockSpec((tm, d), lambda i, be: (i, 0)),
            pl.BlockSpec((1, d, 2 * hidden), lambda i, be: (be[i], 0, 0)),
            pl.BlockSpec((1, hidden, d), lambda i, be: (be[i], 0, 0)),
        ],
        out_specs=pl.BlockSpec((tm, d), lambda i, be: (i, 0)),
    )
    return pl.pallas_call(
        functools.partial(_expert_kernel, hidden=hidden),
        grid_spec=grid_spec,
        out_shape=jax.ShapeDtypeStruct((n_rows, d), F32),
        compiler_params=_params("arbitrary"),
        name="moe_experts",
    )(block_e, x_buf, w_gu, w_down)


def _combine_kernel(dest_ref, y_ref, gate_ref, x_ref, lg_ref, lb_ref, o_ref, rows_ref, sem, *, alpha):
    tm = x_ref.shape[0]

    def copies(r):
        return [_row_copy(y_ref, dest_ref[0, 0, MOE_TOPK * r + s], rows_ref.at[s], r, sem)
                for s in range(MOE_TOPK)]

    def start(r, carry):
        for cp in copies(r):
            cp.start()
        return carry

    def wait(r, carry):
        for cp in copies(r):
            cp.wait()
        return carry

    lax.fori_loop(0, tm, start, 0)
    lax.fori_loop(0, tm, wait, 0)
    gate = gate_ref[...]
    f = sum(gate[:, s:s + 1] * rows_ref[s] for s in range(MOE_TOPK))
    o_ref[...] = _layer_norm_rows(alpha * x_ref[...] + f, lg_ref[...], lb_ref[...])


def _combine(y, dest_tiles, gates, x1, ln_g, ln_b, alpha):
    n, d = x1.shape
    tm = ROW_TILE
    row = lambda i: (i, 0)
    return pl.pallas_call(
        functools.partial(_combine_kernel, alpha=alpha),
        grid=(n // tm,),
        in_specs=[
            pl.BlockSpec((1, 1, MOE_TOPK * tm), lambda i: (i, 0, 0), memory_space=pltpu.SMEM),
            pl.BlockSpec(memory_space=pl.ANY),
            pl.BlockSpec((tm, LANES), row), pl.BlockSpec((tm, d), row), _const_spec((1, d)), _const_spec((1, d)),
        ],
        out_specs=pl.BlockSpec((tm, d), row),
        out_shape=jax.ShapeDtypeStruct((n, d), F32),
        scratch_shapes=[pltpu.VMEM((MOE_TOPK, tm, d), F32), pltpu.SemaphoreType.DMA(())],
        compiler_params=_params("arbitrary"),
        name="moe_combine",
    )(dest_tiles, y, gates, x1, ln_g.reshape(1, d), ln_b.reshape(1, d))


def _hier_moe_block(x1, logits, w_gu, w_down, ln_g, ln_b, alpha):
    n_tok, d = x1.shape
    idx, gates, counts = _router(logits)
    counts = counts[0, :MOE_EXPERTS]
    padded = (counts + MOE_BLOCK - 1) // MOE_BLOCK * MOE_BLOCK
    pad_end = jnp.cumsum(padded)
    pad_start = pad_end - padded
    expert = idx[:, :MOE_TOPK]
    pos = idx[:, MOE_TOPK:2 * MOE_TOPK]
    e_iota = jnp.arange(MOE_EXPERTS, dtype=jnp.int32)
    base = jnp.sum(jnp.where(expert[:, :, None] == e_iota, pad_start, 0), axis=-1)
    dest = (base + pos).astype(jnp.int32)
    n_blocks = n_tok * MOE_TOPK // MOE_BLOCK + MOE_EXPERTS
    n_rows = n_blocks * MOE_BLOCK
    block_start = jnp.arange(n_blocks, dtype=jnp.int32) * MOE_BLOCK
    block_e = jnp.minimum(jnp.sum((pad_end[None, :] <= block_start[:, None]).astype(jnp.int32), axis=1),
                          MOE_EXPERTS - 1)
    dest_tiles = dest.reshape(n_tok // ROW_TILE, 1, MOE_TOPK * ROW_TILE)
    x_buf = _dispatch(x1, dest_tiles, n_rows)
    y = _experts(x_buf, block_e, w_gu, w_down)
    return _combine(y, dest_tiles, gates, x1, ln_g, ln_b, alpha)


def _rwkv_proj_kernel(*refs, seq, has_mix):
    if has_mix:
        (x_ref, xp_ref, mu_ref, wrkv_ref, w0_ref, w1_ref, w2_ref, a0_ref, a1_ref, a2_ref, g1_ref, g2_ref,
         v0_ref, v1_ref, v2_ref, vf_ref, r_ref, k_ref, v_ref, z_ref, a_ref, g_ref) = refs
    else:
        (x_ref, xp_ref, mu_ref, wrkv_ref, w0_ref, w1_ref, w2_ref, a0_ref, a1_ref, a2_ref, g1_ref, g2_ref,
         r_ref, k_ref, v_ref, z_ref, a_ref, g_ref) = refs
    x = x_ref[...]
    tm = x.shape[0]
    row = lax.broadcasted_iota(jnp.int32, x.shape, 0)
    at_seq_start = (pl.program_id(0) * tm) % seq == 0
    prev_last = jnp.where(at_seq_start, 0.0, xp_ref[7:8, :])
    shifted = jnp.where(row == 0, prev_last, pltpu.roll(x, 1, axis=0))
    xx = shifted - x

    def mixed(n):
        return (x + xx * mu_ref[n:n + 1, :]).astype(BF16)

    def sigmoid(t):
        return 1.0 / (1.0 + jnp.exp(-t))

    r_ref[...] = _dot(mixed(0), wrkv_ref[0])
    k_ref[...] = _dot(mixed(1), wrkv_ref[1])
    xv = mixed(2)
    v = _dot(xv, wrkv_ref[2])
    if has_mix:
        mix = sigmoid(v0_ref[...] + _dot(_dot(xv, v1_ref[...]).astype(BF16), v2_ref[...]))
        v = v + (vf_ref[...] - v) * mix
    v_ref[...] = v
    z_ref[...] = w0_ref[...] + _dot(jnp.tanh(_dot(mixed(3), w1_ref[...])).astype(BF16), w2_ref[...])
    a_ref[...] = sigmoid(a0_ref[...] + _dot(_dot(mixed(4), a1_ref[...]).astype(BF16), a2_ref[...]))
    g_ref[...] = _dot(sigmoid(_dot(mixed(5), g1_ref[...])).astype(BF16), g2_ref[...])


def _pad_cols(w, width):
    return jnp.pad(w, ((0, 0), (0, width - w.shape[1])))


def _pad_rows(w, height):
    return jnp.pad(w, ((0, height - w.shape[0]), (0, 0)))


def _rwkv_proj(x, seq, mu, w_rkv, w0, w1, w2, a0, a1, a2, g1, g2, value_mix, v_first):
    n, d = x.shape
    tm = ROW_TILE
    row = lambda i: (i, 0)
    lora = lambda w_in, w_out, width: (_pad_cols(w_in, width).astype(BF16), _pad_rows(w_out, width).astype(BF16))
    w1p, w2p = lora(w1, w2, LANES)
    a1p, a2p = lora(a1, a2, LANES)
    g1p, g2p = lora(g1, g2, 2 * LANES)
    mu8 = _pad_rows(mu, 8)
    ins = [x, x, mu8, w_rkv.astype(BF16), w0.reshape(1, d), w1p, w2p, a0.reshape(1, d), a1p, a2p, g1p, g2p]
    specs = [
        pl.BlockSpec((tm, d), row),
        pl.BlockSpec((8, d), lambda i: (jnp.maximum(i * (tm // 8) - 1, 0), 0)),
        _const_spec((8, d)), _const_spec((3, d, d)), _const_spec((1, d)),
        _const_spec((d, LANES)), _const_spec((LANES, d)), _const_spec((1, d)),
        _const_spec((d, LANES)), _const_spec((LANES, d)),
        _const_spec((d, 2 * LANES)), _const_spec((2 * LANES, d)),
    ]
    has_mix = value_mix is not None
    if has_mix:
        v0, v1, v2 = value_mix
        v1p, v2p = lora(v1, v2, LANES)
        ins += [v0.reshape(1, d), v1p, v2p, v_first]
        specs += [_const_spec((1, d)), _const_spec((d, LANES)), _const_spec((LANES, d)), pl.BlockSpec((tm, d), row)]
    out = jax.ShapeDtypeStruct((n, d), F32)
    return pl.pallas_call(
        functools.partial(_rwkv_proj_kernel, seq=seq, has_mix=has_mix),
        grid=(n // tm,),
        in_specs=specs,
        out_specs=[pl.BlockSpec((tm, d), row)] * 6,
        out_shape=[out] * 6,
        compiler_params=_params("parallel"),
        name="rwkv_proj",
    )(*ins)


def _wkv_kernel(r_ref, k_ref, v_ref, z_ref, a_ref, g_ref, kk_ref, ka_ref, rk_ref, lg_ref, lb_ref, o_ref, s_ref,
                *, tblock):
    c = WKV_CHUNK
    hn = RW_HEAD
    w = 2 * hn
    hi = lax.Precision.HIGHEST

    @pl.when(pl.program_id(2) == 0)
    def _():
        s_ref[...] = jnp.zeros_like(s_ref)

    def iota2(shape, dim):
        return lax.broadcasted_iota(jnp.int32, shape, dim)

    lane_c = iota2((c, w), 1)
    head0 = lane_c < hn
    rw, cw = iota2((w, w), 0), iota2((w, w), 1)
    same_head = (rw // hn) == (cw // hn)
    seg_ones = same_head.astype(F32)
    strict = same_head & ((cw % hn) < (rw % hn))
    incl = same_head & ((cw % hn) <= (rw % hn))
    blk16 = (rw // 16) == (cw // 16)
    blk32 = (rw // 32) == (cw // 32)
    eye = (rw == cw).astype(F32)
    tri = (iota2((c, c), 1) <= iota2((c, c), 0)).astype(F32)

    def seg_sum(t):
        return _dot(t, seg_ones, precision=hi)

    def stack(t):
        zero = jnp.zeros_like(t)
        return jnp.concatenate([jnp.where(head0, t, zero), jnp.where(head0, zero, t)], axis=0)

    def fold(t):
        return t[:c] + t[c:]

    def bf(t):
        return t.astype(BF16)

    kk_p, ka_p, rk_p, lg_p, lb_p = kk_ref[...], ka_ref[...], rk_ref[...], lg_ref[...], lb_ref[...]

    def chunk(ci, carry):
        t0 = pl.multiple_of(ci * c, c)
        sl = pl.ds(t0, c)
        r, k, v, z, a, g = (ref[0, sl, :] for ref in (r_ref, k_ref, v_ref, z_ref, a_ref, g_ref))
        kk = k * kk_p
        kk = kk / jnp.maximum(jnp.sqrt(seg_sum(kk * kk)), 1e-12)
        kmod = k * (1.0 + (a - 1.0) * ka_p)
        bvec = kk * a
        lw = (-math.exp(-0.5)) / (1.0 + jnp.exp(-z))
        cum = _dot(tri, lw, precision=hi)
        total = cum[c - 1:c, :]
        at = -kk * jnp.exp(cum - lw)
        rt = r * jnp.exp(cum)
        inv = jnp.exp(-cum)
        bt = bvec * inv
        kt = kmod * inv
        to_end = jnp.exp(total - cum)
        bh = bvec * to_end
        kh = kmod * to_end

        ast, rst, bst, kst = bf(stack(at)), bf(stack(rt)), bf(stack(bt)), bf(stack(kt))
        l_full = jnp.where(strict, _dot_nt(ast, bst), 0.0)
        a_ak = jnp.where(strict, _dot_nt(ast, kst), 0.0)
        a_rb = jnp.where(incl, _dot_nt(rst, bst), 0.0)
        a_rk = jnp.where(incl, _dot_nt(rst, kst), 0.0)

        x1 = jnp.where(blk16, l_full, 0.0)
        x1b = bf(x1)
        tm = eye + x1
        x2 = _dot(x1b, x1b)
        tm = tm + _dot(bf(tm), bf(x2))
        x4 = _dot(bf(x2), bf(x2))
        tm = tm + _dot(bf(tm), bf(x4))
        x8 = _dot(bf(x4), bf(x4))
        tm = tm + _dot(bf(tm), bf(x8))
        off = bf(jnp.where(blk32 & jnp.logical_not(blk16), l_full, 0.0))
        tmb = bf(tm)
        tm = tm + _dot(bf(_dot(tmb, off)), tmb)
        off = bf(jnp.where(jnp.logical_not(blk32), l_full, 0.0))
        tmb = bf(tm)
        tm = tm + _dot(bf(_dot(tmb, off)), tmb)

        s = s_ref[...]
        sb = bf(s)
        vst = bf(stack(v))
        xr = _dot_nt(bf(at), sb) + _dot(bf(fold(a_ak)), vst)
        u = _dot(bf(fold(tm)), bf(stack(xr)))
        y = _dot_nt(bf(rt), sb) + _dot(bf(fold(a_rb)), bf(stack(u))) + _dot(bf(fold(a_rk)), vst)
        upd = _dot_tn(bf(jnp.concatenate([u, v], axis=0)), bf(jnp.concatenate([bh, kh], axis=0)))
        s_ref[...] = s * jnp.exp(total) + jnp.where(same_head, upd, 0.0)

        mean = seg_sum(y) * (1.0 / hn)
        yc = y - mean
        var = seg_sum(yc * yc) * (1.0 / hn)
        yn = yc * lax.rsqrt(var + RW_GN_EPS) * lg_p + lb_p
        yn = yn + seg_sum(r * kmod * rk_p) * v
        o_ref[0, sl, :] = (yn * g).astype(o_ref.dtype)
        return carry

    lax.fori_loop(0, tblock // c, chunk, 0)


def _wkv(r, k, v, z, a, g, k_k, k_a, r_k, lnx_g, lnx_b, batch, seq):
    n, d = r.shape
    w = 2 * RW_HEAD
    tb = WKV_TBLOCK
    seq_spec = pl.BlockSpec((1, tb, w), lambda b, h, t: (b, t, h))
    par_spec = pl.BlockSpec((1, w), lambda b, h, t: (0, h))
    r3 = lambda t: t.reshape(batch, seq, d)
    out = pl.pallas_call(
        functools.partial(_wkv_kernel, tblock=tb),
        grid=(batch, d // w, seq // tb),
        in_specs=[seq_spec] * 6 + [par_spec] * 5,
        out_specs=seq_spec,
        out_shape=jax.ShapeDtypeStruct((batch, seq, d), BF16),
        scratch_shapes=[pltpu.VMEM((w, w), F32)],
        compiler_params=_params("parallel", "parallel", "arbitrary"),
        name="wkv7_chunked",
    )(r3(r), r3(k), r3(v), r3(z), r3(a), r3(g), k_k.reshape(1, d), k_a.reshape(1, d), r_k.reshape(1, d),
      lnx_g.reshape(1, d), lnx_b.reshape(1, d))
    return out.reshape(n, d)


def kernel(x, ln1_g, ln1_b, ln2_g, ln2_b, attn_w_qkv, attn_w_o, attn_lambda, attn_subln_g, rw_mu, rw_w_rkv, rw_w_o, rw_w0, rw_w1, rw_w2, rw_a0, rw_a1, rw_a2, rw_g1, rw_g2, rw_k_k, rw_k_a, rw_r_k, rw_lnx_g, rw_lnx_b, rw_v0, rw_v1, rw_v2, moe_rg_w, moe_rg_b, moe_re_w, moe_re_b, moe_w_gu, moe_w_down):
    batch, seq, d = x.shape
    depth = ln1_g.shape[0]
    n = batch * seq
    alpha = (2 * depth) ** 0.25
    x = x.reshape(n, d)
    v_first = None
    for i in range(depth):
        j = i // N_MIXERS
        if i % N_MIXERS == 0:
            lambda_init = 0.8 - 0.6 * math.exp(-0.3 * i)
            lam = attn_lambda[j]
            lam_full = jnp.exp(jnp.sum(lam[0] * lam[1])) - jnp.exp(jnp.sum(lam[2] * lam[3])) + lambda_init
            lam_row = jnp.full((1, 2 * DA_HEAD_DIM), lam_full, F32)
            q_scale = DA_HEAD_DIM ** -0.5 * math.log2(math.e)
            col_scale = jnp.concatenate([jnp.full((d,), q_scale, F32), jnp.ones((2 * d,), F32)])
            w_qkv = (attn_w_qkv[j] * col_scale[None, :]).astype(BF16)
            qkv = _proj(x, w_qkv, BF16)
            mixed = _diff_attention(qkv, lam_row, attn_subln_g[j], lambda_init, batch, seq)
            w_o = attn_w_o[j]
        else:
            value_mix = None if j == 0 else (rw_v0[j - 1], rw_v1[j - 1], rw_v2[j - 1])
            r, k, v, z, a, g = _rwkv_proj(x, seq, rw_mu[j], rw_w_rkv[j], rw_w0[j], rw_w1[j], rw_w2[j], rw_a0[j],
                                          rw_a1[j], rw_a2[j], rw_g1[j], rw_g2[j], value_mix, v_first)
            if value_mix is None:
                v_first = v
            mixed = _wkv(r, k, v, z, a, g, rw_k_k[j], rw_k_a[j], rw_r_k[j].reshape(d), rw_lnx_g[j], rw_lnx_b[j],
                         batch, seq)
            w_o = rw_w_o[j]
        n_router = MOE_GROUPS + MOE_EXPERTS
        w_router = _pad_cols(jnp.concatenate([moe_rg_w[i], moe_re_w[i]], axis=1), LANES)
        b_router = _pad_cols(jnp.concatenate([moe_rg_b[i], moe_re_b[i]]).reshape(1, n_router), LANES)
        x1, logits = _post_mixer(mixed, x, w_o.astype(BF16), ln1_g[i], ln1_b[i], w_router, b_router, alpha)
        x = _hier_moe_block(x1, logits, moe_w_gu[i].astype(BF16), moe_w_down[i].astype(BF16), ln2_g[i], ln2_b[i],
                            alpha)
    return x.reshape(batch, seq, d)
```

```python
import functools
import math

import jax
import jax.numpy as jnp
from jax import lax
from jax.experimental import pallas as pl
from jax.experimental.pallas import tpu as pltpu

F32 = jnp.float32
BF16 = jnp.bfloat16

LANES = 128
VMEM_LIMIT = 56 * 1024 * 1024

DA_HEADS = 8
DA_HEAD_DIM = 64
RMS_EPS = 1e-5
RW_HEAD = 64
RW_GN_EPS = 64e-5
MOE_GROUPS = 4
MOE_EPG = 8
MOE_EXPERTS = MOE_GROUPS * MOE_EPG
MOE_TOPK = 2
LN_EPS = 1e-5
N_MIXERS = 2

ROW_TILE = 256
ROUTER_TILE = 512
ATTN_TQ = 256
ATTN_TK = 512
WKV_CHUNK = 64
WKV_TBLOCK = 256
MOE_BLOCK = 256
NEG_BIG = -1e30


def _dot(a, b, precision=None):
    return jnp.dot(a, b, preferred_element_type=F32, precision=precision)


def _dot_nt(a, b):
    return lax.dot_general(a, b, (((1,), (1,)), ((), ())), preferred_element_type=F32)


def _dot_tn(a, b):
    return lax.dot_general(a, b, (((0,), (0,)), ((), ())), preferred_element_type=F32)


def _params(*sem):
    return pltpu.CompilerParams(dimension_semantics=sem, vmem_limit_bytes=VMEM_LIMIT)


def _const_spec(shape):
    nd = len(shape)
    return pl.BlockSpec(shape, lambda *_: (0,) * nd)


def _proj_kernel(x_ref, w_ref, o_ref):
    o_ref[...] = _dot(x_ref[...].astype(BF16), w_ref[...]).astype(o_ref.dtype)


def _proj(x, w, out_dtype):
    n, d = x.shape
    m = w.shape[1]
    tm = 512
    return pl.pallas_call(
        _proj_kernel,
        grid=(n // tm,),
        in_specs=[pl.BlockSpec((tm, d), lambda i: (i, 0)), _const_spec((d, m))],
        out_specs=pl.BlockSpec((tm, m), lambda i: (i, 0)),
        out_shape=jax.ShapeDtypeStruct((n, m), out_dtype),
        compiler_params=_params("parallel"),
        name="qkv_proj",
    )(x, w)


def _attn_kernel(lam_ref, g_ref, q_ref, k_ref, v_ref, o_ref, vx_ref, m_ref, acc_ref, sa_ref, sb_ref, *, tq, tk,
                 out_scale):
    qi = pl.program_id(2)
    hd = DA_HEAD_DIM
    hw = 2 * hd

    @pl.when(qi == 0)
    def _():
        vx_ref[:, :hw] = v_ref[0]
        vx_ref[:, hw:] = jnp.ones((vx_ref.shape[0], hw), BF16)

    q = q_ref[0]
    lane = lax.broadcasted_iota(jnp.int32, q.shape, 1)
    zero = jnp.zeros_like(q)
    q_stack = jnp.concatenate([jnp.where(lane < hd, q, zero), jnp.where(lane >= hd, q, zero)], axis=0)
    m_ref[...] = jnp.full(m_ref.shape, NEG_BIG, F32)
    acc_ref[...] = jnp.zeros(acc_ref.shape, F32)
    row0 = qi * tq

    def scores(j, s_ref):
        start = pl.multiple_of(j * tk, tk)
        s_ref[...] = _dot_nt(q_stack, k_ref[0, pl.ds(start, tk), :])

    def consume(j, s_ref, masked):
        start = pl.multiple_of(j * tk, tk)
        vb = vx_ref[pl.ds(start, tk), :]
        if masked:
            row = row0 + lax.broadcasted_iota(jnp.int32, (tq, tk), 0)
            col = start + lax.broadcasted_iota(jnp.int32, (tq, tk), 1)
            keep = col <= row
        for c in range(2):
            s = s_ref[c * tq:(c + 1) * tq, :]
            if masked:
                s = jnp.where(keep, s, NEG_BIG)
            chunks = [s[:, i * LANES:(i + 1) * LANES] for i in range(tk // LANES)]
            m_old = m_ref[c]
            m_new = jnp.maximum(m_old, jnp.max(functools.reduce(jnp.maximum, chunks), axis=-1, keepdims=True))
            alpha = jnp.exp2(m_old - m_new)
            m_ref[c] = m_new
            p = jnp.concatenate([jnp.exp2(ch - m_new).astype(BF16) for ch in chunks], axis=1)
            acc_ref[c] = acc_ref[c] * jnp.concatenate([alpha, alpha], axis=1) + _dot(p, vb)

    n = row0 // tk + 1
    n_pairs = (n - 1) // 2
    scores(0, sa_ref)

    def pair(jj, carry):
        j = 2 * jj
        scores(j + 1, sb_ref)
        consume(j, sa_ref, False)
        scores(j + 2, sa_ref)
        consume(j + 1, sb_ref, False)
        return carry

    lax.fori_loop(0, n_pairs, pair, 0)
    jb = 2 * n_pairs

    @pl.when(n % 2 == 1)
    def _():
        consume(jb, sa_ref, True)

    @pl.when(n % 2 == 0)
    def _():
        scores(jb + 1, sb_ref)
        consume(jb, sa_ref, False)
        consume(jb + 1, sb_ref, True)

    a0, a1 = acc_ref[0], acc_ref[1]
    o = a0[:, :hw] * (1.0 / a0[:, hw:]) - lam_ref[...] * (a1[:, :hw] * (1.0 / a1[:, hw:]))
    ms = jnp.mean(o * o, axis=-1, keepdims=True)
    o = o * lax.rsqrt(ms + RMS_EPS) * g_ref[...] * out_scale
    o_ref[0] = o.astype(o_ref.dtype)


def _diff_attention(qkv, lam_row, subln_g, lambda_init, batch, seq):
    d = DA_HEADS * 2 * DA_HEAD_DIM
    qkv = qkv.reshape(batch, seq, 3 * d)
    tq, tk = ATTN_TQ, ATTN_TK
    assert tq <= tk and tk % tq == 0 and seq % tk == 0
    hw = 2 * DA_HEAD_DIM
    out = pl.pallas_call(
        functools.partial(_attn_kernel, tq=tq, tk=tk, out_scale=1.0 - lambda_init),
        grid=(batch, DA_HEADS, seq // tq),
        in_specs=[
            _const_spec((1, hw)),
            _const_spec((1, hw)),
            pl.BlockSpec((1, tq, hw), lambda b, h, i: (b, i, h)),
            pl.BlockSpec((1, seq, hw), lambda b, h, i: (b, 0, DA_HEADS + h)),
            pl.BlockSpec((1, seq, hw), lambda b, h, i: (b, 0, 2 * DA_HEADS + h)),
        ],
        out_specs=pl.BlockSpec((1, tq, hw), lambda b, h, i: (b, i, h)),
        out_shape=jax.ShapeDtypeStruct((batch, seq, d), BF16),
        scratch_shapes=[pltpu.VMEM((seq, 2 * hw), BF16), pltpu.VMEM((2, tq, hw), F32),
                        pltpu.VMEM((2, tq, 2 * hw), F32), pltpu.VMEM((2 * tq, tk), F32),
                        pltpu.VMEM((2 * tq, tk), F32)],
        compiler_params=_params("parallel", "parallel", "arbitrary"),
        name="diff_attn",
    )(lam_row, subln_g.reshape(1, hw), qkv, qkv, qkv)
    return out.reshape(batch * seq, d)


def _layer_norm_rows(z, g, b):
    mu = jnp.mean(z, axis=-1, keepdims=True)
    zc = z - mu
    var = jnp.mean(zc * zc, axis=-1, keepdims=True)
    return zc * lax.rsqrt(var + LN_EPS) * g + b


def _post_mixer_kernel(o_ref, x_ref, w_ref, lg_ref, lb_ref, wr_ref, br_ref, x1_ref, lo_ref, *, alpha):
    h = _dot(o_ref[...], w_ref[...])
    x1 = _layer_norm_rows(alpha * x_ref[...] + h, lg_ref[...], lb_ref[...])
    x1_ref[...] = x1
    lo_ref[...] = _dot(x1, wr_ref[...], precision=lax.Precision.HIGHEST) + br_ref[...]


def _post_mixer(o, x, w_o, ln_g, ln_b, w_router, b_router, alpha):
    n, d = x.shape
    tm = ROW_TILE
    row = lambda i: (i, 0)
    return pl.pallas_call(
        functools.partial(_post_mixer_kernel, alpha=alpha),
        grid=(n // tm,),
        in_specs=[
            pl.BlockSpec((tm, d), row), pl.BlockSpec((tm, d), row), _const_spec((d, d)),
            _const_spec((1, d)), _const_spec((1, d)), _const_spec((d, LANES)), _const_spec((1, LANES)),
        ],
        out_specs=[pl.BlockSpec((tm, d), row), pl.BlockSpec((tm, LANES), row)],
        out_shape=[jax.ShapeDtypeStruct((n, d), F32), jax.ShapeDtypeStruct((n, LANES), F32)],
        compiler_params=_params("parallel"),
        name="post_mixer",
    )(o, x, w_o, ln_g.reshape(1, d), ln_b.reshape(1, d), w_router, b_router)


def _router_kernel(lo_ref, idx_ref, gate_ref, cnt_ref, carry_ref):
    step = pl.program_id(0)

    @pl.when(step == 0)
    def _():
        carry_ref[...] = jnp.zeros_like(carry_ref)

    lo = lo_ref[...]
    tm = lo.shape[0]
    lane = lax.broadcasted_iota(jnp.int32, lo.shape, 1)
    big = jnp.int32(LANES)

    def softmax_over(mask):
        mx = jnp.max(jnp.where(mask, lo, NEG_BIG), axis=-1, keepdims=True)
        ex = jnp.where(mask, jnp.exp(lo - mx), 0.0)
        return ex / jnp.sum(ex, axis=-1, keepdims=True)

    def top1(p, mask):
        best = jnp.max(jnp.where(mask, p, -1.0), axis=-1, keepdims=True)
        arg = jnp.min(jnp.where(mask & (p == best), lane, big), axis=-1, keepdims=True)
        return best, arg

    g_mask = lane < MOE_GROUPS
    g_p, g_idx = top1(softmax_over(g_mask), g_mask)
    first = MOE_GROUPS + MOE_EPG * g_idx
    e_mask = (lane >= first) & (lane < first + MOE_EPG)
    e_prob = softmax_over(e_mask)
    p1, i1 = top1(e_prob, e_mask)
    rest = e_mask & (lane != i1)
    p2, i2 = top1(e_prob, rest)
    denom = p1 + p2
    gate1 = g_p * (p1 / denom)
    gate2 = g_p * (p2 / denom)
    e1 = i1 - MOE_GROUPS
    e2 = i2 - MOE_GROUPS

    oh1 = (lane == e1).astype(F32)
    oh2 = (lane == e2).astype(F32)
    both = oh1 + oh2
    r_i = lax.broadcasted_iota(jnp.int32, (tm, tm), 0)
    c_i = lax.broadcasted_iota(jnp.int32, (tm, tm), 1)
    before = (c_i < r_i).astype(BF16)
    seen = _dot(before, both.astype(BF16)) + carry_ref[...]
    pos1 = jnp.sum(seen * oh1, axis=-1, keepdims=True).astype(jnp.int32)
    pos2 = jnp.sum(seen * oh2, axis=-1, keepdims=True).astype(jnp.int32)
    carry_ref[...] += jnp.sum(both, axis=0, keepdims=True)
    cnt_ref[...] = carry_ref[...].astype(jnp.int32)

    zero_i = jnp.zeros(lo.shape, jnp.int32)
    idx_ref[...] = (jnp.where(lane == 0, e1, zero_i) + jnp.where(lane == 1, e2, zero_i)
                    + jnp.where(lane == 2, pos1, zero_i) + jnp.where(lane == 3, pos2, zero_i))
    gate_ref[...] = jnp.where(lane == 0, gate1, 0.0) + jnp.where(lane == 1, gate2, 0.0)


def _router(logits):
    n = logits.shape[0]
    tm = ROUTER_TILE
    row = lambda i: (i, 0)
    return pl.pallas_call(
        _router_kernel,
        grid=(n // tm,),
        in_specs=[pl.BlockSpec((tm, LANES), row)],
        out_specs=[pl.BlockSpec((tm, LANES), row), pl.BlockSpec((tm, LANES), row), _const_spec((1, LANES))],
        out_shape=[jax.ShapeDtypeStruct((n, LANES), jnp.int32), jax.ShapeDtypeStruct((n, LANES), F32),
                   jax.ShapeDtypeStruct((1, LANES), jnp.int32)],
        scratch_shapes=[pltpu.VMEM((1, LANES), F32)],
        compiler_params=_params("arbitrary"),
        name="moe_router",
    )(logits)


def _row_copy(src_ref, src_row, dst_ref, dst_row, sem):
    return pltpu.make_async_copy(src_ref.at[pl.ds(src_row, 1)], dst_ref.at[pl.ds(dst_row, 1)], sem)


def _dispatch_kernel(dest_ref, x_ref, buf_in_ref, buf_ref, sem):
    del buf_in_ref
    tm = x_ref.shape[0]

    def copies(r):
        return [_row_copy(x_ref, r, buf_ref, dest_ref[0, 0, MOE_TOPK * r + s], sem) for s in range(MOE_TOPK)]

    def start(r, carry):
        for cp in copies(r):
            cp.start()
        return carry

    def wait(r, carry):
        for cp in copies(r):
            cp.wait()
        return carry

    lax.fori_loop(0, tm, start, 0)
    lax.fori_loop(0, tm, wait, 0)


def _dispatch(x1, dest_tiles, n_rows):
    n, d = x1.shape
    tm = ROW_TILE
    zeros = jnp.zeros((n_rows, d), x1.dtype)
    return pl.pallas_call(
        _dispatch_kernel,
        grid=(n // tm,),
        in_specs=[
            pl.BlockSpec((1, 1, MOE_TOPK * tm), lambda i: (i, 0, 0), memory_space=pltpu.SMEM),
            pl.BlockSpec((tm, d), lambda i: (i, 0)),
            pl.BlockSpec(memory_space=pl.ANY),
        ],
        out_specs=pl.BlockSpec(memory_space=pl.ANY),
        out_shape=jax.ShapeDtypeStruct((n_rows, d), x1.dtype),
        scratch_shapes=[pltpu.SemaphoreType.DMA(())],
        input_output_aliases={2: 0},
        compiler_params=_params("arbitrary"),
        name="moe_dispatch",
    )(dest_tiles, x1, zeros)


def _expert_kernel(be_ref, x_ref, wgu_ref, wd_ref, y_ref, *, hidden):
    del be_ref
    h = _dot(x_ref[...].astype(BF16), wgu_ref[0])
    hg = h[:, :hidden]
    hu = h[:, hidden:]
    act = hg * (1.0 / (1.0 + jnp.exp(-hg))) * hu
    y_ref[...] = _dot(act.astype(BF16), wd_ref[0])


def _experts(x_buf, block_e, w_gu, w_down):
    n_rows, d = x_buf.shape
    hidden = w_down.shape[1]
    tm = MOE_BLOCK
    grid_spec = pltpu.PrefetchScalarGridSpec(
        num_scalar_prefetch=1,
        grid=(n_rows // tm,),
        in_specs=[
            pl.BlockSpec((tm, d), lambda i, be: (i, 0)),
            pl.BlockSpec((1, d, 2 * hidden), lambda i, be: (be[i], 0, 0)),
            pl.BlockSpec((1, hidden, d), lambda i, be: (be[i], 0, 0)),
        ],
        out_specs=pl.BlockSpec((tm, d), lambda i, be: (i, 0)),
    )
    return pl.pallas_call(
        functools.partial(_expert_kernel, hidden=hidden),
        grid_spec=grid_spec,
        out_shape=jax.ShapeDtypeStruct((n_rows, d), F32),
        compiler_params=_params("arbitrary"),
        name="moe_experts",
    )(block_e, x_buf, w_gu, w_down)


def _combine_kernel(dest_ref, y_ref, gate_ref, x_ref, lg_ref, lb_ref, o_ref, rows_ref, sem, *, alpha):
    tm = x_ref.shape[0]

    def copies(r):
        return [_row_copy(y_ref, dest_ref[0, 0, MOE_TOPK * r + s], rows_ref.at[s], r, sem)
                for s in range(MOE_TOPK)]

    def start(r, carry):
        for cp in copies(r):
            cp.start()
        return carry

    def wait(r, carry):
        for cp in copies(r):
            cp.wait()
        return carry

    lax.fori_loop(0, tm, start, 0)
    lax.fori_loop(0, tm, wait, 0)
    gate = gate_ref[...]
    f = sum(gate[:, s:s + 1] * rows_ref[s] for s in range(MOE_TOPK))
    o_ref[...] = _layer_norm_rows(alpha * x_ref[...] + f, lg_ref[...], lb_ref[...])


def _combine(y, dest_tiles, gates, x1, ln_g, ln_b, alpha):
    n, d = x1.shape
    tm = ROW_TILE
    row = lambda i: (i, 0)
    return pl.pallas_call(
        functools.partial(_combine_kernel, alpha=alpha),
        grid=(n // tm,),
        in_specs=[
            pl.BlockSpec((1, 1, MOE_TOPK * tm), lambda i: (i, 0, 0), memory_space=pltpu.SMEM),
            pl.BlockSpec(memory_space=pl.ANY),
            pl.BlockSpec((tm, LANES), row), pl.BlockSpec((tm, d), row), _const_spec((1, d)), _const_spec((1, d)),
        ],
        out_specs=pl.BlockSpec((tm, d), row),
        out_shape=jax.ShapeDtypeStruct((n, d), F32),
        scratch_shapes=[pltpu.VMEM((MOE_TOPK, tm, d), F32), pltpu.SemaphoreType.DMA(())],
        compiler_params=_params("arbitrary"),
        name="moe_combine",
    )(dest_tiles, y, gates, x1, ln_g.reshape(1, d), ln_b.reshape(1, d))


def _hier_moe_block(x1, logits, w_gu, w_down, ln_g, ln_b, alpha):
    n_tok, d = x1.shape
    idx, gates, counts = _router(logits)
    counts = counts[0, :MOE_EXPERTS]
    padded = (counts + MOE_BLOCK - 1) // MOE_BLOCK * MOE_BLOCK
    pad_end = jnp.cumsum(padded)
    pad_start = pad_end - padded
    expert = idx[:, :MOE_TOPK]
    pos = idx[:, MOE_TOPK:2 * MOE_TOPK]
    e_iota = jnp.arange(MOE_EXPERTS, dtype=jnp.int32)
    base = jnp.sum(jnp.where(expert[:, :, None] == e_iota, pad_start, 0), axis=-1)
    dest = (base + pos).astype(jnp.int32)
    n_blocks = n_tok * MOE_TOPK // MOE_BLOCK + MOE_EXPERTS
    n_rows = n_blocks * MOE_BLOCK
    block_start = jnp.arange(n_blocks, dtype=jnp.int32) * MOE_BLOCK
    block_e = jnp.minimum(jnp.sum((pad_end[None, :] <= block_start[:, None]).astype(jnp.int32), axis=1),
                          MOE_EXPERTS - 1)
    dest_tiles = dest.reshape(n_tok // ROW_TILE, 1, MOE_TOPK * ROW_TILE)
    x_buf = _dispatch(x1, dest_tiles, n_rows)
    y = _experts(x_buf, block_e, w_gu, w_down)
    return _combine(y, dest_tiles, gates, x1, ln_g, ln_b, alpha)


def _rwkv_proj_kernel(*refs, seq, has_mix):
    if has_mix:
        (x_ref, xp_ref, mu_ref, wrkv_ref, w0_ref, w1_ref, w2_ref, a0_ref, a1_ref, a2_ref, g1_ref, g2_ref,
         v0_ref, v1_ref, v2_ref, vf_ref, r_ref, k_ref, v_ref, z_ref, a_ref, g_ref) = refs
    else:
        (x_ref, xp_ref, mu_ref, wrkv_ref, w0_ref, w1_ref, w2_ref, a0_ref, a1_ref, a2_ref, g1_ref, g2_ref,
         r_ref, k_ref, v_ref, z_ref, a_ref, g_ref) = refs
    x = x_ref[...]
    tm = x.shape[0]
    row = lax.broadcasted_iota(jnp.int32, x.shape, 0)
    at_seq_start = (pl.program_id(0) * tm) % seq == 0
    prev_last = jnp.where(at_seq_start, 0.0, xp_ref[7:8, :])
    shifted = jnp.where(row == 0, prev_last, pltpu.roll(x, 1, axis=0))
    xx = shifted - x

    def mixed(n):
        return (x + xx * mu_ref[n:n + 1, :]).astype(BF16)

    def sigmoid(t):
        return 1.0 / (1.0 + jnp.exp(-t))

    r_ref[...] = _dot(mixed(0), wrkv_ref[0])
    k_ref[...] = _dot(mixed(1), wrkv_ref[1])
    xv = mixed(2)
    v = _dot(xv, wrkv_ref[2])
    if has_mix:
        mix = sigmoid(v0_ref[...] + _dot(_dot(xv, v1_ref[...]).astype(BF16), v2_ref[...]))
        v = v + (vf_ref[...] - v) * mix
    v_ref[...] = v
    z_ref[...] = w0_ref[...] + _dot(jnp.tanh(_dot(mixed(3), w1_ref[...])).astype(BF16), w2_ref[...])
    a_ref[...] = sigmoid(a0_ref[...] + _dot(_dot(mixed(4), a1_ref[...]).astype(BF16), a2_ref[...]))
    g_ref[...] = _dot(sigmoid(_dot(mixed(5), g1_ref[...])).astype(BF16), g2_ref[...])


def _pad_cols(w, width):
    return jnp.pad(w, ((0, 0), (0, width - w.shape[1])))


def _pad_rows(w, height):
    return jnp.pad(w, ((0, height - w.shape[0]), (0, 0)))


def _rwkv_proj(x, seq, mu, w_rkv, w0, w1, w2, a0, a1, a2, g1, g2, value_mix, v_first):
    n, d = x.shape
    tm = ROW_TILE
    row = lambda i: (i, 0)
    lora = lambda w_in, w_out, width: (_pad_cols(w_in, width).astype(BF16), _pad_rows(w_out, width).astype(BF16))
    w1p, w2p = lora(w1, w2, LANES)
    a1p, a2p = lora(a1, a2, LANES)
    g1p, g2p = lora(g1, g2, 2 * LANES)
    mu8 = _pad_rows(mu, 8)
    ins = [x, x, mu8, w_rkv.astype(BF16), w0.reshape(1, d), w1p, w2p, a0.reshape(1, d), a1p, a2p, g1p, g2p]
    specs = [
        pl.BlockSpec((tm, d), row),
        pl.BlockSpec((8, d), lambda i: (jnp.maximum(i * (tm // 8) - 1, 0), 0)),
        _const_spec((8, d)), _const_spec((3, d, d)), _const_spec((1, d)),
        _const_spec((d, LANES)), _const_spec((LANES, d)), _const_spec((1, d)),
        _const_spec((d, LANES)), _const_spec((LANES, d)),
        _const_spec((d, 2 * LANES)), _const_spec((2 * LANES, d)),
    ]
    has_mix = value_mix is not None
    if has_mix:
        v0, v1, v2 = value_mix
        v1p, v2p = lora(v1, v2, LANES)
        ins += [v0.reshape(1, d), v1p, v2p, v_first]
        specs += [_const_spec((1, d)), _const_spec((d, LANES)), _const_spec((LANES, d)), pl.BlockSpec((tm, d), row)]
    out = jax.ShapeDtypeStruct((n, d), F32)
    return pl.pallas_call(
        functools.partial(_rwkv_proj_kernel, seq=seq, has_mix=has_mix),
        grid=(n // tm,),
        in_specs=specs,
        out_specs=[pl.BlockSpec((tm, d), row)] * 6,
        out_shape=[out] * 6,
        compiler_params=_params("parallel"),
        name="rwkv_proj",
    )(*ins)


def _wkv_kernel(r_ref, k_ref, v_ref, z_ref, a_ref, g_ref, kk_ref, ka_ref, rk_ref, lg_ref, lb_ref, o_ref, s_ref,
                *, tblock):
    c = WKV_CHUNK
    hn = RW_HEAD
    w = 2 * hn

    @pl.when(pl.program_id(2) == 0)
    def _():
        s_ref[...] = jnp.zeros_like(s_ref)

    def iota2(shape, dim):
        return lax.broadcasted_iota(jnp.int32, shape, dim)

    lane_c = iota2((c, w), 1)
    head0 = lane_c < hn
    rw, cw = iota2((w, w), 0), iota2((w, w), 1)
    same_head = (rw // hn) == (cw // hn)
    strict = same_head & ((cw % hn) < (rw % hn))
    incl = same_head & ((cw % hn) <= (rw % hn))
    blk16 = (rw // 16) == (cw // 16)
    blk32 = (rw // 32) == (cw // 32)
    eye = (rw == cw).astype(F32)
    r3, c3 = iota2((3 * w, w), 0), iota2((3 * w, w), 1)
    seg_ones3 = (((r3 % w) // hn) == (c3 // hn)).astype(BF16)
    rt3, ct3 = iota2((tblock, 3 * tblock), 0), iota2((tblock, 3 * tblock), 1) % tblock
    tri3 = ((ct3 <= rt3) & ((ct3 // c) == (rt3 // c))).astype(BF16)

    def split3(t):
        p1 = t.astype(BF16)
        rest = t - p1.astype(F32)
        p2 = rest.astype(BF16)
        return p1, p2, (rest - p2.astype(F32)).astype(BF16)

    def seg_sum(t):
        return _dot(jnp.concatenate(split3(t), axis=1), seg_ones3)

    def stack(t):
        zero = jnp.zeros_like(t)
        return jnp.concatenate([jnp.where(head0, t, zero), jnp.where(head0, zero, t)], axis=0)

    def fold(t):
        return t[:c] + t[c:]

    def bf(t):
        return t.astype(BF16)

    nch = tblock // c
    chunks = range(nch)

    def rows(t, ci):
        return t[ci * c:(ci + 1) * c]

    r, k, v, z, a, g = (ref[0] for ref in (r_ref, k_ref, v_ref, z_ref, a_ref, g_ref))
    kk = k * kk_ref[...]
    kk = kk / jnp.maximum(jnp.sqrt(seg_sum(kk * kk)), 1e-12)
    kmod = k * (1.0 + (a - 1.0) * ka_ref[...])
    bvec = kk * a
    lw = (-math.exp(-0.5)) / (1.0 + jnp.exp(-z))
    cum = _dot(tri3, jnp.concatenate(split3(lw), axis=0))
    at = -kk * jnp.exp(cum - lw)
    rt = r * jnp.exp(cum)
    inv = jnp.exp(-cum)
    bt = bvec * inv
    kt = kmod * inv
    totals = [cum[(ci + 1) * c - 1:(ci + 1) * c, :] for ci in chunks]
    to_end = jnp.concatenate([jnp.exp(totals[ci] - rows(cum, ci)) for ci in chunks], axis=0)
    bh = bvec * to_end
    kh = kmod * to_end

    ast = [bf(stack(rows(at, ci))) for ci in chunks]
    rst = [bf(stack(rows(rt, ci))) for ci in chunks]
    bst = [bf(stack(rows(bt, ci))) for ci in chunks]
    kst = [bf(stack(rows(kt, ci))) for ci in chunks]
    l_full = [jnp.where(strict, _dot_nt(ast[ci], bst[ci]), 0.0) for ci in chunks]
    a_ak = [bf(fold(jnp.where(strict, _dot_nt(ast[ci], kst[ci]), 0.0))) for ci in chunks]
    a_rb = [bf(fold(jnp.where(incl, _dot_nt(rst[ci], bst[ci]), 0.0))) for ci in chunks]
    a_rk = [bf(fold(jnp.where(incl, _dot_nt(rst[ci], kst[ci]), 0.0))) for ci in chunks]

    xp = [bf(jnp.where(blk16, l_full[ci], 0.0)) for ci in chunks]
    tm = [eye + jnp.where(blk16, l_full[ci], 0.0) for ci in chunks]
    for _ in range(3):
        xp = [bf(_dot(xp[ci], xp[ci])) for ci in chunks]
        tm = [tm[ci] + _dot(bf(tm[ci]), xp[ci]) for ci in chunks]
    for inside, outside in ((blk32, blk16), (same_head, blk32)):
        off = [bf(jnp.where(inside & jnp.logical_not(outside), l_full[ci], 0.0)) for ci in chunks]
        tmb = [bf(tm[ci]) for ci in chunks]
        half = [bf(_dot(tmb[ci], off[ci])) for ci in chunks]
        tm = [tm[ci] + _dot(half[ci], tmb[ci]) for ci in chunks]
    t_fold = [bf(fold(tm[ci])) for ci in chunks]
    vst = [bf(stack(rows(v, ci))) for ci in chunks]
    atb, rtb = bf(at), bf(rt)
    uv_rhs = [bf(jnp.concatenate([rows(bh, ci), rows(kh, ci)], axis=0)) for ci in chunks]
    y_loc = [_dot(a_rk[ci], vst[ci]) for ci in chunks]
    x_loc = [_dot(a_ak[ci], vst[ci]) for ci in chunks]
    decay_c = [jnp.exp(totals[ci]) for ci in chunks]

    s = s_ref[...]
    ys = []
    for ci in chunks:
        sb = bf(s)
        xr = _dot_nt(rows(atb, ci), sb) + x_loc[ci]
        u = _dot(t_fold[ci], bf(stack(xr)))
        ys.append(_dot_nt(rows(rtb, ci), sb) + _dot(a_rb[ci], bf(stack(u))) + y_loc[ci])
        upd = _dot_tn(bf(jnp.concatenate([u, rows(v, ci)], axis=0)), uv_rhs[ci])
        s = s * decay_c[ci] + jnp.where(same_head, upd, 0.0)
    s_ref[...] = s

    y = jnp.concatenate(ys, axis=0)
    mean = seg_sum(y) * (1.0 / hn)
    yc = y - mean
    var = seg_sum(yc * yc) * (1.0 / hn)
    yn = yc * lax.rsqrt(var + RW_GN_EPS) * lg_ref[...] + lb_ref[...]
    yn = yn + seg_sum(r * kmod * rk_ref[...]) * v
    o_ref[0] = (yn * g).astype(o_ref.dtype)


def _wkv(r, k, v, z, a, g, k_k, k_a, r_k, lnx_g, lnx_b, batch, seq):
    n, d = r.shape
    w = 2 * RW_HEAD
    tb = WKV_TBLOCK
    seq_spec = pl.BlockSpec((1, tb, w), lambda b, h, t: (b, t, h))
    par_spec = pl.BlockSpec((1, w), lambda b, h, t: (0, h))
    r3 = lambda t: t.reshape(batch, seq, d)
    out = pl.pallas_call(
        functools.partial(_wkv_kernel, tblock=tb),
        grid=(batch, d // w, seq // tb),
        in_specs=[seq_spec] * 6 + [par_spec] * 5,
        out_specs=seq_spec,
        out_shape=jax.ShapeDtypeStruct((batch, seq, d), BF16),
        scratch_shapes=[pltpu.VMEM((w, w), F32)],
        compiler_params=_params("parallel", "parallel", "arbitrary"),
        name="wkv7_chunked",
    )(r3(r), r3(k), r3(v), r3(z), r3(a), r3(g), k_k.reshape(1, d), k_a.reshape(1, d), r_k.reshape(1, d),
      lnx_g.reshape(1, d), lnx_b.reshape(1, d))
    return out.reshape(n, d)


def kernel(x, ln1_g, ln1_b, ln2_g, ln2_b, attn_w_qkv, attn_w_o, attn_lambda, attn_subln_g, rw_mu, rw_w_rkv, rw_w_o, rw_w0, rw_w1, rw_w2, rw_a0, rw_a1, rw_a2, rw_g1, rw_g2, rw_k_k, rw_k_a, rw_r_k, rw_lnx_g, rw_lnx_b, rw_v0, rw_v1, rw_v2, moe_rg_w, moe_rg_b, moe_re_w, moe_re_b, moe_w_gu, moe_w_down):
    batch, seq, d = x.shape
    depth = ln1_g.shape[0]
    n = batch * seq
    alpha = (2 * depth) ** 0.25
    x = x.reshape(n, d)
    v_first = None
    for i in range(depth):
        j = i // N_MIXERS
        if i % N_MIXERS == 0:
            lambda_init = 0.8 - 0.6 * math.exp(-0.3 * i)
            lam = attn_lambda[j]
            lam_full = jnp.exp(jnp.sum(lam[0] * lam[1])) - jnp.exp(jnp.sum(lam[2] * lam[3])) + lambda_init
            lam_row = jnp.full((1, 2 * DA_HEAD_DIM), lam_full, F32)
            q_scale = DA_HEAD_DIM ** -0.5 * math.log2(math.e)
            col_scale = jnp.concatenate([jnp.full((d,), q_scale, F32), jnp.ones((2 * d,), F32)])
            w_qkv = (attn_w_qkv[j] * col_scale[None, :]).astype(BF16)
            qkv = _proj(x, w_qkv, BF16)
            mixed = _diff_attention(qkv, lam_row, attn_subln_g[j], lambda_init, batch, seq)
            w_o = attn_w_o[j]
        else:
            value_mix = None if j == 0 else (rw_v0[j - 1], rw_v1[j - 1], rw_v2[j - 1])
            r, k, v, z, a, g = _rwkv_proj(x, seq, rw_mu[j], rw_w_rkv[j], rw_w0[j], rw_w1[j], rw_w2[j], rw_a0[j],
                                          rw_a1[j], rw_a2[j], rw_g1[j], rw_g2[j], value_mix, v_first)
            if value_mix is None:
                v_first = v
            mixed = _wkv(r, k, v, z, a, g, rw_k_k[j], rw_k_a[j], rw_r_k[j].reshape(d), rw_lnx_g[j], rw_lnx_b[j],
                         batch, seq)
            w_o = rw_w_o[j]
        n_router = MOE_GROUPS + MOE_EXPERTS
        w_router = _pad_cols(jnp.concatenate([moe_rg_w[i], moe_re_w[i]], axis=1), LANES)
        b_router = _pad_cols(jnp.concatenate([moe_rg_b[i], moe_re_b[i]]).reshape(1, n_router), LANES)
        x1, logits = _post_mixer(mixed, x, w_o.astype(BF16), ln1_g[i], ln1_b[i], w_router, b_router, alpha)
        x = _hier_moe_block(x1, logits, moe_w_gu[i].astype(BF16), moe_w_down[i].astype(BF16), ln2_g[i], ln2_b[i],
                            alpha)
    return x.reshape(batch, seq, d)
```

```python
import functools
import math

import jax
import jax.numpy as jnp
from jax import lax
from jax.experimental import pallas as pl
from jax.experimental.pallas import tpu as pltpu

F32 = jnp.float32
BF16 = jnp.bfloat16

LANES = 128
VMEM_LIMIT = 56 * 1024 * 1024

DA_HEADS = 8
DA_HEAD_DIM = 64
RMS_EPS = 1e-5
RW_HEAD = 64
RW_GN_EPS = 64e-5
MOE_GROUPS = 4
MOE_EPG = 8
MOE_EXPERTS = MOE_GROUPS * MOE_EPG
MOE_TOPK = 2
LN_EPS = 1e-5
N_MIXERS = 2

ROW_TILE = 256
ROUTER_TILE = 512
ATTN_TQ = 256
ATTN_TK = 512
WKV_CHUNK = 64
WKV_TBLOCK = 512
WKV_PAIRS = 2
MOE_BLOCK = 256
NEG_BIG = -1e30


def _dot(a, b, precision=None):
    return jnp.dot(a, b, preferred_element_type=F32, precision=precision)


def _dot_nt(a, b):
    return lax.dot_general(a, b, (((1,), (1,)), ((), ())), preferred_element_type=F32)


def _dot_tn(a, b):
    return lax.dot_general(a, b, (((0,), (0,)), ((), ())), preferred_element_type=F32)


def _params(*sem):
    return pltpu.CompilerParams(dimension_semantics=sem, vmem_limit_bytes=VMEM_LIMIT)


def _const_spec(shape):
    nd = len(shape)
    return pl.BlockSpec(shape, lambda *_: (0,) * nd)


def _proj_kernel(x_ref, w_ref, o_ref):
    o_ref[...] = _dot(x_ref[...].astype(BF16), w_ref[...]).astype(o_ref.dtype)


def _proj(x, w, out_dtype):
    n, d = x.shape
    m = w.shape[1]
    tm = 512
    return pl.pallas_call(
        _proj_kernel,
        grid=(n // tm,),
        in_specs=[pl.BlockSpec((tm, d), lambda i: (i, 0)), _const_spec((d, m))],
        out_specs=pl.BlockSpec((tm, m), lambda i: (i, 0)),
        out_shape=jax.ShapeDtypeStruct((n, m), out_dtype),
        compiler_params=_params("parallel"),
        name="qkv_proj",
    )(x, w)


def _attn_kernel(lam_ref, g_ref, q_ref, k_ref, v_ref, o_ref, vx_ref, m_ref, acc_ref, sa_ref, sb_ref, *, tq, tk,
                 out_scale):
    qi = pl.program_id(2)
    hd = DA_HEAD_DIM
    hw = 2 * hd

    @pl.when(qi == 0)
    def _():
        vx_ref[:, :hw] = v_ref[0]
        vx_ref[:, hw:] = jnp.ones((vx_ref.shape[0], hw), BF16)

    q = q_ref[0]
    lane = lax.broadcasted_iota(jnp.int32, q.shape, 1)
    zero = jnp.zeros_like(q)
    q_stack = jnp.concatenate([jnp.where(lane < hd, q, zero), jnp.where(lane >= hd, q, zero)], axis=0)
    m_ref[...] = jnp.full(m_ref.shape, NEG_BIG, F32)
    acc_ref[...] = jnp.zeros(acc_ref.shape, F32)
    row0 = qi * tq

    def scores(j, s_ref):
        start = pl.multiple_of(j * tk, tk)
        s_ref[...] = _dot_nt(q_stack, k_ref[0, pl.ds(start, tk), :])

    def consume(j, s_ref, masked):
        start = pl.multiple_of(j * tk, tk)
        vb = vx_ref[pl.ds(start, tk), :]
        if masked:
            row = row0 + lax.broadcasted_iota(jnp.int32, (tq, tk), 0)
            col = start + lax.broadcasted_iota(jnp.int32, (tq, tk), 1)
            keep = col <= row
        for c in range(2):
            s = s_ref[c * tq:(c + 1) * tq, :]
            if masked:
                s = jnp.where(keep, s, NEG_BIG)
            chunks = [s[:, i * LANES:(i + 1) * LANES] for i in range(tk // LANES)]
            m_old = m_ref[c]
            m_new = jnp.maximum(m_old, jnp.max(functools.reduce(jnp.maximum, chunks), axis=-1, keepdims=True))
            alpha = jnp.exp2(m_old - m_new)
            m_ref[c] = m_new
            p = jnp.concatenate([jnp.exp2(ch - m_new).astype(BF16) for ch in chunks], axis=1)
            acc_ref[c] = acc_ref[c] * jnp.concatenate([alpha, alpha], axis=1) + _dot(p, vb)

    n = row0 // tk + 1
    n_pairs = (n - 1) // 2
    scores(0, sa_ref)

    def pair(jj, carry):
        j = 2 * jj
        scores(j + 1, sb_ref)
        consume(j, sa_ref, False)
        scores(j + 2, sa_ref)
        consume(j + 1, sb_ref, False)
        return carry

    lax.fori_loop(0, n_pairs, pair, 0)
    jb = 2 * n_pairs

    @pl.when(n % 2 == 1)
    def _():
        consume(jb, sa_ref, True)

    @pl.when(n % 2 == 0)
    def _():
        scores(jb + 1, sb_ref)
        consume(jb, sa_ref, False)
        consume(jb + 1, sb_ref, True)

    a0, a1 = acc_ref[0], acc_ref[1]
    o = a0[:, :hw] * (1.0 / a0[:, hw:]) - lam_ref[...] * (a1[:, :hw] * (1.0 / a1[:, hw:]))
    ms = jnp.mean(o * o, axis=-1, keepdims=True)
    o = o * lax.rsqrt(ms + RMS_EPS) * g_ref[...] * out_scale
    o_ref[0] = o.astype(o_ref.dtype)


def _diff_attention(qkv, lam_row, subln_g, lambda_init, batch, seq):
    d = DA_HEADS * 2 * DA_HEAD_DIM
    qkv = qkv.reshape(batch, seq, 3 * d)
    tq, tk = ATTN_TQ, ATTN_TK
    assert tq <= tk and tk % tq == 0 and seq % tk == 0
    hw = 2 * DA_HEAD_DIM
    out = pl.pallas_call(
        functools.partial(_attn_kernel, tq=tq, tk=tk, out_scale=1.0 - lambda_init),
        grid=(batch, DA_HEADS, seq // tq),
        in_specs=[
            _const_spec((1, hw)),
            _const_spec((1, hw)),
            pl.BlockSpec((1, tq, hw), lambda b, h, i: (b, i, h)),
            pl.BlockSpec((1, seq, hw), lambda b, h, i: (b, 0, DA_HEADS + h)),
            pl.BlockSpec((1, seq, hw), lambda b, h, i: (b, 0, 2 * DA_HEADS + h)),
        ],
        out_specs=pl.BlockSpec((1, tq, hw), lambda b, h, i: (b, i, h)),
        out_shape=jax.ShapeDtypeStruct((batch, seq, d), BF16),
        scratch_shapes=[pltpu.VMEM((seq, 2 * hw), BF16), pltpu.VMEM((2, tq, hw), F32),
                        pltpu.VMEM((2, tq, 2 * hw), F32), pltpu.VMEM((2 * tq, tk), F32),
                        pltpu.VMEM((2 * tq, tk), F32)],
        compiler_params=_params("parallel", "parallel", "arbitrary"),
        name="diff_attn",
    )(lam_row, subln_g.reshape(1, hw), qkv, qkv, qkv)
    return out.reshape(batch * seq, d)


def _layer_norm_rows(z, g, b):
    mu = jnp.mean(z, axis=-1, keepdims=True)
    zc = z - mu
    var = jnp.mean(zc * zc, axis=-1, keepdims=True)
    return zc * lax.rsqrt(var + LN_EPS) * g + b


def _post_mixer_kernel(o_ref, x_ref, w_ref, lg_ref, lb_ref, wr_ref, br_ref, x1_ref, lo_ref, *, alpha):
    h = _dot(o_ref[...], w_ref[...])
    x1 = _layer_norm_rows(alpha * x_ref[...] + h, lg_ref[...], lb_ref[...])
    x1_ref[...] = x1
    x_hi = x1.astype(BF16)
    x_lo = (x1 - x_hi.astype(F32)).astype(BF16)
    lo_ref[...] = _dot(jnp.concatenate([x_hi, x_lo, x_hi], axis=1), wr_ref[...]) + br_ref[...]


def _post_mixer(o, x, w_o, ln_g, ln_b, w_router, b_router, alpha):
    n, d = x.shape
    tm = ROW_TILE
    row = lambda i: (i, 0)
    return pl.pallas_call(
        functools.partial(_post_mixer_kernel, alpha=alpha),
        grid=(n // tm,),
        in_specs=[
            pl.BlockSpec((tm, d), row), pl.BlockSpec((tm, d), row), _const_spec((d, d)),
            _const_spec((1, d)), _const_spec((1, d)), _const_spec((3 * d, LANES)), _const_spec((1, LANES)),
        ],
        out_specs=[pl.BlockSpec((tm, d), row), pl.BlockSpec((tm, LANES), row)],
        out_shape=[jax.ShapeDtypeStruct((n, d), F32), jax.ShapeDtypeStruct((n, LANES), F32)],
        compiler_params=_params("parallel"),
        name="post_mixer",
    )(o, x, w_o, ln_g.reshape(1, d), ln_b.reshape(1, d), w_router, b_router)


def _router_kernel(lo_ref, idx_ref, gate_ref, cnt_ref, carry_ref):
    step = pl.program_id(0)

    @pl.when(step == 0)
    def _():
        carry_ref[...] = jnp.zeros_like(carry_ref)

    lo = lo_ref[...]
    tm = lo.shape[0]
    lane = lax.broadcasted_iota(jnp.int32, lo.shape, 1)
    big = jnp.int32(LANES)

    def softmax_over(mask):
        mx = jnp.max(jnp.where(mask, lo, NEG_BIG), axis=-1, keepdims=True)
        ex = jnp.where(mask, jnp.exp(lo - mx), 0.0)
        return ex / jnp.sum(ex, axis=-1, keepdims=True)

    def top1(p, mask):
        best = jnp.max(jnp.where(mask, p, -1.0), axis=-1, keepdims=True)
        arg = jnp.min(jnp.where(mask & (p == best), lane, big), axis=-1, keepdims=True)
        return best, arg

    g_mask = lane < MOE_GROUPS
    g_p, g_idx = top1(softmax_over(g_mask), g_mask)
    first = MOE_GROUPS + MOE_EPG * g_idx
    e_mask = (lane >= first) & (lane < first + MOE_EPG)
    e_prob = softmax_over(e_mask)
    p1, i1 = top1(e_prob, e_mask)
    rest = e_mask & (lane != i1)
    p2, i2 = top1(e_prob, rest)
    denom = p1 + p2
    gate1 = g_p * (p1 / denom)
    gate2 = g_p * (p2 / denom)
    e1 = i1 - MOE_GROUPS
    e2 = i2 - MOE_GROUPS

    oh1 = (lane == e1).astype(F32)
    oh2 = (lane == e2).astype(F32)
    both = oh1 + oh2
    r_i = lax.broadcasted_iota(jnp.int32, (tm, tm), 0)
    c_i = lax.broadcasted_iota(jnp.int32, (tm, tm), 1)
    before = (c_i < r_i).astype(BF16)
    seen = _dot(before, both.astype(BF16)) + carry_ref[...]
    pos1 = jnp.sum(seen * oh1, axis=-1, keepdims=True).astype(jnp.int32)
    pos2 = jnp.sum(seen * oh2, axis=-1, keepdims=True).astype(jnp.int32)
    carry_ref[...] += jnp.sum(both, axis=0, keepdims=True)
    cnt_ref[...] = carry_ref[...].astype(jnp.int32)

    zero_i = jnp.zeros(lo.shape, jnp.int32)
    idx_ref[...] = (jnp.where(lane == 0, e1, zero_i) + jnp.where(lane == 1, e2, zero_i)
                    + jnp.where(lane == 2, pos1, zero_i) + jnp.where(lane == 3, pos2, zero_i))
    gate_ref[...] = jnp.where(lane == 0, gate1, 0.0) + jnp.where(lane == 1, gate2, 0.0)


def _router(logits):
    n = logits.shape[0]
    tm = ROUTER_TILE
    row = lambda i: (i, 0)
    return pl.pallas_call(
        _router_kernel,
        grid=(n // tm,),
        in_specs=[pl.BlockSpec((tm, LANES), row)],
        out_specs=[pl.BlockSpec((tm, LANES), row), pl.BlockSpec((tm, LANES), row), _const_spec((1, LANES))],
        out_shape=[jax.ShapeDtypeStruct((n, LANES), jnp.int32), jax.ShapeDtypeStruct((n, LANES), F32),
                   jax.ShapeDtypeStruct((1, LANES), jnp.int32)],
        scratch_shapes=[pltpu.VMEM((1, LANES), F32)],
        compiler_params=_params("arbitrary"),
        name="moe_router",
    )(logits)


def _row_copy(src_ref, src_row, dst_ref, dst_row, sem):
    return pltpu.make_async_copy(src_ref.at[pl.ds(src_row, 1)], dst_ref.at[pl.ds(dst_row, 1)], sem)


def _dispatch_kernel(dest_ref, x_ref, buf_in_ref, buf_ref, sem):
    del buf_in_ref
    tm = x_ref.shape[0]

    def copies(r):
        return [_row_copy(x_ref, r, buf_ref, dest_ref[0, 0, MOE_TOPK * r + s], sem) for s in range(MOE_TOPK)]

    def start(r, carry):
        for cp in copies(r):
            cp.start()
        return carry

    def wait(r, carry):
        for cp in copies(r):
            cp.wait()
        return carry

    lax.fori_loop(0, tm, start, 0)
    lax.fori_loop(0, tm, wait, 0)


def _dispatch(x1, dest_tiles, n_rows):
    n, d = x1.shape
    tm = ROW_TILE
    zeros = jnp.zeros((n_rows, d), x1.dtype)
    return pl.pallas_call(
        _dispatch_kernel,
        grid=(n // tm,),
        in_specs=[
            pl.BlockSpec((1, 1, MOE_TOPK * tm), lambda i: (i, 0, 0), memory_space=pltpu.SMEM),
            pl.BlockSpec((tm, d), lambda i: (i, 0)),
            pl.BlockSpec(memory_space=pl.ANY),
        ],
        out_specs=pl.BlockSpec(memory_space=pl.ANY),
        out_shape=jax.ShapeDtypeStruct((n_rows, d), x1.dtype),
        scratch_shapes=[pltpu.SemaphoreType.DMA(())],
        input_output_aliases={2: 0},
        compiler_params=_params("arbitrary"),
        name="moe_dispatch",
    )(dest_tiles, x1, zeros)


def _expert_kernel(be_ref, x_ref, wgu_ref, wd_ref, y_ref, wgu_bf_ref, wd_bf_ref, *, hidden):
    i = pl.program_id(0)

    @pl.when((i == 0) | (be_ref[i] != be_ref[jnp.maximum(i - 1, 0)]))
    def _():
        wgu_bf_ref[...] = wgu_ref[0].astype(BF16)
        wd_bf_ref[...] = wd_ref[0].astype(BF16)

    h = _dot(x_ref[...].astype(BF16), wgu_bf_ref[...])
    hg = h[:, :hidden]
    hu = h[:, hidden:]
    act = hg * (1.0 / (1.0 + jnp.exp(-hg))) * hu
    y_ref[...] = _dot(act.astype(BF16), wd_bf_ref[...])


def _experts(x_buf, block_e, w_gu, w_down):
    n_rows, d = x_buf.shape
    hidden = w_down.shape[1]
    tm = MOE_BLOCK
    grid_spec = pltpu.PrefetchScalarGridSpec(
        num_scalar_prefetch=1,
        grid=(n_rows // tm,),
        in_specs=[
            pl.BlockSpec((tm, d), lambda i, be: (i, 0)),
            pl.BlockSpec((1, d, 2 * hidden), lambda i, be: (be[i], 0, 0)),
            pl.BlockSpec((1, hidden, d), lambda i, be: (be[i], 0, 0)),
        ],
        out_specs=pl.BlockSpec((tm, d), lambda i, be: (i, 0)),
        scratch_shapes=[pltpu.VMEM((d, 2 * hidden), BF16), pltpu.VMEM((hidden, d), BF16)],
    )
    return pl.pallas_call(
        functools.partial(_expert_kernel, hidden=hidden),
        grid_spec=grid_spec,
        out_shape=jax.ShapeDtypeStruct((n_rows, d), F32),
        compiler_params=_params("arbitrary"),
        name="moe_experts",
    )(block_e, x_buf, w_gu, w_down)


def _combine_kernel(dest_ref, y_ref, gate_ref, x_ref, lg_ref, lb_ref, o_ref, rows_ref, sem, *, alpha):
    tm = x_ref.shape[0]

    def copies(r):
        return [_row_copy(y_ref, dest_ref[0, 0, MOE_TOPK * r + s], rows_ref.at[s], r, sem)
                for s in range(MOE_TOPK)]

    def start(r, carry):
        for cp in copies(r):
            cp.start()
        return carry

    def wait(r, carry):
        for cp in copies(r):
            cp.wait()
        return carry

    lax.fori_loop(0, tm, start, 0)
    lax.fori_loop(0, tm, wait, 0)
    gate = gate_ref[...]
    f = sum(gate[:, s:s + 1] * rows_ref[s] for s in range(MOE_TOPK))
    o_ref[...] = _layer_norm_rows(alpha * x_ref[...] + f, lg_ref[...], lb_ref[...])


def _combine(y, dest_tiles, gates, x1, ln_g, ln_b, alpha):
    n, d = x1.shape
    tm = ROW_TILE
    row = lambda i: (i, 0)
    return pl.pallas_call(
        functools.partial(_combine_kernel, alpha=alpha),
        grid=(n // tm,),
        in_specs=[
            pl.BlockSpec((1, 1, MOE_TOPK * tm), lambda i: (i, 0, 0), memory_space=pltpu.SMEM),
            pl.BlockSpec(memory_space=pl.ANY),
            pl.BlockSpec((tm, LANES), row), pl.BlockSpec((tm, d), row), _const_spec((1, d)), _const_spec((1, d)),
        ],
        out_specs=pl.BlockSpec((tm, d), row),
        out_shape=jax.ShapeDtypeStruct((n, d), F32),
        scratch_shapes=[pltpu.VMEM((MOE_TOPK, tm, d), F32), pltpu.SemaphoreType.DMA(())],
        compiler_params=_params("arbitrary"),
        name="moe_combine",
    )(dest_tiles, y, gates, x1, ln_g.reshape(1, d), ln_b.reshape(1, d))


def _hier_moe_block(x1, logits, w_gu, w_down, ln_g, ln_b, alpha):
    n_tok, d = x1.shape
    idx, gates, counts = _router(logits)
    counts = counts[0, :MOE_EXPERTS]
    padded = (counts + MOE_BLOCK - 1) // MOE_BLOCK * MOE_BLOCK
    pad_end = jnp.cumsum(padded)
    pad_start = pad_end - padded
    expert = idx[:, :MOE_TOPK]
    pos = idx[:, MOE_TOPK:2 * MOE_TOPK]
    e_iota = jnp.arange(MOE_EXPERTS, dtype=jnp.int32)
    base = jnp.sum(jnp.where(expert[:, :, None] == e_iota, pad_start, 0), axis=-1)
    dest = (base + pos).astype(jnp.int32)
    n_blocks = n_tok * MOE_TOPK // MOE_BLOCK + MOE_EXPERTS
    n_rows = n_blocks * MOE_BLOCK
    block_start = jnp.arange(n_blocks, dtype=jnp.int32) * MOE_BLOCK
    block_e = jnp.minimum(jnp.sum((pad_end[None, :] <= block_start[:, None]).astype(jnp.int32), axis=1),
                          MOE_EXPERTS - 1)
    dest_tiles = dest.reshape(n_tok // ROW_TILE, 1, MOE_TOPK * ROW_TILE)
    x_buf = _dispatch(x1, dest_tiles, n_rows)
    y = _experts(x_buf, block_e, w_gu, w_down)
    return _combine(y, dest_tiles, gates, x1, ln_g, ln_b, alpha)


def _rwkv_proj_kernel(*refs, seq, has_mix):
    if has_mix:
        (x_ref, xp_ref, mu_ref, wrkv_ref, w0_ref, w1_ref, w2_ref, a0_ref, a1_ref, a2_ref, g1_ref, g2_ref,
         v0_ref, v1_ref, v2_ref, vf_ref, r_ref, k_ref, v_ref, z_ref, a_ref, g_ref) = refs
    else:
        (x_ref, xp_ref, mu_ref, wrkv_ref, w0_ref, w1_ref, w2_ref, a0_ref, a1_ref, a2_ref, g1_ref, g2_ref,
         r_ref, k_ref, v_ref, z_ref, a_ref, g_ref) = refs
    x = x_ref[...]
    tm = x.shape[0]
    row = lax.broadcasted_iota(jnp.int32, x.shape, 0)
    at_seq_start = (pl.program_id(0) * tm) % seq == 0
    prev_last = jnp.where(at_seq_start, 0.0, xp_ref[7:8, :])
    shifted = jnp.where(row == 0, prev_last, pltpu.roll(x, 1, axis=0))
    xx = shifted - x

    def mixed(n):
        return (x + xx * mu_ref[n:n + 1, :]).astype(BF16)

    def sigmoid(t):
        return 1.0 / (1.0 + jnp.exp(-t))

    r_ref[...] = _dot(mixed(0), wrkv_ref[0])
    k_ref[...] = _dot(mixed(1), wrkv_ref[1])
    xv = mixed(2)
    v = _dot(xv, wrkv_ref[2])
    if has_mix:
        mix = sigmoid(v0_ref[...] + _dot(_dot(xv, v1_ref[...]).astype(BF16), v2_ref[...]))
        v = v + (vf_ref[...] - v) * mix
    v_ref[...] = v
    z_ref[...] = w0_ref[...] + _dot(jnp.tanh(_dot(mixed(3), w1_ref[...])).astype(BF16), w2_ref[...])
    a_ref[...] = sigmoid(a0_ref[...] + _dot(_dot(mixed(4), a1_ref[...]).astype(BF16), a2_ref[...]))
    g_ref[...] = _dot(sigmoid(_dot(mixed(5), g1_ref[...])).astype(BF16), g2_ref[...])


def _pad_cols(w, width):
    return jnp.pad(w, ((0, 0), (0, width - w.shape[1])))


def _pad_rows(w, height):
    return jnp.pad(w, ((0, height - w.shape[0]), (0, 0)))


def _rwkv_proj(x, seq, mu, w_rkv, w0, w1, w2, a0, a1, a2, g1, g2, value_mix, v_first):
    n, d = x.shape
    tm = ROW_TILE
    row = lambda i: (i, 0)
    lora = lambda w_in, w_out, width: (_pad_cols(w_in, width).astype(BF16), _pad_rows(w_out, width).astype(BF16))
    w1p, w2p = lora(w1, w2, LANES)
    a1p, a2p = lora(a1, a2, LANES)
    g1p, g2p = lora(g1, g2, 2 * LANES)
    mu8 = _pad_rows(mu, 8)
    ins = [x, x, mu8, w_rkv.astype(BF16), w0.reshape(1, d), w1p, w2p, a0.reshape(1, d), a1p, a2p, g1p, g2p]
    specs = [
        pl.BlockSpec((tm, d), row),
        pl.BlockSpec((8, d), lambda i: (jnp.maximum(i * (tm // 8) - 1, 0), 0)),
        _const_spec((8, d)), _const_spec((3, d, d)), _const_spec((1, d)),
        _const_spec((d, LANES)), _const_spec((LANES, d)), _const_spec((1, d)),
        _const_spec((d, LANES)), _const_spec((LANES, d)),
        _const_spec((d, 2 * LANES)), _const_spec((2 * LANES, d)),
    ]
    has_mix = value_mix is not None
    if has_mix:
        v0, v1, v2 = value_mix
        v1p, v2p = lora(v1, v2, LANES)
        ins += [v0.reshape(1, d), v1p, v2p, v_first]
        specs += [_const_spec((1, d)), _const_spec((d, LANES)), _const_spec((LANES, d)), pl.BlockSpec((tm, d), row)]
    out = jax.ShapeDtypeStruct((n, d), F32)
    return pl.pallas_call(
        functools.partial(_rwkv_proj_kernel, seq=seq, has_mix=has_mix),
        grid=(n // tm,),
        in_specs=specs,
        out_specs=[pl.BlockSpec((tm, d), row)] * 6,
        out_shape=[out] * 6,
        compiler_params=_params("parallel"),
        name="rwkv_proj",
    )(*ins)


def _wkv_kernel(r_ref, k_ref, v_ref, z_ref, a_ref, g_ref, kk_ref, ka_ref, rk_ref, lg_ref, lb_ref, o_ref, s_ref,
                *, tblock, npair):
    c = WKV_CHUNK
    hn = RW_HEAD
    w = 2 * hn
    wt = npair * w

    @pl.when(pl.program_id(2) == 0)
    def _():
        s_ref[...] = jnp.zeros_like(s_ref)

    def iota2(shape, dim):
        return lax.broadcasted_iota(jnp.int32, shape, dim)

    lane_c = iota2((c, w), 1)
    head0 = lane_c < hn
    rw, cw = iota2((w, w), 0), iota2((w, w), 1)
    same_head = (rw // hn) == (cw // hn)
    strict = same_head & ((cw % hn) < (rw % hn))
    incl = same_head & ((cw % hn) <= (rw % hn))
    blk16 = (rw // 16) == (cw // 16)
    blk32 = (rw // 32) == (cw // 32)
    eye = (rw == cw).astype(F32)
    r3, c3 = iota2((3 * wt, wt), 0), iota2((3 * wt, wt), 1)
    seg_ones3 = (((r3 % wt) // hn) == (c3 // hn)).astype(BF16)
    rt3, ct3 = iota2((tblock, 3 * tblock), 0), iota2((tblock, 3 * tblock), 1) % tblock
    tri3 = ((ct3 <= rt3) & ((ct3 // c) == (rt3 // c))).astype(BF16)

    def split3(t):
        p1 = t.astype(BF16)
        rest = t - p1.astype(F32)
        p2 = rest.astype(BF16)
        return p1, p2, (rest - p2.astype(F32)).astype(BF16)

    def seg_sum(t):
        return _dot(jnp.concatenate(split3(t), axis=1), seg_ones3)

    def stack(t):
        zero = jnp.zeros_like(t)
        return jnp.concatenate([jnp.where(head0, t, zero), jnp.where(head0, zero, t)], axis=0)

    def fold(t):
        return t[:c] + t[c:]

    def bf(t):
        return t.astype(BF16)

    nch = tblock // c
    chunks = range(nch)
    items = [(ci, p) for ci in chunks for p in range(npair)]

    def rows(t, ci):
        return t[ci * c:(ci + 1) * c]

    def sub(t, it):
        ci, p = it
        return t[ci * c:(ci + 1) * c, p * w:(p + 1) * w]

    r, k, v, z, a, g = (ref[0] for ref in (r_ref, k_ref, v_ref, z_ref, a_ref, g_ref))
    kk = k * kk_ref[...]
    kk = kk / jnp.maximum(jnp.sqrt(seg_sum(kk * kk)), 1e-12)
    kmod = k * (1.0 + (a - 1.0) * ka_ref[...])
    bvec = kk * a
    lw = (-math.exp(-0.5)) / (1.0 + jnp.exp(-z))
    cum = _dot(tri3, jnp.concatenate(split3(lw), axis=0))
    at = -kk * jnp.exp(cum - lw)
    rt = r * jnp.exp(cum)
    inv = jnp.exp(-cum)
    bt = bvec * inv
    kt = kmod * inv
    totals = [cum[(ci + 1) * c - 1:(ci + 1) * c, :] for ci in chunks]
    to_end = jnp.concatenate([jnp.exp(totals[ci] - rows(cum, ci)) for ci in chunks], axis=0)
    bh = bvec * to_end
    kh = kmod * to_end

    n_it = range(len(items))
    ast = [bf(stack(sub(at, it))) for it in items]
    rst = [bf(stack(sub(rt, it))) for it in items]
    bst = [bf(stack(sub(bt, it))) for it in items]
    kst = [bf(stack(sub(kt, it))) for it in items]
    l_full = [jnp.where(strict, _dot_nt(ast[i], bst[i]), 0.0) for i in n_it]
    a_ak = [bf(fold(jnp.where(strict, _dot_nt(ast[i], kst[i]), 0.0))) for i in n_it]
    a_rb = [bf(fold(jnp.where(incl, _dot_nt(rst[i], bst[i]), 0.0))) for i in n_it]
    a_rk = [bf(fold(jnp.where(incl, _dot_nt(rst[i], kst[i]), 0.0))) for i in n_it]
    vst = [bf(stack(sub(v, it))) for it in items]
    x_loc = [_dot(a_ak[i], vst[i]) for i in n_it]
    y_loc = [_dot(a_rk[i], vst[i]) for i in n_it]

    xp = [bf(jnp.where(blk16, l_full[i], 0.0)) for i in n_it]
    tm = [eye + jnp.where(blk16, l_full[i], 0.0) for i in n_it]
    for _ in range(3):
        xp = [bf(_dot(xp[i], xp[i])) for i in n_it]
        tm = [tm[i] + _dot(bf(tm[i]), xp[i]) for i in n_it]
    for inside, outside in ((blk32, blk16), (same_head, blk32)):
        off = [bf(jnp.where(inside & jnp.logical_not(outside), l_full[i], 0.0)) for i in n_it]
        tmb = [bf(tm[i]) for i in n_it]
        half = [bf(_dot(tmb[i], off[i])) for i in n_it]
        tm = [tm[i] + _dot(half[i], tmb[i]) for i in n_it]
    t_fold = [bf(fold(tm[i])) for i in n_it]

    au = [_dot(t_fold[i], jnp.concatenate([ast[i], bf(stack(x_loc[i]))], axis=1)) for i in n_it]
    ah = [au[i][:, :w] for i in n_it]
    ul = [au[i][:, w:] for i in n_it]
    ry = [_dot(a_rb[i], jnp.concatenate([bf(stack(ah[i])), bf(stack(ul[i]))], axis=1)) for i in n_it]
    rh = [bf(sub(rt, items[i]) + ry[i][:, :w]) for i in n_it]
    yl = [y_loc[i] + ry[i][:, w:] for i in n_it]
    w_mat = [bf(jnp.where(same_head, _dot_tn(bf(ah[i]), bf(sub(bh, items[i]))), 0.0)) for i in n_it]
    g_mat = [jnp.where(same_head,
                       _dot_tn(bf(jnp.concatenate([ul[i], sub(v, items[i])], axis=0)),
                               bf(jnp.concatenate([sub(bh, items[i]), sub(kh, items[i])], axis=0))), 0.0)
             for i in n_it]
    decay_c = [jnp.exp(totals[ci][:, p * w:(p + 1) * w]) for ci, p in items]

    s = [s_ref[p] for p in range(npair)]
    ys = [[None] * npair for _ in chunks]
    for i, (ci, p) in enumerate(items):
        sb = bf(s[p])
        ys[ci][p] = _dot_nt(rh[i], sb) + yl[i]
        s[p] = s[p] * decay_c[i] + _dot(sb, w_mat[i]) + g_mat[i]
    for p in range(npair):
        s_ref[p] = s[p]

    y = jnp.concatenate([jnp.concatenate(ys[ci], axis=1) if npair > 1 else ys[ci][0] for ci in chunks], axis=0)
    mean = seg_sum(y) * (1.0 / hn)
    yc = y - mean
    var = seg_sum(yc * yc) * (1.0 / hn)
    yn = yc * lax.rsqrt(var + RW_GN_EPS) * lg_ref[...] + lb_ref[...]
    yn = yn + seg_sum(r * kmod * rk_ref[...]) * v
    o_ref[0] = (yn * g).astype(o_ref.dtype)


def _wkv(r, k, v, z, a, g, k_k, k_a, r_k, lnx_g, lnx_b, batch, seq):
    n, d = r.shape
    w = 2 * RW_HEAD
    npair = WKV_PAIRS
    wt = npair * w
    tb = WKV_TBLOCK
    seq_spec = pl.BlockSpec((1, tb, wt), lambda b, h, t: (b, t, h))
    par_spec = pl.BlockSpec((1, wt), lambda b, h, t: (0, h))
    r3 = lambda t: t.reshape(batch, seq, d)
    out = pl.pallas_call(
        functools.partial(_wkv_kernel, tblock=tb, npair=npair),
        grid=(batch, d // wt, seq // tb),
        in_specs=[seq_spec] * 6 + [par_spec] * 5,
        out_specs=seq_spec,
        out_shape=jax.ShapeDtypeStruct((batch, seq, d), BF16),
        scratch_shapes=[pltpu.VMEM((npair, w, w), F32)],
        compiler_params=_params("parallel", "parallel", "arbitrary"),
        name="wkv7_chunked",
    )(r3(r), r3(k), r3(v), r3(z), r3(a), r3(g), k_k.reshape(1, d), k_a.reshape(1, d), r_k.reshape(1, d),
      lnx_g.reshape(1, d), lnx_b.reshape(1, d))
    return out.reshape(n, d)


def kernel(x, ln1_g, ln1_b, ln2_g, ln2_b, attn_w_qkv, attn_w_o, attn_lambda, attn_subln_g, rw_mu, rw_w_rkv, rw_w_o, rw_w0, rw_w1, rw_w2, rw_a0, rw_a1, rw_a2, rw_g1, rw_g2, rw_k_k, rw_k_a, rw_r_k, rw_lnx_g, rw_lnx_b, rw_v0, rw_v1, rw_v2, moe_rg_w, moe_rg_b, moe_re_w, moe_re_b, moe_w_gu, moe_w_down):
    batch, seq, d = x.shape
    depth = ln1_g.shape[0]
    n = batch * seq
    alpha = (2 * depth) ** 0.25
    x = x.reshape(n, d)
    v_first = None
    for i in range(depth):
        j = i // N_MIXERS
        if i % N_MIXERS == 0:
            lambda_init = 0.8 - 0.6 * math.exp(-0.3 * i)
            lam = attn_lambda[j]
            lam_full = jnp.exp(jnp.sum(lam[0] * lam[1])) - jnp.exp(jnp.sum(lam[2] * lam[3])) + lambda_init
            lam_row = jnp.full((1, 2 * DA_HEAD_DIM), lam_full, F32)
            q_scale = DA_HEAD_DIM ** -0.5 * math.log2(math.e)
            col_scale = jnp.concatenate([jnp.full((d,), q_scale, F32), jnp.ones((2 * d,), F32)])
            w_qkv = (attn_w_qkv[j] * col_scale[None, :]).astype(BF16)
            qkv = _proj(x, w_qkv, BF16)
            mixed = _diff_attention(qkv, lam_row, attn_subln_g[j], lambda_init, batch, seq)
            w_o = attn_w_o[j]
        else:
            value_mix = None if j == 0 else (rw_v0[j - 1], rw_v1[j - 1], rw_v2[j - 1])
            r, k, v, z, a, g = _rwkv_proj(x, seq, rw_mu[j], rw_w_rkv[j], rw_w0[j], rw_w1[j], rw_w2[j], rw_a0[j],
                                          rw_a1[j], rw_a2[j], rw_g1[j], rw_g2[j], value_mix, v_first)
            if value_mix is None:
                v_first = v
            mixed = _wkv(r, k, v, z, a, g, rw_k_k[j], rw_k_a[j], rw_r_k[j].reshape(d), rw_lnx_g[j], rw_lnx_b[j],
                         batch, seq)
            w_o = rw_w_o[j]
        n_router = MOE_GROUPS + MOE_EXPERTS
        w_router = _pad_cols(jnp.concatenate([moe_rg_w[i], moe_re_w[i]], axis=1), LANES)
        wr_hi = w_router.astype(BF16)
        wr_lo = (w_router - wr_hi.astype(F32)).astype(BF16)
        w_router = jnp.concatenate([wr_hi, wr_hi, wr_lo], axis=0)
        b_router = _pad_cols(jnp.concatenate([moe_rg_b[i], moe_re_b[i]]).reshape(1, n_router), LANES)
        x1, logits = _post_mixer(mixed, x, w_o.astype(BF16), ln1_g[i], ln1_b[i], w_router, b_router, alpha)
        x = _hier_moe_block(x1, logits, moe_w_gu[i], moe_w_down[i], ln2_g[i], ln2_b[i], alpha)
    return x.reshape(batch, seq, d)
```

```python
import functools
import math

import jax
import jax.numpy as jnp
from jax import lax
from jax.experimental import pallas as pl
from jax.experimental.pallas import tpu as pltpu

F32 = jnp.float32
BF16 = jnp.bfloat16

LANES = 128
VMEM_LIMIT = 56 * 1024 * 1024

DA_HEADS = 8
DA_HEAD_DIM = 64
RMS_EPS = 1e-5
RW_HEAD = 64
RW_GN_EPS = 64e-5
MOE_GROUPS = 4
MOE_EPG = 8
MOE_EXPERTS = MOE_GROUPS * MOE_EPG
MOE_TOPK = 2
LN_EPS = 1e-5
N_MIXERS = 2

ROW_TILE = 256
ROUTER_TILE = 512
ATTN_TQ = 512
ATTN_TK = 512
WKV_CHUNK = 64
WKV_TBLOCK = 512
WKV_PAIRS = 2
MOE_BLOCK = 256
DMA_UNROLL = 8
NEG_BIG = -1e30


def _dot(a, b, precision=None):
    return jnp.dot(a, b, preferred_element_type=F32, precision=precision)


def _dot_nt(a, b):
    return lax.dot_general(a, b, (((1,), (1,)), ((), ())), preferred_element_type=F32)


def _dot_tn(a, b):
    return lax.dot_general(a, b, (((0,), (0,)), ((), ())), preferred_element_type=F32)


def _params(*sem):
    return pltpu.CompilerParams(dimension_semantics=sem, vmem_limit_bytes=VMEM_LIMIT)


def _const_spec(shape):
    nd = len(shape)
    return pl.BlockSpec(shape, lambda *_: (0,) * nd)


def _proj_kernel(x_ref, w_ref, o_ref):
    o_ref[...] = _dot(x_ref[...].astype(BF16), w_ref[...]).astype(o_ref.dtype)


def _proj(x, w, out_dtype):
    n, d = x.shape
    m = w.shape[1]
    tm = 512
    return pl.pallas_call(
        _proj_kernel,
        grid=(n // tm,),
        in_specs=[pl.BlockSpec((tm, d), lambda i: (i, 0)), _const_spec((d, m))],
        out_specs=pl.BlockSpec((tm, m), lambda i: (i, 0)),
        out_shape=jax.ShapeDtypeStruct((n, m), out_dtype),
        compiler_params=_params("parallel"),
        name="qkv_proj",
    )(x, w)


def _attn_kernel(lam_ref, g_ref, q_ref, k_ref, v_ref, o_ref, vx_ref, m_ref, acc_ref, sa_ref, sb_ref, *, tq, tk,
                 out_scale):
    qi = pl.program_id(2)
    hd = DA_HEAD_DIM
    hw = 2 * hd

    @pl.when(qi == 0)
    def _():
        vx_ref[:, :hw] = v_ref[0]
        vx_ref[:, hw:] = jnp.ones((vx_ref.shape[0], hw), BF16)

    q = q_ref[0]
    lane = lax.broadcasted_iota(jnp.int32, q.shape, 1)
    zero = jnp.zeros_like(q)
    q_stack = jnp.concatenate([jnp.where(lane < hd, q, zero), jnp.where(lane >= hd, q, zero)], axis=0)
    m_ref[...] = jnp.full(m_ref.shape, NEG_BIG, F32)
    acc_ref[...] = jnp.zeros(acc_ref.shape, F32)
    row0 = qi * tq

    def scores(j, s_ref):
        start = pl.multiple_of(j * tk, tk)
        s_ref[...] = _dot_nt(q_stack, k_ref[0, pl.ds(start, tk), :])

    def consume(j, s_ref, masked):
        start = pl.multiple_of(j * tk, tk)
        vb = vx_ref[pl.ds(start, tk), :]
        if masked:
            row = row0 + lax.broadcasted_iota(jnp.int32, (tq, tk), 0)
            col = start + lax.broadcasted_iota(jnp.int32, (tq, tk), 1)
            keep = col <= row
        for c in range(2):
            s = s_ref[c * tq:(c + 1) * tq, :]
            if masked:
                s = jnp.where(keep, s, NEG_BIG)
            chunks = [s[:, i * LANES:(i + 1) * LANES] for i in range(tk // LANES)]
            m_old = m_ref[c]
            m_new = jnp.maximum(m_old, jnp.max(functools.reduce(jnp.maximum, chunks), axis=-1, keepdims=True))
            alpha = jnp.exp2(m_old - m_new)
            m_ref[c] = m_new
            p = jnp.concatenate([jnp.exp2(ch - m_new).astype(BF16) for ch in chunks], axis=1)
            acc_ref[c] = acc_ref[c] * jnp.concatenate([alpha, alpha], axis=1) + _dot(p, vb)

    n = row0 // tk + 1
    n_pairs = (n - 1) // 2
    scores(0, sa_ref)

    def pair(jj, carry):
        j = 2 * jj
        scores(j + 1, sb_ref)
        consume(j, sa_ref, False)
        scores(j + 2, sa_ref)
        consume(j + 1, sb_ref, False)
        return carry

    lax.fori_loop(0, n_pairs, pair, 0)
    jb = 2 * n_pairs

    @pl.when(n % 2 == 1)
    def _():
        consume(jb, sa_ref, True)

    @pl.when(n % 2 == 0)
    def _():
        scores(jb + 1, sb_ref)
        consume(jb, sa_ref, False)
        consume(jb + 1, sb_ref, True)

    a0, a1 = acc_ref[0], acc_ref[1]
    o = a0[:, :hw] * (1.0 / a0[:, hw:]) - lam_ref[...] * (a1[:, :hw] * (1.0 / a1[:, hw:]))
    ms = jnp.mean(o * o, axis=-1, keepdims=True)
    o = o * lax.rsqrt(ms + RMS_EPS) * g_ref[...] * out_scale
    o_ref[0] = o.astype(o_ref.dtype)


def _diff_attention(qkv, lam_row, subln_g, lambda_init, batch, seq):
    d = DA_HEADS * 2 * DA_HEAD_DIM
    qkv = qkv.reshape(batch, seq, 3 * d)
    tq, tk = ATTN_TQ, ATTN_TK
    assert tq <= tk and tk % tq == 0 and seq % tk == 0
    hw = 2 * DA_HEAD_DIM
    out = pl.pallas_call(
        functools.partial(_attn_kernel, tq=tq, tk=tk, out_scale=1.0 - lambda_init),
        grid=(batch, DA_HEADS, seq // tq),
        in_specs=[
            _const_spec((1, hw)),
            _const_spec((1, hw)),
            pl.BlockSpec((1, tq, hw), lambda b, h, i: (b, i, h)),
            pl.BlockSpec((1, seq, hw), lambda b, h, i: (b, 0, DA_HEADS + h)),
            pl.BlockSpec((1, seq, hw), lambda b, h, i: (b, 0, 2 * DA_HEADS + h)),
        ],
        out_specs=pl.BlockSpec((1, tq, hw), lambda b, h, i: (b, i, h)),
        out_shape=jax.ShapeDtypeStruct((batch, seq, d), BF16),
        scratch_shapes=[pltpu.VMEM((seq, 2 * hw), BF16), pltpu.VMEM((2, tq, hw), F32),
                        pltpu.VMEM((2, tq, 2 * hw), F32), pltpu.VMEM((2 * tq, tk), F32),
                        pltpu.VMEM((2 * tq, tk), F32)],
        compiler_params=_params("parallel", "parallel", "arbitrary"),
        name="diff_attn",
    )(lam_row, subln_g.reshape(1, hw), qkv, qkv, qkv)
    return out.reshape(batch * seq, d)


def _layer_norm_rows(z, g, b):
    mu = jnp.mean(z, axis=-1, keepdims=True)
    zc = z - mu
    var = jnp.mean(zc * zc, axis=-1, keepdims=True)
    return zc * lax.rsqrt(var + LN_EPS) * g + b


def _post_mixer_kernel(o_ref, x_ref, w_ref, lg_ref, lb_ref, wr_ref, br_ref, x1_ref, lo_ref, *, alpha):
    h = _dot(o_ref[...], w_ref[...])
    x1 = _layer_norm_rows(alpha * x_ref[...] + h, lg_ref[...], lb_ref[...])
    x1_ref[...] = x1
    x_hi = x1.astype(BF16)
    x_lo = (x1 - x_hi.astype(F32)).astype(BF16)
    lo_ref[...] = _dot(jnp.concatenate([x_hi, x_lo, x_hi], axis=1), wr_ref[...]) + br_ref[...]


def _post_mixer(o, x, w_o, ln_g, ln_b, w_router, b_router, alpha):
    n, d = x.shape
    tm = ROW_TILE
    row = lambda i: (i, 0)
    return pl.pallas_call(
        functools.partial(_post_mixer_kernel, alpha=alpha),
        grid=(n // tm,),
        in_specs=[
            pl.BlockSpec((tm, d), row), pl.BlockSpec((tm, d), row), _const_spec((d, d)),
            _const_spec((1, d)), _const_spec((1, d)), _const_spec((3 * d, LANES)), _const_spec((1, LANES)),
        ],
        out_specs=[pl.BlockSpec((tm, d), row), pl.BlockSpec((tm, LANES), row)],
        out_shape=[jax.ShapeDtypeStruct((n, d), F32), jax.ShapeDtypeStruct((n, LANES), F32)],
        compiler_params=_params("parallel"),
        name="post_mixer",
    )(o, x, w_o, ln_g.reshape(1, d), ln_b.reshape(1, d), w_router, b_router)


def _router_kernel(lo_ref, idx_ref, gate_ref, cnt_ref, carry_ref):
    step = pl.program_id(0)

    @pl.when(step == 0)
    def _():
        carry_ref[...] = jnp.zeros_like(carry_ref)

    lo = lo_ref[...]
    tm = lo.shape[0]
    lane = lax.broadcasted_iota(jnp.int32, lo.shape, 1)
    big = jnp.int32(LANES)

    def softmax_over(mask):
        mx = jnp.max(jnp.where(mask, lo, NEG_BIG), axis=-1, keepdims=True)
        ex = jnp.where(mask, jnp.exp(lo - mx), 0.0)
        return ex / jnp.sum(ex, axis=-1, keepdims=True)

    def top1(p, mask):
        best = jnp.max(jnp.where(mask, p, -1.0), axis=-1, keepdims=True)
        arg = jnp.min(jnp.where(mask & (p == best), lane, big), axis=-1, keepdims=True)
        return best, arg

    g_mask = lane < MOE_GROUPS
    g_p, g_idx = top1(softmax_over(g_mask), g_mask)
    first = MOE_GROUPS + MOE_EPG * g_idx
    e_mask = (lane >= first) & (lane < first + MOE_EPG)
    e_prob = softmax_over(e_mask)
    p1, i1 = top1(e_prob, e_mask)
    rest = e_mask & (lane != i1)
    p2, i2 = top1(e_prob, rest)
    denom = p1 + p2
    gate1 = g_p * (p1 / denom)
    gate2 = g_p * (p2 / denom)
    e1 = i1 - MOE_GROUPS
    e2 = i2 - MOE_GROUPS

    oh1 = (lane == e1).astype(F32)
    oh2 = (lane == e2).astype(F32)
    both = oh1 + oh2
    r_i = lax.broadcasted_iota(jnp.int32, (tm, tm), 0)
    c_i = lax.broadcasted_iota(jnp.int32, (tm, tm), 1)
    before = (c_i < r_i).astype(BF16)
    seen = _dot(before, both.astype(BF16)) + carry_ref[...]
    pos1 = jnp.sum(seen * oh1, axis=-1, keepdims=True).astype(jnp.int32)
    pos2 = jnp.sum(seen * oh2, axis=-1, keepdims=True).astype(jnp.int32)
    carry_ref[...] += jnp.sum(both, axis=0, keepdims=True)
    cnt_ref[...] = carry_ref[...].astype(jnp.int32)

    zero_i = jnp.zeros(lo.shape, jnp.int32)
    idx_ref[...] = (jnp.where(lane == 0, e1, zero_i) + jnp.where(lane == 1, e2, zero_i)
                    + jnp.where(lane == 2, pos1, zero_i) + jnp.where(lane == 3, pos2, zero_i))
    gate_ref[...] = jnp.where(lane == 0, gate1, 0.0) + jnp.where(lane == 1, gate2, 0.0)


def _router(logits):
    n = logits.shape[0]
    tm = ROUTER_TILE
    row = lambda i: (i, 0)
    return pl.pallas_call(
        _router_kernel,
        grid=(n // tm,),
        in_specs=[pl.BlockSpec((tm, LANES), row)],
        out_specs=[pl.BlockSpec((tm, LANES), row), pl.BlockSpec((tm, LANES), row), _const_spec((1, LANES))],
        out_shape=[jax.ShapeDtypeStruct((n, LANES), jnp.int32), jax.ShapeDtypeStruct((n, LANES), F32),
                   jax.ShapeDtypeStruct((1, LANES), jnp.int32)],
        scratch_shapes=[pltpu.VMEM((1, LANES), F32)],
        compiler_params=_params("arbitrary"),
        name="moe_router",
    )(logits)


def _row_copy(src_ref, src_row, dst_ref, dst_row, sem):
    return pltpu.make_async_copy(src_ref.at[pl.ds(src_row, 1)], dst_ref.at[pl.ds(dst_row, 1)], sem)


def _dispatch_kernel(dest_ref, x_ref, buf_in_ref, buf_ref, sem):
    del buf_in_ref
    tm = x_ref.shape[0]

    def copies(r):
        return [_row_copy(x_ref, r, buf_ref, dest_ref[0, 0, MOE_TOPK * r + s], sem) for s in range(MOE_TOPK)]

    def start(r, carry):
        for cp in copies(r):
            cp.start()
        return carry

    def wait(r, carry):
        for cp in copies(r):
            cp.wait()
        return carry

    lax.fori_loop(0, tm, start, 0, unroll=DMA_UNROLL)
    lax.fori_loop(0, tm, wait, 0, unroll=DMA_UNROLL)


def _dispatch(x1, dest_tiles, n_rows):
    n, d = x1.shape
    tm = ROW_TILE
    zeros = jnp.zeros((n_rows, d), x1.dtype)
    return pl.pallas_call(
        _dispatch_kernel,
        grid=(n // tm,),
        in_specs=[
            pl.BlockSpec((1, 1, MOE_TOPK * tm), lambda i: (i, 0, 0), memory_space=pltpu.SMEM),
            pl.BlockSpec((tm, d), lambda i: (i, 0)),
            pl.BlockSpec(memory_space=pl.ANY),
        ],
        out_specs=pl.BlockSpec(memory_space=pl.ANY),
        out_shape=jax.ShapeDtypeStruct((n_rows, d), x1.dtype),
        scratch_shapes=[pltpu.SemaphoreType.DMA(())],
        input_output_aliases={2: 0},
        compiler_params=_params("arbitrary"),
        name="moe_dispatch",
    )(dest_tiles, x1, zeros)


def _expert_kernel(be_ref, x_ref, wgu_ref, wd_ref, y_ref, wgu_bf_ref, wd_bf_ref, *, hidden):
    i = pl.program_id(0)

    @pl.when((i == 0) | (be_ref[i] != be_ref[jnp.maximum(i - 1, 0)]))
    def _():
        wgu_bf_ref[...] = wgu_ref[0].astype(BF16)
        wd_bf_ref[...] = wd_ref[0].astype(BF16)

    n_parts = 2
    part = x_ref.shape[0] // n_parts
    rows = [pl.ds(p * part, part) for p in range(n_parts)]
    hs = [_dot(x_ref[r, :].astype(BF16), wgu_bf_ref[...]) for r in rows]
    for r, h in zip(rows, hs):
        hg = h[:, :hidden]
        hu = h[:, hidden:]
        act = hg * (1.0 / (1.0 + jnp.exp(-hg))) * hu
        y_ref[r, :] = _dot(act.astype(BF16), wd_bf_ref[...])


def _experts(x_buf, block_e, w_gu, w_down):
    n_rows, d = x_buf.shape
    hidden = w_down.shape[1]
    tm = MOE_BLOCK
    grid_spec = pltpu.PrefetchScalarGridSpec(
        num_scalar_prefetch=1,
        grid=(n_rows // tm,),
        in_specs=[
            pl.BlockSpec((tm, d), lambda i, be: (i, 0)),
            pl.BlockSpec((1, d, 2 * hidden), lambda i, be: (be[i], 0, 0)),
            pl.BlockSpec((1, hidden, d), lambda i, be: (be[i], 0, 0)),
        ],
        out_specs=pl.BlockSpec((tm, d), lambda i, be: (i, 0)),
        scratch_shapes=[pltpu.VMEM((d, 2 * hidden), BF16), pltpu.VMEM((hidden, d), BF16)],
    )
    return pl.pallas_call(
        functools.partial(_expert_kernel, hidden=hidden),
        grid_spec=grid_spec,
        out_shape=jax.ShapeDtypeStruct((n_rows, d), F32),
        compiler_params=_params("arbitrary"),
        name="moe_experts",
    )(block_e, x_buf, w_gu, w_down)


def _combine_kernel(dest_ref, y_ref, gate_ref, x_ref, lg_ref, lb_ref, o_ref, rows_ref, sem, *, alpha):
    tm = x_ref.shape[0]

    def copies(r):
        return [_row_copy(y_ref, dest_ref[0, 0, MOE_TOPK * r + s], rows_ref.at[s], r, sem)
                for s in range(MOE_TOPK)]

    def start(r, carry):
        for cp in copies(r):
            cp.start()
        return carry

    def wait(r, carry):
        for cp in copies(r):
            cp.wait()
        return carry

    lax.fori_loop(0, tm, start, 0, unroll=DMA_UNROLL)
    lax.fori_loop(0, tm, wait, 0, unroll=DMA_UNROLL)
    gate = gate_ref[...]
    f = sum(gate[:, s:s + 1] * rows_ref[s] for s in range(MOE_TOPK))
    o_ref[...] = _layer_norm_rows(alpha * x_ref[...] + f, lg_ref[...], lb_ref[...])


def _combine(y, dest_tiles, gates, x1, ln_g, ln_b, alpha):
    n, d = x1.shape
    tm = ROW_TILE
    row = lambda i: (i, 0)
    return pl.pallas_call(
        functools.partial(_combine_kernel, alpha=alpha),
        grid=(n // tm,),
        in_specs=[
            pl.BlockSpec((1, 1, MOE_TOPK * tm), lambda i: (i, 0, 0), memory_space=pltpu.SMEM),
            pl.BlockSpec(memory_space=pl.ANY),
            pl.BlockSpec((tm, LANES), row), pl.BlockSpec((tm, d), row), _const_spec((1, d)), _const_spec((1, d)),
        ],
        out_specs=pl.BlockSpec((tm, d), row),
        out_shape=jax.ShapeDtypeStruct((n, d), F32),
        scratch_shapes=[pltpu.VMEM((MOE_TOPK, tm, d), F32), pltpu.SemaphoreType.DMA(())],
        compiler_params=_params("arbitrary"),
        name="moe_combine",
    )(dest_tiles, y, gates, x1, ln_g.reshape(1, d), ln_b.reshape(1, d))


def _hier_moe_block(x1, logits, w_gu, w_down, layer, ln_g, ln_b, alpha):
    n_tok, d = x1.shape
    idx, gates, counts = _router(logits)
    counts = counts[0, :MOE_EXPERTS]
    padded = (counts + MOE_BLOCK - 1) // MOE_BLOCK * MOE_BLOCK
    pad_end = jnp.cumsum(padded)
    pad_start = pad_end - padded
    expert = idx[:, :MOE_TOPK]
    pos = idx[:, MOE_TOPK:2 * MOE_TOPK]
    e_iota = jnp.arange(MOE_EXPERTS, dtype=jnp.int32)
    base = jnp.sum(jnp.where(expert[:, :, None] == e_iota, pad_start, 0), axis=-1)
    dest = (base + pos).astype(jnp.int32)
    n_blocks = n_tok * MOE_TOPK // MOE_BLOCK + MOE_EXPERTS
    n_rows = n_blocks * MOE_BLOCK
    block_start = jnp.arange(n_blocks, dtype=jnp.int32) * MOE_BLOCK
    block_e = jnp.minimum(jnp.sum((pad_end[None, :] <= block_start[:, None]).astype(jnp.int32), axis=1),
                          MOE_EXPERTS - 1) + layer * MOE_EXPERTS
    dest_tiles = dest.reshape(n_tok // ROW_TILE, 1, MOE_TOPK * ROW_TILE)
    x_buf = _dispatch(x1, dest_tiles, n_rows)
    y = _experts(x_buf, block_e, w_gu, w_down)
    return _combine(y, dest_tiles, gates, x1, ln_g, ln_b, alpha)


def _rwkv_proj_kernel(*refs, seq, has_mix):
    if has_mix:
        (x_ref, xp_ref, mu_ref, wrkv_ref, w0_ref, w1_ref, w2_ref, a0_ref, a1_ref, a2_ref, g1_ref, g2_ref,
         v0_ref, v1_ref, v2_ref, vf_ref, r_ref, k_ref, v_ref, z_ref, a_ref, g_ref) = refs
    else:
        (x_ref, xp_ref, mu_ref, wrkv_ref, w0_ref, w1_ref, w2_ref, a0_ref, a1_ref, a2_ref, g1_ref, g2_ref,
         r_ref, k_ref, v_ref, z_ref, a_ref, g_ref) = refs
    x = x_ref[...]
    tm = x.shape[0]
    row = lax.broadcasted_iota(jnp.int32, x.shape, 0)
    at_seq_start = (pl.program_id(0) * tm) % seq == 0
    prev_last = jnp.where(at_seq_start, 0.0, xp_ref[7:8, :])
    shifted = jnp.where(row == 0, prev_last, pltpu.roll(x, 1, axis=0))
    xx = shifted - x

    def mixed(n):
        return (x + xx * mu_ref[n:n + 1, :]).astype(BF16)

    def sigmoid(t):
        return 1.0 / (1.0 + jnp.exp(-t))

    r_ref[...] = _dot(mixed(0), wrkv_ref[0])
    k_ref[...] = _dot(mixed(1), wrkv_ref[1])
    xv = mixed(2)
    v = _dot(xv, wrkv_ref[2])
    if has_mix:
        mix = sigmoid(v0_ref[...] + _dot(_dot(xv, v1_ref[...]).astype(BF16), v2_ref[...]))
        v = v + (vf_ref[...] - v) * mix
    v_ref[...] = v
    z_ref[...] = w0_ref[...] + _dot(jnp.tanh(_dot(mixed(3), w1_ref[...])).astype(BF16), w2_ref[...])
    a_ref[...] = sigmoid(a0_ref[...] + _dot(_dot(mixed(4), a1_ref[...]).astype(BF16), a2_ref[...]))
    g_ref[...] = _dot(sigmoid(_dot(mixed(5), g1_ref[...])).astype(BF16), g2_ref[...])


def _pad_cols(w, width):
    return jnp.pad(w, ((0, 0), (0, width - w.shape[1])))


def _pad_rows(w, height):
    return jnp.pad(w, ((0, height - w.shape[0]), (0, 0)))


def _rwkv_proj(x, seq, mu, w_rkv, w0, w1, w2, a0, a1, a2, g1, g2, value_mix, v_first):
    n, d = x.shape
    tm = ROW_TILE
    row = lambda i: (i, 0)
    lora = lambda w_in, w_out, width: (_pad_cols(w_in, width).astype(BF16), _pad_rows(w_out, width).astype(BF16))
    w1p, w2p = lora(w1, w2, LANES)
    a1p, a2p = lora(a1, a2, LANES)
    g1p, g2p = lora(g1, g2, 2 * LANES)
    mu8 = _pad_rows(mu, 8)
    ins = [x, x, mu8, w_rkv.astype(BF16), w0.reshape(1, d), w1p, w2p, a0.reshape(1, d), a1p, a2p, g1p, g2p]
    specs = [
        pl.BlockSpec((tm, d), row),
        pl.BlockSpec((8, d), lambda i: (jnp.maximum(i * (tm // 8) - 1, 0), 0)),
        _const_spec((8, d)), _const_spec((3, d, d)), _const_spec((1, d)),
        _const_spec((d, LANES)), _const_spec((LANES, d)), _const_spec((1, d)),
        _const_spec((d, LANES)), _const_spec((LANES, d)),
        _const_spec((d, 2 * LANES)), _const_spec((2 * LANES, d)),
    ]
    has_mix = value_mix is not None
    if has_mix:
        v0, v1, v2 = value_mix
        v1p, v2p = lora(v1, v2, LANES)
        ins += [v0.reshape(1, d), v1p, v2p, v_first]
        specs += [_const_spec((1, d)), _const_spec((d, LANES)), _const_spec((LANES, d)), pl.BlockSpec((tm, d), row)]
    out = jax.ShapeDtypeStruct((n, d), F32)
    return pl.pallas_call(
        functools.partial(_rwkv_proj_kernel, seq=seq, has_mix=has_mix),
        grid=(n // tm,),
        in_specs=specs,
        out_specs=[pl.BlockSpec((tm, d), row)] * 6,
        out_shape=[out] * 6,
        compiler_params=_params("parallel"),
        name="rwkv_proj",
    )(*ins)


def _wkv_kernel(r_ref, k_ref, v_ref, z_ref, a_ref, g_ref, kk_ref, ka_ref, rk_ref, lg_ref, lb_ref, o_ref, s_ref,
                *, tblock, npair):
    c = WKV_CHUNK
    hn = RW_HEAD
    w = 2 * hn
    wt = npair * w

    @pl.when(pl.program_id(2) == 0)
    def _():
        s_ref[...] = jnp.zeros_like(s_ref)

    def iota2(shape, dim):
        return lax.broadcasted_iota(jnp.int32, shape, dim)

    lane_c = iota2((c, w), 1)
    head0 = lane_c < hn
    rw, cw = iota2((w, w), 0), iota2((w, w), 1)
    same_head = (rw // hn) == (cw // hn)
    strict = same_head & ((cw % hn) < (rw % hn))
    incl = same_head & ((cw % hn) <= (rw % hn))
    blk16 = (rw // 16) == (cw // 16)
    blk32 = (rw // 32) == (cw // 32)
    eye = (rw == cw).astype(F32)
    r3, c3 = iota2((3 * wt, wt), 0), iota2((3 * wt, wt), 1)
    seg_ones3 = (((r3 % wt) // hn) == (c3 // hn)).astype(BF16)
    rt3, ct3 = iota2((tblock, 3 * tblock), 0), iota2((tblock, 3 * tblock), 1) % tblock
    tri3 = ((ct3 <= rt3) & ((ct3 // c) == (rt3 // c))).astype(BF16)

    def split3(t):
        p1 = t.astype(BF16)
        rest = t - p1.astype(F32)
        p2 = rest.astype(BF16)
        return p1, p2, (rest - p2.astype(F32)).astype(BF16)

    def seg_sum(t):
        return _dot(jnp.concatenate(split3(t), axis=1), seg_ones3)

    def stack(t):
        zero = jnp.zeros_like(t)
        return jnp.concatenate([jnp.where(head0, t, zero), jnp.where(head0, zero, t)], axis=0)

    def fold(t):
        return t[:c] + t[c:]

    def bf(t):
        return t.astype(BF16)

    nch = tblock // c
    chunks = range(nch)
    items = [(ci, p) for ci in chunks for p in range(npair)]

    def rows(t, ci):
        return t[ci * c:(ci + 1) * c]

    def sub(t, it):
        ci, p = it
        return t[ci * c:(ci + 1) * c, p * w:(p + 1) * w]

    r, k, v, z, a, g = (ref[0] for ref in (r_ref, k_ref, v_ref, z_ref, a_ref, g_ref))
    kk = k * kk_ref[...]
    kk = kk / jnp.maximum(jnp.sqrt(seg_sum(kk * kk)), 1e-12)
    kmod = k * (1.0 + (a - 1.0) * ka_ref[...])
    bvec = kk * a
    lw = (-math.exp(-0.5)) / (1.0 + jnp.exp(-z))
    cum = _dot(tri3, jnp.concatenate(split3(lw), axis=0))
    at = -kk * jnp.exp(cum - lw)
    rt = r * jnp.exp(cum)
    inv = jnp.exp(-cum)
    bt = bvec * inv
    kt = kmod * inv
    totals = [cum[(ci + 1) * c - 1:(ci + 1) * c, :] for ci in chunks]
    to_end = jnp.concatenate([jnp.exp(totals[ci] - rows(cum, ci)) for ci in chunks], axis=0)
    bh = bvec * to_end
    kh = kmod * to_end

    n_it = range(len(items))
    ast = [bf(stack(sub(at, it))) for it in items]
    rst = [bf(stack(sub(rt, it))) for it in items]
    bst = [bf(stack(sub(bt, it))) for it in items]
    kst = [bf(stack(sub(kt, it))) for it in items]
    l_full = [jnp.where(strict, _dot_nt(ast[i], bst[i]), 0.0) for i in n_it]
    a_ak = [bf(fold(jnp.where(strict, _dot_nt(ast[i], kst[i]), 0.0))) for i in n_it]
    a_rb = [bf(fold(jnp.where(incl, _dot_nt(rst[i], bst[i]), 0.0))) for i in n_it]
    a_rk = [bf(fold(jnp.where(incl, _dot_nt(rst[i], kst[i]), 0.0))) for i in n_it]
    vst = [bf(stack(sub(v, it))) for it in items]
    x_loc = [_dot(a_ak[i], vst[i]) for i in n_it]
    y_loc = [_dot(a_rk[i], vst[i]) for i in n_it]

    xp = [bf(jnp.where(blk16, l_full[i], 0.0)) for i in n_it]
    tm = [eye + jnp.where(blk16, l_full[i], 0.0) for i in n_it]
    for _ in range(3):
        xp = [bf(_dot(xp[i], xp[i])) for i in n_it]
        tm = [tm[i] + _dot(bf(tm[i]), xp[i]) for i in n_it]
    for inside, outside in ((blk32, blk16), (same_head, blk32)):
        off = [bf(jnp.where(inside & jnp.logical_not(outside), l_full[i], 0.0)) for i in n_it]
        tmb = [bf(tm[i]) for i in n_it]
        half = [bf(_dot(tmb[i], off[i])) for i in n_it]
        tm = [tm[i] + _dot(half[i], tmb[i]) for i in n_it]
    t_fold = [bf(fold(tm[i])) for i in n_it]

    au = [_dot(t_fold[i], jnp.concatenate([ast[i], bf(stack(x_loc[i]))], axis=1)) for i in n_it]
    ah = [au[i][:, :w] for i in n_it]
    ul = [au[i][:, w:] for i in n_it]
    ry = [_dot(a_rb[i], jnp.concatenate([bf(stack(ah[i])), bf(stack(ul[i]))], axis=1)) for i in n_it]
    rh = [bf(sub(rt, items[i]) + ry[i][:, :w]) for i in n_it]
    yl = [y_loc[i] + ry[i][:, w:] for i in n_it]
    w_mat = [bf(jnp.where(same_head, _dot_tn(bf(ah[i]), bf(sub(bh, items[i]))), 0.0)) for i in n_it]
    g_mat = [jnp.where(same_head,
                       _dot_tn(bf(jnp.concatenate([ul[i], sub(v, items[i])], axis=0)),
                               bf(jnp.concatenate([sub(bh, items[i]), sub(kh, items[i])], axis=0))), 0.0)
             for i in n_it]
    decay_c = [jnp.exp(totals[ci][:, p * w:(p + 1) * w]) for ci, p in items]

    s = [s_ref[p] for p in range(npair)]
    ys = [[None] * npair for _ in chunks]
    for i, (ci, p) in enumerate(items):
        sb = bf(s[p])
        ys[ci][p] = _dot_nt(rh[i], sb) + yl[i]
        s[p] = s[p] * decay_c[i] + _dot(sb, w_mat[i]) + g_mat[i]
    for p in range(npair):
        s_ref[p] = s[p]

    y = jnp.concatenate([jnp.concatenate(ys[ci], axis=1) if npair > 1 else ys[ci][0] for ci in chunks], axis=0)
    mean = seg_sum(y) * (1.0 / hn)
    yc = y - mean
    var = seg_sum(yc * yc) * (1.0 / hn)
    yn = yc * lax.rsqrt(var + RW_GN_EPS) * lg_ref[...] + lb_ref[...]
    yn = yn + seg_sum(r * kmod * rk_ref[...]) * v
    o_ref[0] = (yn * g).astype(o_ref.dtype)


def _wkv(r, k, v, z, a, g, k_k, k_a, r_k, lnx_g, lnx_b, batch, seq):
    n, d = r.shape
    w = 2 * RW_HEAD
    npair = WKV_PAIRS
    wt = npair * w
    tb = WKV_TBLOCK
    seq_spec = pl.BlockSpec((1, tb, wt), lambda b, h, t: (b, t, h))
    par_spec = pl.BlockSpec((1, wt), lambda b, h, t: (0, h))
    r3 = lambda t: t.reshape(batch, seq, d)
    out = pl.pallas_call(
        functools.partial(_wkv_kernel, tblock=tb, npair=npair),
        grid=(batch, d // wt, seq // tb),
        in_specs=[seq_spec] * 6 + [par_spec] * 5,
        out_specs=seq_spec,
        out_shape=jax.ShapeDtypeStruct((batch, seq, d), BF16),
        scratch_shapes=[pltpu.VMEM((npair, w, w), F32)],
        compiler_params=_params("parallel", "parallel", "arbitrary"),
        name="wkv7_chunked",
    )(r3(r), r3(k), r3(v), r3(z), r3(a), r3(g), k_k.reshape(1, d), k_a.reshape(1, d), r_k.reshape(1, d),
      lnx_g.reshape(1, d), lnx_b.reshape(1, d))
    return out.reshape(n, d)


def kernel(x, ln1_g, ln1_b, ln2_g, ln2_b, attn_w_qkv, attn_w_o, attn_lambda, attn_subln_g, rw_mu, rw_w_rkv, rw_w_o, rw_w0, rw_w1, rw_w2, rw_a0, rw_a1, rw_a2, rw_g1, rw_g2, rw_k_k, rw_k_a, rw_r_k, rw_lnx_g, rw_lnx_b, rw_v0, rw_v1, rw_v2, moe_rg_w, moe_rg_b, moe_re_w, moe_re_b, moe_w_gu, moe_w_down):
    batch, seq, d = x.shape
    depth = ln1_g.shape[0]
    n = batch * seq
    alpha = (2 * depth) ** 0.25
    x = x.reshape(n, d)
    v_first = None
    for i in range(depth):
        j = i // N_MIXERS
        if i % N_MIXERS == 0:
            lambda_init = 0.8 - 0.6 * math.exp(-0.3 * i)
            lam = attn_lambda[j]
            lam_full = jnp.exp(jnp.sum(lam[0] * lam[1])) - jnp.exp(jnp.sum(lam[2] * lam[3])) + lambda_init
            lam_row = jnp.full((1, 2 * DA_HEAD_DIM), lam_full, F32)
            q_scale = DA_HEAD_DIM ** -0.5 * math.log2(math.e)
            col_scale = jnp.concatenate([jnp.full((d,), q_scale, F32), jnp.ones((2 * d,), F32)])
            w_qkv = (attn_w_qkv[j] * col_scale[None, :]).astype(BF16)
            qkv = _proj(x, w_qkv, BF16)
            mixed = _diff_attention(qkv, lam_row, attn_subln_g[j], lambda_init, batch, seq)
            w_o = attn_w_o[j]
        else:
            value_mix = None if j == 0 else (rw_v0[j - 1], rw_v1[j - 1], rw_v2[j - 1])
            r, k, v, z, a, g = _rwkv_proj(x, seq, rw_mu[j], rw_w_rkv[j], rw_w0[j], rw_w1[j], rw_w2[j], rw_a0[j],
                                          rw_a1[j], rw_a2[j], rw_g1[j], rw_g2[j], value_mix, v_first)
            if value_mix is None:
                v_first = v
            mixed = _wkv(r, k, v, z, a, g, rw_k_k[j], rw_k_a[j], rw_r_k[j].reshape(d), rw_lnx_g[j], rw_lnx_b[j],
                         batch, seq)
            w_o = rw_w_o[j]
        n_router = MOE_GROUPS + MOE_EXPERTS
        w_router = _pad_cols(jnp.concatenate([moe_rg_w[i], moe_re_w[i]], axis=1), LANES)
        wr_hi = w_router.astype(BF16)
        wr_lo = (w_router - wr_hi.astype(F32)).astype(BF16)
        w_router = jnp.concatenate([wr_hi, wr_hi, wr_lo], axis=0)
        b_router = _pad_cols(jnp.concatenate([moe_rg_b[i], moe_re_b[i]]).reshape(1, n_router), LANES)
        x1, logits = _post_mixer(mixed, x, w_o.astype(BF16), ln1_g[i], ln1_b[i], w_router, b_router, alpha)
        x = _hier_moe_block(x1, logits, moe_w_gu.reshape((-1,) + moe_w_gu.shape[2:]),
                            moe_w_down.reshape((-1,) + moe_w_down.shape[2:]), i, ln2_g[i], ln2_b[i], alpha)
    return x.reshape(batch, seq, d)
```

```python
import functools
import math

import jax
import jax.numpy as jnp
from jax import lax
from jax.experimental import pallas as pl
from jax.experimental.pallas import tpu as pltpu

F32 = jnp.float32
BF16 = jnp.bfloat16

LANES = 128
VMEM_LIMIT = 56 * 1024 * 1024

DA_HEADS = 8
DA_HEAD_DIM = 64
RMS_EPS = 1e-5
RW_HEAD = 64
RW_GN_EPS = 64e-5
MOE_GROUPS = 4
MOE_EPG = 8
MOE_EXPERTS = MOE_GROUPS * MOE_EPG
MOE_TOPK = 2
LN_EPS = 1e-5
N_MIXERS = 2

ROW_TILE = 256
ROUTER_TILE = 512
ATTN_TQ = 512
ATTN_TK = 512
WKV_CHUNK = 64
WKV_TBLOCK = 512
WKV_PAIRS = 4
MOE_BLOCK = 256
DMA_UNROLL = 8
NEG_BIG = -1e30


def _dot(a, b, precision=None):
    return jnp.dot(a, b, preferred_element_type=F32, precision=precision)


def _dot_nt(a, b):
    return lax.dot_general(a, b, (((1,), (1,)), ((), ())), preferred_element_type=F32)


def _dot_tn(a, b):
    return lax.dot_general(a, b, (((0,), (0,)), ((), ())), preferred_element_type=F32)


def _params(*sem):
    return pltpu.CompilerParams(dimension_semantics=sem, vmem_limit_bytes=VMEM_LIMIT)


def _const_spec(shape):
    nd = len(shape)
    return pl.BlockSpec(shape, lambda *_: (0,) * nd)


def _proj_kernel(x_ref, w_ref, o_ref):
    o_ref[...] = _dot(x_ref[...].astype(BF16), w_ref[...]).astype(o_ref.dtype)


def _proj(x, w, out_dtype):
    n, d = x.shape
    m = w.shape[1]
    tm = 512
    return pl.pallas_call(
        _proj_kernel,
        grid=(n // tm,),
        in_specs=[pl.BlockSpec((tm, d), lambda i: (i, 0)), _const_spec((d, m))],
        out_specs=pl.BlockSpec((tm, m), lambda i: (i, 0)),
        out_shape=jax.ShapeDtypeStruct((n, m), out_dtype),
        compiler_params=_params("parallel"),
        name="qkv_proj",
    )(x, w)


def _attn_kernel(lam_ref, g_ref, q_ref, k_ref, v_ref, o_ref, vx_ref, m_ref, acc_ref, sa_ref, sb_ref, *, tq, tk,
                 out_scale):
    qi = pl.program_id(2)
    hd = DA_HEAD_DIM
    hw = 2 * hd

    @pl.when(qi == 0)
    def _():
        vx_ref[:, :hw] = v_ref[0]
        vx_ref[:, hw:] = jnp.ones((vx_ref.shape[0], hw), BF16)

    q = q_ref[0]
    lane = lax.broadcasted_iota(jnp.int32, q.shape, 1)
    zero = jnp.zeros_like(q)
    q_stack = jnp.concatenate([jnp.where(lane < hd, q, zero), jnp.where(lane >= hd, q, zero)], axis=0)
    m_ref[...] = jnp.full(m_ref.shape, NEG_BIG, F32)
    acc_ref[...] = jnp.zeros(acc_ref.shape, F32)
    row0 = qi * tq

    def scores(j, s_ref):
        start = pl.multiple_of(j * tk, tk)
        s_ref[...] = _dot_nt(q_stack, k_ref[0, pl.ds(start, tk), :])

    def consume(j, s_ref, masked):
        start = pl.multiple_of(j * tk, tk)
        vb = vx_ref[pl.ds(start, tk), :]
        if masked:
            row = row0 + lax.broadcasted_iota(jnp.int32, (tq, tk), 0)
            col = start + lax.broadcasted_iota(jnp.int32, (tq, tk), 1)
            keep = col <= row
        for c in range(2):
            s = s_ref[c * tq:(c + 1) * tq, :]
            if masked:
                s = jnp.where(keep, s, NEG_BIG)
            chunks = [s[:, i * LANES:(i + 1) * LANES] for i in range(tk // LANES)]
            m_old = m_ref[c]
            m_new = jnp.maximum(m_old, jnp.max(functools.reduce(jnp.maximum, chunks), axis=-1, keepdims=True))
            alpha = jnp.exp2(m_old - m_new)
            m_ref[c] = m_new
            p = jnp.concatenate([jnp.exp2(ch - m_new).astype(BF16) for ch in chunks], axis=1)
            acc_ref[c] = acc_ref[c] * jnp.concatenate([alpha, alpha], axis=1) + _dot(p, vb)

    n = row0 // tk + 1
    n_pairs = (n - 1) // 2
    scores(0, sa_ref)

    def pair(jj, carry):
        j = 2 * jj
        scores(j + 1, sb_ref)
        consume(j, sa_ref, False)
        scores(j + 2, sa_ref)
        consume(j + 1, sb_ref, False)
        return carry

    lax.fori_loop(0, n_pairs, pair, 0)
    jb = 2 * n_pairs

    @pl.when(n % 2 == 1)
    def _():
        consume(jb, sa_ref, True)

    @pl.when(n % 2 == 0)
    def _():
        scores(jb + 1, sb_ref)
        consume(jb, sa_ref, False)
        consume(jb + 1, sb_ref, True)

    a0, a1 = acc_ref[0], acc_ref[1]
    o = a0[:, :hw] * (1.0 / a0[:, hw:]) - lam_ref[...] * (a1[:, :hw] * (1.0 / a1[:, hw:]))
    ms = jnp.mean(o * o, axis=-1, keepdims=True)
    o = o * lax.rsqrt(ms + RMS_EPS) * g_ref[...] * out_scale
    o_ref[0] = o.astype(o_ref.dtype)


def _diff_attention(qkv, lam_row, subln_g, lambda_init, batch, seq):
    d = DA_HEADS * 2 * DA_HEAD_DIM
    qkv = qkv.reshape(batch, seq, 3 * d)
    tq, tk = ATTN_TQ, ATTN_TK
    assert tq <= tk and tk % tq == 0 and seq % tk == 0
    hw = 2 * DA_HEAD_DIM
    out = pl.pallas_call(
        functools.partial(_attn_kernel, tq=tq, tk=tk, out_scale=1.0 - lambda_init),
        grid=(batch, DA_HEADS, seq // tq),
        in_specs=[
            _const_spec((1, hw)),
            _const_spec((1, hw)),
            pl.BlockSpec((1, tq, hw), lambda b, h, i: (b, i, h)),
            pl.BlockSpec((1, seq, hw), lambda b, h, i: (b, 0, DA_HEADS + h)),
            pl.BlockSpec((1, seq, hw), lambda b, h, i: (b, 0, 2 * DA_HEADS + h)),
        ],
        out_specs=pl.BlockSpec((1, tq, hw), lambda b, h, i: (b, i, h)),
        out_shape=jax.ShapeDtypeStruct((batch, seq, d), BF16),
        scratch_shapes=[pltpu.VMEM((seq, 2 * hw), BF16), pltpu.VMEM((2, tq, hw), F32),
                        pltpu.VMEM((2, tq, 2 * hw), F32), pltpu.VMEM((2 * tq, tk), F32),
                        pltpu.VMEM((2 * tq, tk), F32)],
        compiler_params=_params("parallel", "parallel", "arbitrary"),
        name="diff_attn",
    )(lam_row, subln_g.reshape(1, hw), qkv, qkv, qkv)
    return out.reshape(batch * seq, d)


def _layer_norm_rows(z, g, b):
    mu = jnp.mean(z, axis=-1, keepdims=True)
    zc = z - mu
    var = jnp.mean(zc * zc, axis=-1, keepdims=True)
    return zc * lax.rsqrt(var + LN_EPS) * g + b


def _post_mixer_kernel(o_ref, x_ref, w_ref, lg_ref, lb_ref, wr_ref, br_ref, x1_ref, lo_ref, *, alpha):
    h = _dot(o_ref[...], w_ref[...])
    x1 = _layer_norm_rows(alpha * x_ref[...] + h, lg_ref[...], lb_ref[...])
    x1_ref[...] = x1
    x_hi = x1.astype(BF16)
    x_lo = (x1 - x_hi.astype(F32)).astype(BF16)
    lo_ref[...] = _dot(jnp.concatenate([x_hi, x_lo, x_hi], axis=1), wr_ref[...]) + br_ref[...]


def _post_mixer(o, x, w_o, ln_g, ln_b, w_router, b_router, alpha):
    n, d = x.shape
    tm = ROW_TILE
    row = lambda i: (i, 0)
    return pl.pallas_call(
        functools.partial(_post_mixer_kernel, alpha=alpha),
        grid=(n // tm,),
        in_specs=[
            pl.BlockSpec((tm, d), row), pl.BlockSpec((tm, d), row), _const_spec((d, d)),
            _const_spec((1, d)), _const_spec((1, d)), _const_spec((3 * d, LANES)), _const_spec((1, LANES)),
        ],
        out_specs=[pl.BlockSpec((tm, d), row), pl.BlockSpec((tm, LANES), row)],
        out_shape=[jax.ShapeDtypeStruct((n, d), F32), jax.ShapeDtypeStruct((n, LANES), F32)],
        compiler_params=_params("parallel"),
        name="post_mixer",
    )(o, x, w_o, ln_g.reshape(1, d), ln_b.reshape(1, d), w_router, b_router)


def _router_kernel(lo_ref, idx_ref, gate_ref, cnt_ref, carry_ref):
    step = pl.program_id(0)

    @pl.when(step == 0)
    def _():
        carry_ref[...] = jnp.zeros_like(carry_ref)

    lo = lo_ref[...]
    tm = lo.shape[0]
    lane = lax.broadcasted_iota(jnp.int32, lo.shape, 1)
    big = jnp.int32(LANES)

    def softmax_over(mask):
        mx = jnp.max(jnp.where(mask, lo, NEG_BIG), axis=-1, keepdims=True)
        ex = jnp.where(mask, jnp.exp(lo - mx), 0.0)
        return ex / jnp.sum(ex, axis=-1, keepdims=True)

    def top1(p, mask):
        best = jnp.max(jnp.where(mask, p, -1.0), axis=-1, keepdims=True)
        arg = jnp.min(jnp.where(mask & (p == best), lane, big), axis=-1, keepdims=True)
        return best, arg

    g_mask = lane < MOE_GROUPS
    g_p, g_idx = top1(softmax_over(g_mask), g_mask)
    first = MOE_GROUPS + MOE_EPG * g_idx
    e_mask = (lane >= first) & (lane < first + MOE_EPG)
    e_prob = softmax_over(e_mask)
    p1, i1 = top1(e_prob, e_mask)
    rest = e_mask & (lane != i1)
    p2, i2 = top1(e_prob, rest)
    denom = p1 + p2
    gate1 = g_p * (p1 / denom)
    gate2 = g_p * (p2 / denom)
    e1 = i1 - MOE_GROUPS
    e2 = i2 - MOE_GROUPS

    oh1 = (lane == e1).astype(F32)
    oh2 = (lane == e2).astype(F32)
    both = oh1 + oh2
    r_i = lax.broadcasted_iota(jnp.int32, (tm, tm), 0)
    c_i = lax.broadcasted_iota(jnp.int32, (tm, tm), 1)
    before = (c_i < r_i).astype(BF16)
    seen = _dot(before, both.astype(BF16)) + carry_ref[...]
    pos1 = jnp.sum(seen * oh1, axis=-1, keepdims=True).astype(jnp.int32)
    pos2 = jnp.sum(seen * oh2, axis=-1, keepdims=True).astype(jnp.int32)
    carry_ref[...] += jnp.sum(both, axis=0, keepdims=True)
    cnt_ref[...] = carry_ref[...].astype(jnp.int32)

    zero_i = jnp.zeros(lo.shape, jnp.int32)
    idx_ref[...] = (jnp.where(lane == 0, e1, zero_i) + jnp.where(lane == 1, e2, zero_i)
                    + jnp.where(lane == 2, pos1, zero_i) + jnp.where(lane == 3, pos2, zero_i))
    gate_ref[...] = jnp.where(lane == 0, gate1, 0.0) + jnp.where(lane == 1, gate2, 0.0)


def _router(logits):
    n = logits.shape[0]
    tm = ROUTER_TILE
    row = lambda i: (i, 0)
    return pl.pallas_call(
        _router_kernel,
        grid=(n // tm,),
        in_specs=[pl.BlockSpec((tm, LANES), row)],
        out_specs=[pl.BlockSpec((tm, LANES), row), pl.BlockSpec((tm, LANES), row), _const_spec((1, LANES))],
        out_shape=[jax.ShapeDtypeStruct((n, LANES), jnp.int32), jax.ShapeDtypeStruct((n, LANES), F32),
                   jax.ShapeDtypeStruct((1, LANES), jnp.int32)],
        scratch_shapes=[pltpu.VMEM((1, LANES), F32)],
        compiler_params=_params("arbitrary"),
        name="moe_router",
    )(logits)


def _row_copy(src_ref, src_row, dst_ref, dst_row, sem):
    return pltpu.make_async_copy(src_ref.at[pl.ds(src_row, 1)], dst_ref.at[pl.ds(dst_row, 1)], sem)


def _dispatch_kernel(dest_ref, x_ref, buf_in_ref, buf_ref, sem):
    del buf_in_ref
    tm = x_ref.shape[0]

    def copies(r):
        return [_row_copy(x_ref, r, buf_ref, dest_ref[0, 0, MOE_TOPK * r + s], sem) for s in range(MOE_TOPK)]

    def start(r, carry):
        for cp in copies(r):
            cp.start()
        return carry

    def wait(r, carry):
        for cp in copies(r):
            cp.wait()
        return carry

    lax.fori_loop(0, tm, start, 0, unroll=DMA_UNROLL)
    lax.fori_loop(0, tm, wait, 0, unroll=DMA_UNROLL)


def _dispatch(x1, dest_tiles, n_rows):
    n, d = x1.shape
    tm = ROW_TILE
    zeros = jnp.zeros((n_rows, d), x1.dtype)
    return pl.pallas_call(
        _dispatch_kernel,
        grid=(n // tm,),
        in_specs=[
            pl.BlockSpec((1, 1, MOE_TOPK * tm), lambda i: (i, 0, 0), memory_space=pltpu.SMEM),
            pl.BlockSpec((tm, d), lambda i: (i, 0)),
            pl.BlockSpec(memory_space=pl.ANY),
        ],
        out_specs=pl.BlockSpec(memory_space=pl.ANY),
        out_shape=jax.ShapeDtypeStruct((n_rows, d), x1.dtype),
        scratch_shapes=[pltpu.SemaphoreType.DMA(())],
        input_output_aliases={2: 0},
        compiler_params=_params("arbitrary"),
        name="moe_dispatch",
    )(dest_tiles, x1, zeros)


def _expert_kernel(be_ref, x_ref, wgu_ref, wd_ref, y_ref, wgu_bf_ref, wd_bf_ref, *, hidden):
    i = pl.program_id(0)

    @pl.when((i == 0) | (be_ref[i] != be_ref[jnp.maximum(i - 1, 0)]))
    def _():
        wgu_bf_ref[...] = wgu_ref[0].astype(BF16)
        wd_bf_ref[...] = wd_ref[0].astype(BF16)

    n_parts = 2
    part = x_ref.shape[0] // n_parts
    rows = [pl.ds(p * part, part) for p in range(n_parts)]
    hs = [_dot(x_ref[r, :].astype(BF16), wgu_bf_ref[...]) for r in rows]
    for r, h in zip(rows, hs):
        hg = h[:, :hidden]
        hu = h[:, hidden:]
        act = hg * (1.0 / (1.0 + jnp.exp(-hg))) * hu
        y_ref[r, :] = _dot(act.astype(BF16), wd_bf_ref[...])


def _experts(x_buf, block_e, w_gu, w_down):
    n_rows, d = x_buf.shape
    hidden = w_down.shape[1]
    tm = MOE_BLOCK
    grid_spec = pltpu.PrefetchScalarGridSpec(
        num_scalar_prefetch=1,
        grid=(n_rows // tm,),
        in_specs=[
            pl.BlockSpec((tm, d), lambda i, be: (i, 0)),
            pl.BlockSpec((1, d, 2 * hidden), lambda i, be: (be[i], 0, 0)),
            pl.BlockSpec((1, hidden, d), lambda i, be: (be[i], 0, 0)),
        ],
        out_specs=pl.BlockSpec((tm, d), lambda i, be: (i, 0)),
        scratch_shapes=[pltpu.VMEM((d, 2 * hidden), BF16), pltpu.VMEM((hidden, d), BF16)],
    )
    return pl.pallas_call(
        functools.partial(_expert_kernel, hidden=hidden),
        grid_spec=grid_spec,
        out_shape=jax.ShapeDtypeStruct((n_rows, d), F32),
        compiler_params=_params("arbitrary"),
        name="moe_experts",
    )(block_e, x_buf, w_gu, w_down)


def _combine_kernel(dest_ref, y_ref, gate_ref, x_ref, lg_ref, lb_ref, o_ref, rows_ref, sem, *, alpha):
    tm = x_ref.shape[0]

    def copies(r):
        return [_row_copy(y_ref, dest_ref[0, 0, MOE_TOPK * r + s], rows_ref.at[s], r, sem)
                for s in range(MOE_TOPK)]

    def start(r, carry):
        for cp in copies(r):
            cp.start()
        return carry

    def wait(r, carry):
        for cp in copies(r):
            cp.wait()
        return carry

    lax.fori_loop(0, tm, start, 0, unroll=DMA_UNROLL)
    lax.fori_loop(0, tm, wait, 0, unroll=DMA_UNROLL)
    gate = gate_ref[...]
    f = sum(gate[:, s:s + 1] * rows_ref[s] for s in range(MOE_TOPK))
    o_ref[...] = _layer_norm_rows(alpha * x_ref[...] + f, lg_ref[...], lb_ref[...])


def _combine(y, dest_tiles, gates, x1, ln_g, ln_b, alpha):
    n, d = x1.shape
    tm = ROW_TILE
    row = lambda i: (i, 0)
    return pl.pallas_call(
        functools.partial(_combine_kernel, alpha=alpha),
        grid=(n // tm,),
        in_specs=[
            pl.BlockSpec((1, 1, MOE_TOPK * tm), lambda i: (i, 0, 0), memory_space=pltpu.SMEM),
            pl.BlockSpec(memory_space=pl.ANY),
            pl.BlockSpec((tm, LANES), row), pl.BlockSpec((tm, d), row), _const_spec((1, d)), _const_spec((1, d)),
        ],
        out_specs=pl.BlockSpec((tm, d), row),
        out_shape=jax.ShapeDtypeStruct((n, d), F32),
        scratch_shapes=[pltpu.VMEM((MOE_TOPK, tm, d), F32), pltpu.SemaphoreType.DMA(())],
        compiler_params=_params("arbitrary"),
        name="moe_combine",
    )(dest_tiles, y, gates, x1, ln_g.reshape(1, d), ln_b.reshape(1, d))


def _hier_moe_block(x1, logits, w_gu, w_down, layer, ln_g, ln_b, alpha):
    n_tok, d = x1.shape
    idx, gates, counts = _router(logits)
    counts = counts[0, :MOE_EXPERTS]
    padded = (counts + MOE_BLOCK - 1) // MOE_BLOCK * MOE_BLOCK
    pad_end = jnp.cumsum(padded)
    pad_start = pad_end - padded
    expert = idx[:, :MOE_TOPK]
    pos = idx[:, MOE_TOPK:2 * MOE_TOPK]
    e_iota = jnp.arange(MOE_EXPERTS, dtype=jnp.int32)
    base = jnp.sum(jnp.where(expert[:, :, None] == e_iota, pad_start, 0), axis=-1)
    dest = (base + pos).astype(jnp.int32)
    n_blocks = n_tok * MOE_TOPK // MOE_BLOCK + MOE_EXPERTS
    n_rows = n_blocks * MOE_BLOCK
    block_start = jnp.arange(n_blocks, dtype=jnp.int32) * MOE_BLOCK
    block_e = jnp.minimum(jnp.sum((pad_end[None, :] <= block_start[:, None]).astype(jnp.int32), axis=1),
                          MOE_EXPERTS - 1) + layer * MOE_EXPERTS
    dest_tiles = dest.reshape(n_tok // ROW_TILE, 1, MOE_TOPK * ROW_TILE)
    x_buf = _dispatch(x1, dest_tiles, n_rows)
    y = _experts(x_buf, block_e, w_gu, w_down)
    return _combine(y, dest_tiles, gates, x1, ln_g, ln_b, alpha)


def _rwkv_proj_kernel(*refs, seq, has_mix):
    if has_mix:
        (x_ref, xp_ref, mu_ref, wrkv_ref, w0_ref, w1_ref, w2_ref, a0_ref, a1_ref, a2_ref, g1_ref, g2_ref,
         v0_ref, v1_ref, v2_ref, vf_ref, r_ref, k_ref, v_ref, z_ref, a_ref, g_ref) = refs
    else:
        (x_ref, xp_ref, mu_ref, wrkv_ref, w0_ref, w1_ref, w2_ref, a0_ref, a1_ref, a2_ref, g1_ref, g2_ref,
         r_ref, k_ref, v_ref, z_ref, a_ref, g_ref) = refs
    x = x_ref[...]
    tm = x.shape[0]
    row = lax.broadcasted_iota(jnp.int32, x.shape, 0)
    at_seq_start = (pl.program_id(0) * tm) % seq == 0
    prev_last = jnp.where(at_seq_start, 0.0, xp_ref[7:8, :])
    shifted = jnp.where(row == 0, prev_last, pltpu.roll(x, 1, axis=0))
    xx = shifted - x

    def mixed(n):
        return (x + xx * mu_ref[n:n + 1, :]).astype(BF16)

    def sigmoid(t):
        return 1.0 / (1.0 + jnp.exp(-t))

    r_ref[...] = _dot(mixed(0), wrkv_ref[0])
    k_ref[...] = _dot(mixed(1), wrkv_ref[1])
    xv = mixed(2)
    v = _dot(xv, wrkv_ref[2])
    if has_mix:
        mix = sigmoid(v0_ref[...] + _dot(_dot(xv, v1_ref[...]).astype(BF16), v2_ref[...]))
        v = v + (vf_ref[...] - v) * mix
    v_ref[...] = v
    z_ref[...] = w0_ref[...] + _dot(jnp.tanh(_dot(mixed(3), w1_ref[...])).astype(BF16), w2_ref[...])
    a_ref[...] = sigmoid(a0_ref[...] + _dot(_dot(mixed(4), a1_ref[...]).astype(BF16), a2_ref[...]))
    g_ref[...] = _dot(sigmoid(_dot(mixed(5), g1_ref[...])).astype(BF16), g2_ref[...])


def _pad_cols(w, width):
    return jnp.pad(w, ((0, 0), (0, width - w.shape[1])))


def _pad_rows(w, height):
    return jnp.pad(w, ((0, height - w.shape[0]), (0, 0)))


def _rwkv_proj(x, seq, mu, w_rkv, w0, w1, w2, a0, a1, a2, g1, g2, value_mix, v_first):
    n, d = x.shape
    tm = ROW_TILE
    row = lambda i: (i, 0)
    lora = lambda w_in, w_out, width: (_pad_cols(w_in, width).astype(BF16), _pad_rows(w_out, width).astype(BF16))
    w1p, w2p = lora(w1, w2, LANES)
    a1p, a2p = lora(a1, a2, LANES)
    g1p, g2p = lora(g1, g2, 2 * LANES)
    mu8 = _pad_rows(mu, 8)
    ins = [x, x, mu8, w_rkv.astype(BF16), w0.reshape(1, d), w1p, w2p, a0.reshape(1, d), a1p, a2p, g1p, g2p]
    specs = [
        pl.BlockSpec((tm, d), row),
        pl.BlockSpec((8, d), lambda i: (jnp.maximum(i * (tm // 8) - 1, 0), 0)),
        _const_spec((8, d)), _const_spec((3, d, d)), _const_spec((1, d)),
        _const_spec((d, LANES)), _const_spec((LANES, d)), _const_spec((1, d)),
        _const_spec((d, LANES)), _const_spec((LANES, d)),
        _const_spec((d, 2 * LANES)), _const_spec((2 * LANES, d)),
    ]
    has_mix = value_mix is not None
    if has_mix:
        v0, v1, v2 = value_mix
        v1p, v2p = lora(v1, v2, LANES)
        ins += [v0.reshape(1, d), v1p, v2p, v_first]
        specs += [_const_spec((1, d)), _const_spec((d, LANES)), _const_spec((LANES, d)), pl.BlockSpec((tm, d), row)]
    out = jax.ShapeDtypeStruct((n, d), F32)
    return pl.pallas_call(
        functools.partial(_rwkv_proj_kernel, seq=seq, has_mix=has_mix),
        grid=(n // tm,),
        in_specs=specs,
        out_specs=[pl.BlockSpec((tm, d), row)] * 6,
        out_shape=[out] * 6,
        compiler_params=_params("parallel"),
        name="rwkv_proj",
    )(*ins)


def _wkv_kernel(r_ref, k_ref, v_ref, z_ref, a_ref, g_ref, kk_ref, ka_ref, rk_ref, lg_ref, lb_ref, o_ref, s_ref,
                *, tblock, npair):
    c = WKV_CHUNK
    hn = RW_HEAD
    w = 2 * hn
    wt = npair * w

    @pl.when(pl.program_id(2) == 0)
    def _():
        s_ref[...] = jnp.zeros_like(s_ref)

    def iota2(shape, dim):
        return lax.broadcasted_iota(jnp.int32, shape, dim)

    lane_c = iota2((c, w), 1)
    head0 = lane_c < hn
    rw, cw = iota2((w, w), 0), iota2((w, w), 1)
    same_head = (rw // hn) == (cw // hn)
    t_row, s_col = iota2((c, w), 0), lane_c % hn
    strict = s_col < t_row
    incl = s_col <= t_row
    blk16 = (t_row // 16) == (s_col // 16)
    blk32 = (t_row // 32) == (s_col // 32)
    eye = (t_row == s_col).astype(F32)
    r2, c2 = iota2((2 * w, w), 0), iota2((2 * w, w), 1)
    seg_ones2 = (((r2 % w) // hn) == (c2 // hn)).astype(BF16)
    tri3 = (iota2((c, 3 * c), 1) % c <= iota2((c, 3 * c), 0)).astype(BF16)

    def split(t, pieces):
        out = []
        for _ in range(pieces - 1):
            out.append(t.astype(BF16))
            t = t - out[-1].astype(F32)
        return out + [t.astype(BF16)]

    def seg_sum(t):
        return jnp.concatenate(
            [_dot(jnp.concatenate(split(t[:, p * w:(p + 1) * w], 2), axis=1), seg_ones2) for p in range(npair)],
            axis=1)

    def stack(t):
        zero = jnp.zeros_like(t)
        return jnp.concatenate([jnp.where(head0, t, zero), jnp.where(head0, zero, t)], axis=0)

    def bf(t):
        return t.astype(BF16)

    nch = tblock // c
    chunks = range(nch)
    items = [(ci, p) for ci in chunks for p in range(npair)]

    def rows(t, ci):
        return t[ci * c:(ci + 1) * c]

    def sub(t, it):
        ci, p = it
        return t[ci * c:(ci + 1) * c, p * w:(p + 1) * w]

    r, k, v, z, a, g = (ref[0] for ref in (r_ref, k_ref, v_ref, z_ref, a_ref, g_ref))
    kk = k * kk_ref[...]
    kk = kk / jnp.maximum(jnp.sqrt(seg_sum(kk * kk)), 1e-12)
    kmod = k * (1.0 + (a - 1.0) * ka_ref[...])
    bvec = kk * a
    lw = (-math.exp(-0.5)) / (1.0 + jnp.exp(-z))
    cum = jnp.concatenate([_dot(tri3, jnp.concatenate(split(rows(lw, ci), 3), axis=0)) for ci in chunks], axis=0)
    at = -kk * jnp.exp(cum - lw)
    rt = r * jnp.exp(cum)
    inv = jnp.exp(-cum)
    bt = bvec * inv
    kt = kmod * inv
    totals = [cum[(ci + 1) * c - 1:(ci + 1) * c, :] for ci in chunks]
    to_end = jnp.concatenate([jnp.exp(totals[ci] - rows(cum, ci)) for ci in chunks], axis=0)
    bh = bvec * to_end
    kh = kmod * to_end

    n_it = range(len(items))
    atb, rtb = bf(at), bf(rt)
    ast = [stack(sub(atb, it)) for it in items]
    bk = [jnp.concatenate([stack(bf(sub(bt, it))), stack(bf(sub(kt, it)))], axis=0) for it in items]
    prods = [_dot_nt(jnp.concatenate([sub(atb, items[i]), sub(rtb, items[i])], axis=0), bk[i]) for i in n_it]
    l_full = [jnp.where(strict, prods[i][:c, :w], 0.0) for i in n_it]
    a_ak = [bf(jnp.where(strict, prods[i][:c, w:], 0.0)) for i in n_it]
    a_rb = [bf(jnp.where(incl, prods[i][c:, :w], 0.0)) for i in n_it]
    a_rk = [bf(jnp.where(incl, prods[i][c:, w:], 0.0)) for i in n_it]
    vst = [stack(bf(sub(v, it))) for it in items]
    x_loc = [_dot(a_ak[i], vst[i]) for i in n_it]
    y_loc = [_dot(a_rk[i], vst[i]) for i in n_it]

    xf = [bf(jnp.where(blk16, l_full[i], 0.0)) for i in n_it]
    xs = [stack(xf[i]) for i in n_it]
    tm = [eye + jnp.where(blk16, l_full[i], 0.0) for i in n_it]
    for _ in range(3):
        xf = [bf(_dot(xf[i], xs[i])) for i in n_it]
        xs = [stack(xf[i]) for i in n_it]
        tm = [tm[i] + _dot(bf(tm[i]), xs[i]) for i in n_it]
    for inside, outside in ((blk32, blk16), (None, blk32)):
        keep = jnp.logical_not(outside) if inside is None else inside & jnp.logical_not(outside)
        off = [stack(bf(jnp.where(keep, l_full[i], 0.0))) for i in n_it]
        tmb = [bf(tm[i]) for i in n_it]
        half = [bf(_dot(tmb[i], off[i])) for i in n_it]
        tm = [tm[i] + _dot(half[i], stack(tmb[i])) for i in n_it]
    t_fold = [bf(tm[i]) for i in n_it]

    au = [_dot(t_fold[i], jnp.concatenate([ast[i], bf(stack(x_loc[i]))], axis=1)) for i in n_it]
    ah = [au[i][:, :w] for i in n_it]
    ul = [au[i][:, w:] for i in n_it]
    ry = [_dot(a_rb[i], jnp.concatenate([bf(stack(ah[i])), bf(stack(ul[i]))], axis=1)) for i in n_it]
    rh = [bf(sub(rt, items[i]) + ry[i][:, :w]) for i in n_it]
    yl = [y_loc[i] + ry[i][:, w:] for i in n_it]
    w_mat = [bf(jnp.where(same_head, _dot_tn(bf(ah[i]), bf(sub(bh, items[i]))), 0.0)) for i in n_it]
    g_mat = [jnp.where(same_head,
                       _dot_tn(bf(jnp.concatenate([ul[i], sub(v, items[i])], axis=0)),
                               bf(jnp.concatenate([sub(bh, items[i]), sub(kh, items[i])], axis=0))), 0.0)
             for i in n_it]
    decay_c = [jnp.exp(totals[ci][:, p * w:(p + 1) * w]) for ci, p in items]

    s = [s_ref[p] for p in range(npair)]
    ys = [[None] * npair for _ in chunks]
    for i, (ci, p) in enumerate(items):
        sb = bf(s[p])
        ys[ci][p] = _dot_nt(rh[i], sb) + yl[i]
        s[p] = s[p] * decay_c[i] + _dot(sb, w_mat[i]) + g_mat[i]
    for p in range(npair):
        s_ref[p] = s[p]

    y = jnp.concatenate([jnp.concatenate(ys[ci], axis=1) if npair > 1 else ys[ci][0] for ci in chunks], axis=0)
    mean = seg_sum(y) * (1.0 / hn)
    yc = y - mean
    var = seg_sum(yc * yc) * (1.0 / hn)
    yn = yc * lax.rsqrt(var + RW_GN_EPS) * lg_ref[...] + lb_ref[...]
    yn = yn + seg_sum(r * kmod * rk_ref[...]) * v
    o_ref[0] = (yn * g).astype(o_ref.dtype)


def _wkv(r, k, v, z, a, g, k_k, k_a, r_k, lnx_g, lnx_b, batch, seq):
    n, d = r.shape
    w = 2 * RW_HEAD
    npair = WKV_PAIRS
    wt = npair * w
    tb = WKV_TBLOCK
    seq_spec = pl.BlockSpec((1, tb, wt), lambda b, h, t: (b, t, h))
    par_spec = pl.BlockSpec((1, wt), lambda b, h, t: (0, h))
    r3 = lambda t: t.reshape(batch, seq, d)
    out = pl.pallas_call(
        functools.partial(_wkv_kernel, tblock=tb, npair=npair),
        grid=(batch, d // wt, seq // tb),
        in_specs=[seq_spec] * 6 + [par_spec] * 5,
        out_specs=seq_spec,
        out_shape=jax.ShapeDtypeStruct((batch, seq, d), BF16),
        scratch_shapes=[pltpu.VMEM((npair, w, w), F32)],
        compiler_params=_params("parallel", "parallel", "arbitrary"),
        name="wkv7_chunked",
    )(r3(r), r3(k), r3(v), r3(z), r3(a), r3(g), k_k.reshape(1, d), k_a.reshape(1, d), r_k.reshape(1, d),
      lnx_g.reshape(1, d), lnx_b.reshape(1, d))
    return out.reshape(n, d)


def kernel(x, ln1_g, ln1_b, ln2_g, ln2_b, attn_w_qkv, attn_w_o, attn_lambda, attn_subln_g, rw_mu, rw_w_rkv, rw_w_o, rw_w0, rw_w1, rw_w2, rw_a0, rw_a1, rw_a2, rw_g1, rw_g2, rw_k_k, rw_k_a, rw_r_k, rw_lnx_g, rw_lnx_b, rw_v0, rw_v1, rw_v2, moe_rg_w, moe_rg_b, moe_re_w, moe_re_b, moe_w_gu, moe_w_down):
    batch, seq, d = x.shape
    depth = ln1_g.shape[0]
    n = batch * seq
    alpha = (2 * depth) ** 0.25
    x = x.reshape(n, d)
    v_first = None
    for i in range(depth):
        j = i // N_MIXERS
        if i % N_MIXERS == 0:
            lambda_init = 0.8 - 0.6 * math.exp(-0.3 * i)
            lam = attn_lambda[j]
            lam_full = jnp.exp(jnp.sum(lam[0] * lam[1])) - jnp.exp(jnp.sum(lam[2] * lam[3])) + lambda_init
            lam_row = jnp.full((1, 2 * DA_HEAD_DIM), lam_full, F32)
            q_scale = DA_HEAD_DIM ** -0.5 * math.log2(math.e)
            col_scale = jnp.concatenate([jnp.full((d,), q_scale, F32), jnp.ones((2 * d,), F32)])
            w_qkv = (attn_w_qkv[j] * col_scale[None, :]).astype(BF16)
            qkv = _proj(x, w_qkv, BF16)
            mixed = _diff_attention(qkv, lam_row, attn_subln_g[j], lambda_init, batch, seq)
            w_o = attn_w_o[j]
        else:
            value_mix = None if j == 0 else (rw_v0[j - 1], rw_v1[j - 1], rw_v2[j - 1])
            r, k, v, z, a, g = _rwkv_proj(x, seq, rw_mu[j], rw_w_rkv[j], rw_w0[j], rw_w1[j], rw_w2[j], rw_a0[j],
                                          rw_a1[j], rw_a2[j], rw_g1[j], rw_g2[j], value_mix, v_first)
            if value_mix is None:
                v_first = v
            mixed = _wkv(r, k, v, z, a, g, rw_k_k[j], rw_k_a[j], rw_r_k[j].reshape(d), rw_lnx_g[j], rw_lnx_b[j],
                         batch, seq)
            w_o = rw_w_o[j]
        n_router = MOE_GROUPS + MOE_EXPERTS
        w_router = _pad_cols(jnp.concatenate([moe_rg_w[i], moe_re_w[i]], axis=1), LANES)
        wr_hi = w_router.astype(BF16)
        wr_lo = (w_router - wr_hi.astype(F32)).astype(BF16)
        w_router = jnp.concatenate([wr_hi, wr_hi, wr_lo], axis=0)
        b_router = _pad_cols(jnp.concatenate([moe_rg_b[i], moe_re_b[i]]).reshape(1, n_router), LANES)
        x1, logits = _post_mixer(mixed, x, w_o.astype(BF16), ln1_g[i], ln1_b[i], w_router, b_router, alpha)
        x = _hier_moe_block(x1, logits, moe_w_gu.reshape((-1,) + moe_w_gu.shape[2:]),
                            moe_w_down.reshape((-1,) + moe_w_down.shape[2:]), i, ln2_g[i], ln2_b[i], alpha)
    return x.reshape(batch, seq, d)
```

```python
import functools
import math

import jax
import jax.numpy as jnp
from jax import lax
from jax.experimental import pallas as pl
from jax.experimental.pallas import tpu as pltpu

F32 = jnp.float32
BF16 = jnp.bfloat16

LANES = 128
VMEM_LIMIT = 56 * 1024 * 1024

DA_HEADS = 8
DA_HEAD_DIM = 64
RMS_EPS = 1e-5
RW_HEAD = 64
RW_GN_EPS = 64e-5
MOE_GROUPS = 4
MOE_EPG = 8
MOE_EXPERTS = MOE_GROUPS * MOE_EPG
MOE_TOPK = 2
LN_EPS = 1e-5
N_MIXERS = 2

ROW_TILE = 256
ROUTER_TILE = 512
MOE_TOKEN_TILE = 512
ATTN_TQ = 512
ATTN_TK = 512
WKV_CHUNK = 64
WKV_TBLOCK = 512
WKV_PAIRS = 4
MOE_BLOCK = 256
DMA_UNROLL = 8
NEG_BIG = -1e30


def _dot(a, b, precision=None):
    return jnp.dot(a, b, preferred_element_type=F32, precision=precision)


def _dot_nt(a, b):
    return lax.dot_general(a, b, (((1,), (1,)), ((), ())), preferred_element_type=F32)


def _dot_tn(a, b):
    return lax.dot_general(a, b, (((0,), (0,)), ((), ())), preferred_element_type=F32)


def _params(*sem):
    return pltpu.CompilerParams(dimension_semantics=sem, vmem_limit_bytes=VMEM_LIMIT)


def _const_spec(shape):
    nd = len(shape)
    return pl.BlockSpec(shape, lambda *_: (0,) * nd)


def _proj_kernel(x_ref, w_ref, o_ref):
    o_ref[...] = _dot(x_ref[...].astype(BF16), w_ref[...]).astype(o_ref.dtype)


def _proj(x, w, out_dtype):
    n, d = x.shape
    m = w.shape[1]
    tm = 512
    return pl.pallas_call(
        _proj_kernel,
        grid=(n // tm,),
        in_specs=[pl.BlockSpec((tm, d), lambda i: (i, 0)), _const_spec((d, m))],
        out_specs=pl.BlockSpec((tm, m), lambda i: (i, 0)),
        out_shape=jax.ShapeDtypeStruct((n, m), out_dtype),
        compiler_params=_params("parallel"),
        name="qkv_proj",
    )(x, w)


def _attn_kernel(lam_ref, g_ref, q_ref, k_ref, v_ref, o_ref, vx_ref, m_ref, acc_ref, sa_ref, sb_ref, *, tq, tk,
                 out_scale):
    qi = pl.program_id(2)
    hd = DA_HEAD_DIM
    hw = 2 * hd

    @pl.when(qi == 0)
    def _():
        vx_ref[:, :hw] = v_ref[0]
        vx_ref[:, hw:] = jnp.ones((vx_ref.shape[0], hw), BF16)

    q = q_ref[0]
    lane = lax.broadcasted_iota(jnp.int32, q.shape, 1)
    zero = jnp.zeros_like(q)
    q_stack = jnp.concatenate([jnp.where(lane < hd, q, zero), jnp.where(lane >= hd, q, zero)], axis=0)
    m_ref[...] = jnp.full(m_ref.shape, NEG_BIG, F32)
    acc_ref[...] = jnp.zeros(acc_ref.shape, F32)
    row0 = qi * tq

    def scores(j, s_ref):
        start = pl.multiple_of(j * tk, tk)
        s_ref[...] = _dot_nt(q_stack, k_ref[0, pl.ds(start, tk), :])

    def consume(j, s_ref, masked):
        start = pl.multiple_of(j * tk, tk)
        vb = vx_ref[pl.ds(start, tk), :]
        if masked:
            row = row0 + lax.broadcasted_iota(jnp.int32, (tq, tk), 0)
            col = start + lax.broadcasted_iota(jnp.int32, (tq, tk), 1)
            keep = col <= row
        for c in range(2):
            s = s_ref[c * tq:(c + 1) * tq, :]
            if masked:
                s = jnp.where(keep, s, NEG_BIG)
            chunks = [s[:, i * LANES:(i + 1) * LANES] for i in range(tk // LANES)]
            m_old = m_ref[c]
            m_new = jnp.maximum(m_old, jnp.max(functools.reduce(jnp.maximum, chunks), axis=-1, keepdims=True))
            alpha = jnp.exp2(m_old - m_new)
            m_ref[c] = m_new
            p = jnp.concatenate([jnp.exp2(ch - m_new).astype(BF16) for ch in chunks], axis=1)
            acc_ref[c] = acc_ref[c] * jnp.concatenate([alpha, alpha], axis=1) + _dot(p, vb)

    n = row0 // tk + 1
    n_pairs = (n - 1) // 2
    scores(0, sa_ref)

    def pair(jj, carry):
        j = 2 * jj
        scores(j + 1, sb_ref)
        consume(j, sa_ref, False)
        scores(j + 2, sa_ref)
        consume(j + 1, sb_ref, False)
        return carry

    lax.fori_loop(0, n_pairs, pair, 0)
    jb = 2 * n_pairs

    @pl.when(n % 2 == 1)
    def _():
        consume(jb, sa_ref, True)

    @pl.when(n % 2 == 0)
    def _():
        scores(jb + 1, sb_ref)
        consume(jb, sa_ref, False)
        consume(jb + 1, sb_ref, True)

    a0, a1 = acc_ref[0], acc_ref[1]
    o = a0[:, :hw] * (1.0 / a0[:, hw:]) - lam_ref[...] * (a1[:, :hw] * (1.0 / a1[:, hw:]))
    ms = jnp.mean(o * o, axis=-1, keepdims=True)
    o = o * lax.rsqrt(ms + RMS_EPS) * g_ref[...] * out_scale
    o_ref[0] = o.astype(o_ref.dtype)


def _diff_attention(qkv, lam_row, subln_g, lambda_init, batch, seq):
    d = DA_HEADS * 2 * DA_HEAD_DIM
    qkv = qkv.reshape(batch, seq, 3 * d)
    tq, tk = ATTN_TQ, ATTN_TK
    assert tq <= tk and tk % tq == 0 and seq % tk == 0
    hw = 2 * DA_HEAD_DIM
    out = pl.pallas_call(
        functools.partial(_attn_kernel, tq=tq, tk=tk, out_scale=1.0 - lambda_init),
        grid=(batch, DA_HEADS, seq // tq),
        in_specs=[
            _const_spec((1, hw)),
            _const_spec((1, hw)),
            pl.BlockSpec((1, tq, hw), lambda b, h, i: (b, i, h)),
            pl.BlockSpec((1, seq, hw), lambda b, h, i: (b, 0, DA_HEADS + h)),
            pl.BlockSpec((1, seq, hw), lambda b, h, i: (b, 0, 2 * DA_HEADS + h)),
        ],
        out_specs=pl.BlockSpec((1, tq, hw), lambda b, h, i: (b, i, h)),
        out_shape=jax.ShapeDtypeStruct((batch, seq, d), BF16),
        scratch_shapes=[pltpu.VMEM((seq, 2 * hw), BF16), pltpu.VMEM((2, tq, hw), F32),
                        pltpu.VMEM((2, tq, 2 * hw), F32), pltpu.VMEM((2 * tq, tk), F32),
                        pltpu.VMEM((2 * tq, tk), F32)],
        compiler_params=_params("parallel", "parallel", "arbitrary"),
        name="diff_attn",
    )(lam_row, subln_g.reshape(1, hw), qkv, qkv, qkv)
    return out.reshape(batch * seq, d)


def _layer_norm_rows(z, g, b):
    mu = jnp.mean(z, axis=-1, keepdims=True)
    zc = z - mu
    var = jnp.mean(zc * zc, axis=-1, keepdims=True)
    return zc * lax.rsqrt(var + LN_EPS) * g + b


HIGH16 = 0xFFFF0000


def _pack_bf16_halves(x_hi):
    half = x_hi.shape[1] // 2
    bits = lax.bitcast_convert_type(x_hi, jnp.uint32)
    return (bits[:, :half] >> 16) | (bits[:, half:] & jnp.uint32(HIGH16))


def _unpack_bf16_halves(packed):
    lo = lax.bitcast_convert_type(packed << 16, F32)
    hi = lax.bitcast_convert_type(packed & jnp.uint32(HIGH16), F32)
    return jnp.concatenate([lo, hi], axis=1).astype(BF16)


def _post_mixer_kernel(o_ref, x_ref, w_ref, lg_ref, lb_ref, wr_ref, br_ref, x1_ref, x1p_ref, lo_ref, *, alpha):
    h = _dot(o_ref[...], w_ref[...])
    x1 = _layer_norm_rows(alpha * x_ref[...] + h, lg_ref[...], lb_ref[...])
    x1_ref[...] = x1
    x_hi = x1.astype(BF16)
    x_hi32 = x_hi.astype(F32)
    x_lo = (x1 - x_hi32).astype(BF16)
    x1p_ref[...] = _pack_bf16_halves(x_hi32)
    lo_ref[...] = _dot(jnp.concatenate([x_hi, x_lo, x_hi], axis=1), wr_ref[...]) + br_ref[...]


def _post_mixer(o, x, w_o, ln_g, ln_b, w_router, b_router, alpha):
    n, d = x.shape
    tm = ROW_TILE
    row = lambda i: (i, 0)
    return pl.pallas_call(
        functools.partial(_post_mixer_kernel, alpha=alpha),
        grid=(n // tm,),
        in_specs=[
            pl.BlockSpec((tm, d), row), pl.BlockSpec((tm, d), row), _const_spec((d, d)),
            _const_spec((1, d)), _const_spec((1, d)), _const_spec((3 * d, LANES)), _const_spec((1, LANES)),
        ],
        out_specs=[pl.BlockSpec((tm, d), row), pl.BlockSpec((tm, d // 2), row), pl.BlockSpec((tm, LANES), row)],
        out_shape=[jax.ShapeDtypeStruct((n, d), F32), jax.ShapeDtypeStruct((n, d // 2), jnp.uint32),
                   jax.ShapeDtypeStruct((n, LANES), F32)],
        compiler_params=_params("parallel"),
        name="post_mixer",
    )(o, x, w_o, ln_g.reshape(1, d), ln_b.reshape(1, d), w_router, b_router)


def _router_kernel(lo_ref, idx_ref, gate_ref, cnt_ref, carry_ref):
    step = pl.program_id(0)

    @pl.when(step == 0)
    def _():
        carry_ref[...] = jnp.zeros_like(carry_ref)

    lo = lo_ref[...]
    tm = lo.shape[0]
    lane = lax.broadcasted_iota(jnp.int32, lo.shape, 1)
    big = jnp.int32(LANES)

    def softmax_over(mask):
        mx = jnp.max(jnp.where(mask, lo, NEG_BIG), axis=-1, keepdims=True)
        ex = jnp.where(mask, jnp.exp(lo - mx), 0.0)
        return ex / jnp.sum(ex, axis=-1, keepdims=True)

    def top1(p, mask):
        best = jnp.max(jnp.where(mask, p, -1.0), axis=-1, keepdims=True)
        arg = jnp.min(jnp.where(mask & (p == best), lane, big), axis=-1, keepdims=True)
        return best, arg

    g_mask = lane < MOE_GROUPS
    g_p, g_idx = top1(softmax_over(g_mask), g_mask)
    first = MOE_GROUPS + MOE_EPG * g_idx
    e_mask = (lane >= first) & (lane < first + MOE_EPG)
    e_prob = softmax_over(e_mask)
    p1, i1 = top1(e_prob, e_mask)
    rest = e_mask & (lane != i1)
    p2, i2 = top1(e_prob, rest)
    denom = p1 + p2
    gate1 = g_p * (p1 / denom)
    gate2 = g_p * (p2 / denom)
    e1 = i1 - MOE_GROUPS
    e2 = i2 - MOE_GROUPS

    oh1 = (lane == e1).astype(F32)
    oh2 = (lane == e2).astype(F32)
    both = oh1 + oh2
    r_i = lax.broadcasted_iota(jnp.int32, (tm, tm), 0)
    c_i = lax.broadcasted_iota(jnp.int32, (tm, tm), 1)
    before = (c_i < r_i).astype(BF16)
    seen = _dot(before, both.astype(BF16)) + carry_ref[...]
    pos1 = jnp.sum(seen * oh1, axis=-1, keepdims=True).astype(jnp.int32)
    pos2 = jnp.sum(seen * oh2, axis=-1, keepdims=True).astype(jnp.int32)
    carry_ref[...] += jnp.sum(both, axis=0, keepdims=True)
    cnt_ref[...] = carry_ref[...].astype(jnp.int32)

    zero_i = jnp.zeros(lo.shape, jnp.int32)
    idx_ref[...] = (jnp.where(lane == 0, e1, zero_i) + jnp.where(lane == 1, e2, zero_i)
                    + jnp.where(lane == 2, pos1, zero_i) + jnp.where(lane == 3, pos2, zero_i))
    gate_ref[...] = jnp.where(lane == 0, gate1, 0.0) + jnp.where(lane == 1, gate2, 0.0)


def _router(logits):
    n = logits.shape[0]
    tm = ROUTER_TILE
    row = lambda i: (i, 0)
    return pl.pallas_call(
        _router_kernel,
        grid=(n // tm,),
        in_specs=[pl.BlockSpec((tm, LANES), row)],
        out_specs=[pl.BlockSpec((tm, LANES), row), pl.BlockSpec((tm, LANES), row), _const_spec((1, LANES))],
        out_shape=[jax.ShapeDtypeStruct((n, LANES), jnp.int32), jax.ShapeDtypeStruct((n, LANES), F32),
                   jax.ShapeDtypeStruct((1, LANES), jnp.int32)],
        scratch_shapes=[pltpu.VMEM((1, LANES), F32)],
        compiler_params=_params("arbitrary"),
        name="moe_router",
    )(logits)


def _row_copy(src_ref, src_row, dst_ref, dst_row, sem):
    return pltpu.make_async_copy(src_ref.at[pl.ds(src_row, 1)], dst_ref.at[pl.ds(dst_row, 1)], sem)


def _dispatch_kernel(dest_ref, x_ref, buf_in_ref, buf_ref, sem):
    del buf_in_ref
    tm = x_ref.shape[0]

    def copies(r):
        return [_row_copy(x_ref, r, buf_ref, dest_ref[0, 0, MOE_TOPK * r + s], sem) for s in range(MOE_TOPK)]

    def start(r, carry):
        for cp in copies(r):
            cp.start()
        return carry

    def wait(r, carry):
        for cp in copies(r):
            cp.wait()
        return carry

    lax.fori_loop(0, tm, start, 0, unroll=DMA_UNROLL)
    lax.fori_loop(0, tm, wait, 0, unroll=DMA_UNROLL)


def _dispatch(x1, dest_tiles, n_rows):
    n, d = x1.shape
    tm = MOE_TOKEN_TILE
    zeros = jnp.zeros((n_rows, d), x1.dtype)
    return pl.pallas_call(
        _dispatch_kernel,
        grid=(n // tm,),
        in_specs=[
            pl.BlockSpec((1, 1, MOE_TOPK * tm), lambda i: (i, 0, 0), memory_space=pltpu.SMEM),
            pl.BlockSpec((tm, d), lambda i: (i, 0)),
            pl.BlockSpec(memory_space=pl.ANY),
        ],
        out_specs=pl.BlockSpec(memory_space=pl.ANY),
        out_shape=jax.ShapeDtypeStruct((n_rows, d), x1.dtype),
        scratch_shapes=[pltpu.SemaphoreType.DMA(())],
        input_output_aliases={2: 0},
        compiler_params=_params("arbitrary"),
        name="moe_dispatch",
    )(dest_tiles, x1, zeros)


def _expert_kernel(be_ref, x_ref, wgu_ref, wd_ref, y_ref, wgu_bf_ref, wd_bf_ref, *, hidden):
    i = pl.program_id(0)

    @pl.when((i == 0) | (be_ref[i] != be_ref[jnp.maximum(i - 1, 0)]))
    def _():
        wgu_bf_ref[...] = wgu_ref[0].astype(BF16)
        wd_bf_ref[...] = wd_ref[0].astype(BF16)

    n_parts = 2
    part = x_ref.shape[0] // n_parts
    rows = [pl.ds(p * part, part) for p in range(n_parts)]
    hs = [_dot(_unpack_bf16_halves(x_ref[r, :]), wgu_bf_ref[...]) for r in rows]
    for r, h in zip(rows, hs):
        hg = h[:, :hidden]
        hu = h[:, hidden:]
        act = hg * (1.0 / (1.0 + jnp.exp(-hg))) * hu
        y_ref[r, :] = _dot(act.astype(BF16), wd_bf_ref[...])


def _experts(x_buf, block_e, w_gu, w_down):
    n_rows = x_buf.shape[0]
    hidden, d = w_down.shape[1:]
    tm = MOE_BLOCK
    grid_spec = pltpu.PrefetchScalarGridSpec(
        num_scalar_prefetch=1,
        grid=(n_rows // tm,),
        in_specs=[
            pl.BlockSpec((tm, d // 2), lambda i, be: (i, 0)),
            pl.BlockSpec((1, d, 2 * hidden), lambda i, be: (be[i], 0, 0)),
            pl.BlockSpec((1, hidden, d), lambda i, be: (be[i], 0, 0)),
        ],
        out_specs=pl.BlockSpec((tm, d), lambda i, be: (i, 0)),
        scratch_shapes=[pltpu.VMEM((d, 2 * hidden), BF16), pltpu.VMEM((hidden, d), BF16)],
    )
    return pl.pallas_call(
        functools.partial(_expert_kernel, hidden=hidden),
        grid_spec=grid_spec,
        out_shape=jax.ShapeDtypeStruct((n_rows, d), F32),
        compiler_params=_params("arbitrary"),
        name="moe_experts",
    )(block_e, x_buf, w_gu, w_down)


def _combine_kernel(dest_ref, y_ref, gate_ref, x_ref, lg_ref, lb_ref, o_ref, rows_ref, sem, *, alpha):
    tm = x_ref.shape[0]

    def copies(r):
        return [_row_copy(y_ref, dest_ref[0, 0, MOE_TOPK * r + s], rows_ref.at[s], r, sem)
                for s in range(MOE_TOPK)]

    def start(r, carry):
        for cp in copies(r):
            cp.start()
        return carry

    def wait(r, carry):
        for cp in copies(r):
            cp.wait()
        return carry

    lax.fori_loop(0, tm, start, 0, unroll=DMA_UNROLL)
    lax.fori_loop(0, tm, wait, 0, unroll=DMA_UNROLL)
    gate = gate_ref[...]
    f = sum(gate[:, s:s + 1] * rows_ref[s] for s in range(MOE_TOPK))
    o_ref[...] = _layer_norm_rows(alpha * x_ref[...] + f, lg_ref[...], lb_ref[...])


def _combine(y, dest_tiles, gates, x1, ln_g, ln_b, alpha):
    n, d = x1.shape
    tm = MOE_TOKEN_TILE
    row = lambda i: (i, 0)
    return pl.pallas_call(
        functools.partial(_combine_kernel, alpha=alpha),
        grid=(n // tm,),
        in_specs=[
            pl.BlockSpec((1, 1, MOE_TOPK * tm), lambda i: (i, 0, 0), memory_space=pltpu.SMEM),
            pl.BlockSpec(memory_space=pl.ANY),
            pl.BlockSpec((tm, LANES), row), pl.BlockSpec((tm, d), row), _const_spec((1, d)), _const_spec((1, d)),
        ],
        out_specs=pl.BlockSpec((tm, d), row),
        out_shape=jax.ShapeDtypeStruct((n, d), F32),
        scratch_shapes=[pltpu.VMEM((MOE_TOPK, tm, d), F32), pltpu.SemaphoreType.DMA(())],
        compiler_params=_params("arbitrary"),
        name="moe_combine",
    )(dest_tiles, y, gates, x1, ln_g.reshape(1, d), ln_b.reshape(1, d))


def _hier_moe_block(x1, x1_packed, logits, w_gu, w_down, layer, ln_g, ln_b, alpha):
    n_tok, d = x1.shape
    idx, gates, counts = _router(logits)
    counts = counts[0, :MOE_EXPERTS]
    padded = (counts + MOE_BLOCK - 1) // MOE_BLOCK * MOE_BLOCK
    pad_end = jnp.cumsum(padded)
    pad_start = pad_end - padded
    expert = idx[:, :MOE_TOPK]
    pos = idx[:, MOE_TOPK:2 * MOE_TOPK]
    e_iota = jnp.arange(MOE_EXPERTS, dtype=jnp.int32)
    base = jnp.sum(jnp.where(expert[:, :, None] == e_iota, pad_start, 0), axis=-1)
    dest = (base + pos).astype(jnp.int32)
    n_blocks = n_tok * MOE_TOPK // MOE_BLOCK + MOE_EXPERTS
    n_rows = n_blocks * MOE_BLOCK
    block_start = jnp.arange(n_blocks, dtype=jnp.int32) * MOE_BLOCK
    block_e = jnp.minimum(jnp.sum((pad_end[None, :] <= block_start[:, None]).astype(jnp.int32), axis=1),
                          MOE_EXPERTS - 1) + layer * MOE_EXPERTS
    dest_tiles = dest.reshape(n_tok // MOE_TOKEN_TILE, 1, MOE_TOPK * MOE_TOKEN_TILE)
    x_buf = _dispatch(x1_packed, dest_tiles, n_rows)
    y = _experts(x_buf, block_e, w_gu, w_down)
    return _combine(y, dest_tiles, gates, x1, ln_g, ln_b, alpha)


def _rwkv_proj_kernel(*refs, seq, has_mix):
    if has_mix:
        (x_ref, xp_ref, mu_ref, wrkv_ref, w0_ref, w1_ref, w2_ref, a0_ref, a1_ref, a2_ref, g1_ref, g2_ref,
         v0_ref, v1_ref, v2_ref, vf_ref, r_ref, k_ref, v_ref, z_ref, a_ref, g_ref) = refs
    else:
        (x_ref, xp_ref, mu_ref, wrkv_ref, w0_ref, w1_ref, w2_ref, a0_ref, a1_ref, a2_ref, g1_ref, g2_ref,
         r_ref, k_ref, v_ref, z_ref, a_ref, g_ref) = refs
    x = x_ref[...]
    tm = x.shape[0]
    row = lax.broadcasted_iota(jnp.int32, x.shape, 0)
    at_seq_start = (pl.program_id(0) * tm) % seq == 0
    prev_last = jnp.where(at_seq_start, 0.0, xp_ref[7:8, :])
    shifted = jnp.where(row == 0, prev_last, pltpu.roll(x, 1, axis=0))
    xx = shifted - x

    def mixed(n):
        return (x + xx * mu_ref[n:n + 1, :]).astype(BF16)

    def sigmoid(t):
        return 1.0 / (1.0 + jnp.exp(-t))

    r_ref[...] = _dot(mixed(0), wrkv_ref[0])
    k_ref[...] = _dot(mixed(1), wrkv_ref[1])
    xv = mixed(2)
    v = _dot(xv, wrkv_ref[2])
    if has_mix:
        mix = sigmoid(v0_ref[...] + _dot(_dot(xv, v1_ref[...]).astype(BF16), v2_ref[...]))
        v = v + (vf_ref[...] - v) * mix
    v_ref[...] = v
    z_ref[...] = w0_ref[...] + _dot(jnp.tanh(_dot(mixed(3), w1_ref[...])).astype(BF16), w2_ref[...])
    a_ref[...] = sigmoid(a0_ref[...] + _dot(_dot(mixed(4), a1_ref[...]).astype(BF16), a2_ref[...]))
    g_ref[...] = _dot(sigmoid(_dot(mixed(5), g1_ref[...])).astype(BF16), g2_ref[...])


def _pad_cols(w, width):
    return jnp.pad(w, ((0, 0), (0, width - w.shape[1])))


def _pad_rows(w, height):
    return jnp.pad(w, ((0, height - w.shape[0]), (0, 0)))


def _rwkv_proj(x, seq, mu, w_rkv, w0, w1, w2, a0, a1, a2, g1, g2, value_mix, v_first):
    n, d = x.shape
    tm = ROW_TILE
    row = lambda i: (i, 0)
    lora = lambda w_in, w_out, width: (_pad_cols(w_in, width).astype(BF16), _pad_rows(w_out, width).astype(BF16))
    w1p, w2p = lora(w1, w2, LANES)
    a1p, a2p = lora(a1, a2, LANES)
    g1p, g2p = lora(g1, g2, 2 * LANES)
    mu8 = _pad_rows(mu, 8)
    ins = [x, x, mu8, w_rkv.astype(BF16), w0.reshape(1, d), w1p, w2p, a0.reshape(1, d), a1p, a2p, g1p, g2p]
    specs = [
        pl.BlockSpec((tm, d), row),
        pl.BlockSpec((8, d), lambda i: (jnp.maximum(i * (tm // 8) - 1, 0), 0)),
        _const_spec((8, d)), _const_spec((3, d, d)), _const_spec((1, d)),
        _const_spec((d, LANES)), _const_spec((LANES, d)), _const_spec((1, d)),
        _const_spec((d, LANES)), _const_spec((LANES, d)),
        _const_spec((d, 2 * LANES)), _const_spec((2 * LANES, d)),
    ]
    has_mix = value_mix is not None
    if has_mix:
        v0, v1, v2 = value_mix
        v1p, v2p = lora(v1, v2, LANES)
        ins += [v0.reshape(1, d), v1p, v2p, v_first]
        specs += [_const_spec((1, d)), _const_spec((d, LANES)), _const_spec((LANES, d)), pl.BlockSpec((tm, d), row)]
    out = jax.ShapeDtypeStruct((n, d), F32)
    return pl.pallas_call(
        functools.partial(_rwkv_proj_kernel, seq=seq, has_mix=has_mix),
        grid=(n // tm,),
        in_specs=specs,
        out_specs=[pl.BlockSpec((tm, d), row)] * 6,
        out_shape=[out] * 6,
        compiler_params=_params("parallel"),
        name="rwkv_proj",
    )(*ins)


def _wkv_kernel(r_ref, k_ref, v_ref, z_ref, a_ref, g_ref, kk_ref, ka_ref, rk_ref, lg_ref, lb_ref, o_ref, s_ref,
                *, tblock, npair):
    c = WKV_CHUNK
    hn = RW_HEAD
    w = 2 * hn
    wt = npair * w

    @pl.when(pl.program_id(2) == 0)
    def _():
        s_ref[...] = jnp.zeros_like(s_ref)

    def iota2(shape, dim):
        return lax.broadcasted_iota(jnp.int32, shape, dim)

    lane_c = iota2((c, w), 1)
    head0 = lane_c < hn
    rw, cw = iota2((w, w), 0), iota2((w, w), 1)
    same_head = (rw // hn) == (cw // hn)
    t_row, s_col = iota2((c, w), 0), lane_c % hn
    strict = s_col < t_row
    incl = s_col <= t_row
    blk16 = (t_row // 16) == (s_col // 16)
    blk32 = (t_row // 32) == (s_col // 32)
    eye = (t_row == s_col).astype(F32)
    r2, c2 = iota2((2 * w, w), 0), iota2((2 * w, w), 1)
    seg_ones2 = (((r2 % w) // hn) == (c2 // hn)).astype(BF16)
    tri3 = (iota2((c, 3 * c), 1) % c <= iota2((c, 3 * c), 0)).astype(BF16)

    def split(t, pieces):
        out = []
        for _ in range(pieces - 1):
            out.append(t.astype(BF16))
            t = t - out[-1].astype(F32)
        return out + [t.astype(BF16)]

    def seg_sum(t):
        return jnp.concatenate(
            [_dot(jnp.concatenate(split(t[:, p * w:(p + 1) * w], 2), axis=1), seg_ones2) for p in range(npair)],
            axis=1)

    def stack(t):
        zero = jnp.zeros_like(t)
        return jnp.concatenate([jnp.where(head0, t, zero), jnp.where(head0, zero, t)], axis=0)

    def bf(t):
        return t.astype(BF16)

    nch = tblock // c
    chunks = range(nch)
    items = [(ci, p) for ci in chunks for p in range(npair)]

    def rows(t, ci):
        return t[ci * c:(ci + 1) * c]

    def sub(t, it):
        ci, p = it
        return t[ci * c:(ci + 1) * c, p * w:(p + 1) * w]

    r, k, v, z, a, g = (ref[0] for ref in (r_ref, k_ref, v_ref, z_ref, a_ref, g_ref))
    kk = k * kk_ref[...]
    kk = kk / jnp.maximum(jnp.sqrt(seg_sum(kk * kk)), 1e-12)
    kmod = k * (1.0 + (a - 1.0) * ka_ref[...])
    bvec = kk * a
    lw = (-math.exp(-0.5)) / (1.0 + jnp.exp(-z))
    cum = jnp.concatenate([_dot(tri3, jnp.concatenate(split(rows(lw, ci), 3), axis=0)) for ci in chunks], axis=0)
    at = -kk * jnp.exp(cum - lw)
    rt = r * jnp.exp(cum)
    inv = jnp.exp(-cum)
    bt = bvec * inv
    kt = kmod * inv
    totals = [cum[(ci + 1) * c - 1:(ci + 1) * c, :] for ci in chunks]
    to_end = jnp.concatenate([jnp.exp(totals[ci] - rows(cum, ci)) for ci in chunks], axis=0)
    bh = bvec * to_end
    kh = kmod * to_end

    n_it = range(len(items))
    atb, rtb = bf(at), bf(rt)
    ast = [stack(sub(atb, it)) for it in items]
    bk = [jnp.concatenate([stack(bf(sub(bt, it))), stack(bf(sub(kt, it)))], axis=0) for it in items]
    prods = [_dot_nt(jnp.concatenate([sub(atb, items[i]), sub(rtb, items[i])], axis=0), bk[i]) for i in n_it]
    l_full = [jnp.where(strict, prods[i][:c, :w], 0.0) for i in n_it]
    a_ak = [bf(jnp.where(strict, prods[i][:c, w:], 0.0)) for i in n_it]
    a_rb = [bf(jnp.where(incl, prods[i][c:, :w], 0.0)) for i in n_it]
    a_rk = [bf(jnp.where(incl, prods[i][c:, w:], 0.0)) for i in n_it]
    vst = [stack(bf(sub(v, it))) for it in items]
    x_loc = [_dot(a_ak[i], vst[i]) for i in n_it]
    y_loc = [_dot(a_rk[i], vst[i]) for i in n_it]

    xf = [bf(jnp.where(blk16, l_full[i], 0.0)) for i in n_it]
    xs = [stack(xf[i]) for i in n_it]
    tm = [eye + jnp.where(blk16, l_full[i], 0.0) for i in n_it]
    for _ in range(3):
        xf = [bf(_dot(xf[i], xs[i])) for i in n_it]
        xs = [stack(xf[i]) for i in n_it]
        tm = [tm[i] + _dot(bf(tm[i]), xs[i]) for i in n_it]
    for inside, outside in ((blk32, blk16), (None, blk32)):
        keep = jnp.logical_not(outside) if inside is None else inside & jnp.logical_not(outside)
        off = [stack(bf(jnp.where(keep, l_full[i], 0.0))) for i in n_it]
        tmb = [bf(tm[i]) for i in n_it]
        half = [bf(_dot(tmb[i], off[i])) for i in n_it]
        tm = [tm[i] + _dot(half[i], stack(tmb[i])) for i in n_it]
    t_fold = [bf(tm[i]) for i in n_it]

    au = [_dot(t_fold[i], jnp.concatenate([ast[i], bf(stack(x_loc[i]))], axis=1)) for i in n_it]
    ah = [au[i][:, :w] for i in n_it]
    ul = [au[i][:, w:] for i in n_it]
    ry = [_dot(a_rb[i], jnp.concatenate([bf(stack(ah[i])), bf(stack(ul[i]))], axis=1)) for i in n_it]
    rh = [bf(sub(rt, items[i]) + ry[i][:, :w]) for i in n_it]
    yl = [y_loc[i] + ry[i][:, w:] for i in n_it]
    w_mat = [bf(jnp.where(same_head, _dot_tn(bf(ah[i]), bf(sub(bh, items[i]))), 0.0)) for i in n_it]
    g_mat = [jnp.where(same_head,
                       _dot_tn(bf(jnp.concatenate([ul[i], sub(v, items[i])], axis=0)),
                               bf(jnp.concatenate([sub(bh, items[i]), sub(kh, items[i])], axis=0))), 0.0)
             for i in n_it]
    decay_c = [jnp.exp(totals[ci][:, p * w:(p + 1) * w]) for ci, p in items]

    s = [s_ref[p] for p in range(npair)]
    ys = [[None] * npair for _ in chunks]
    for i, (ci, p) in enumerate(items):
        sb = bf(s[p])
        ys[ci][p] = _dot_nt(rh[i], sb) + yl[i]
        s[p] = s[p] * decay_c[i] + _dot(sb, w_mat[i]) + g_mat[i]
    for p in range(npair):
        s_ref[p] = s[p]

    y = jnp.concatenate([jnp.concatenate(ys[ci], axis=1) if npair > 1 else ys[ci][0] for ci in chunks], axis=0)
    mean = seg_sum(y) * (1.0 / hn)
    yc = y - mean
    var = seg_sum(yc * yc) * (1.0 / hn)
    yn = yc * lax.rsqrt(var + RW_GN_EPS) * lg_ref[...] + lb_ref[...]
    yn = yn + seg_sum(r * kmod * rk_ref[...]) * v
    o_ref[0] = (yn * g).astype(o_ref.dtype)


def _wkv(r, k, v, z, a, g, k_k, k_a, r_k, lnx_g, lnx_b, batch, seq):
    n, d = r.shape
    w = 2 * RW_HEAD
    npair = WKV_PAIRS
    wt = npair * w
    tb = WKV_TBLOCK
    seq_spec = pl.BlockSpec((1, tb, wt), lambda b, h, t: (b, t, h))
    par_spec = pl.BlockSpec((1, wt), lambda b, h, t: (0, h))
    r3 = lambda t: t.reshape(batch, seq, d)
    out = pl.pallas_call(
        functools.partial(_wkv_kernel, tblock=tb, npair=npair),
        grid=(batch, d // wt, seq // tb),
        in_specs=[seq_spec] * 6 + [par_spec] * 5,
        out_specs=seq_spec,
        out_shape=jax.ShapeDtypeStruct((batch, seq, d), BF16),
        scratch_shapes=[pltpu.VMEM((npair, w, w), F32)],
        compiler_params=_params("parallel", "parallel", "arbitrary"),
        name="wkv7_chunked",
    )(r3(r), r3(k), r3(v), r3(z), r3(a), r3(g), k_k.reshape(1, d), k_a.reshape(1, d), r_k.reshape(1, d),
      lnx_g.reshape(1, d), lnx_b.reshape(1, d))
    return out.reshape(n, d)


def kernel(x, ln1_g, ln1_b, ln2_g, ln2_b, attn_w_qkv, attn_w_o, attn_lambda, attn_subln_g, rw_mu, rw_w_rkv, rw_w_o, rw_w0, rw_w1, rw_w2, rw_a0, rw_a1, rw_a2, rw_g1, rw_g2, rw_k_k, rw_k_a, rw_r_k, rw_lnx_g, rw_lnx_b, rw_v0, rw_v1, rw_v2, moe_rg_w, moe_rg_b, moe_re_w, moe_re_b, moe_w_gu, moe_w_down):
    batch, seq, d = x.shape
    depth = ln1_g.shape[0]
    n = batch * seq
    alpha = (2 * depth) ** 0.25
    x = x.reshape(n, d)
    v_first = None
    for i in range(depth):
        j = i // N_MIXERS
        if i % N_MIXERS == 0:
            lambda_init = 0.8 - 0.6 * math.exp(-0.3 * i)
            lam = attn_lambda[j]
            lam_full = jnp.exp(jnp.sum(lam[0] * lam[1])) - jnp.exp(jnp.sum(lam[2] * lam[3])) + lambda_init
            lam_row = jnp.full((1, 2 * DA_HEAD_DIM), lam_full, F32)
            q_scale = DA_HEAD_DIM ** -0.5 * math.log2(math.e)
            col_scale = jnp.concatenate([jnp.full((d,), q_scale, F32), jnp.ones((2 * d,), F32)])
            w_qkv = (attn_w_qkv[j] * col_scale[None, :]).astype(BF16)
            qkv = _proj(x, w_qkv, BF16)
            mixed = _diff_attention(qkv, lam_row, attn_subln_g[j], lambda_init, batch, seq)
            w_o = attn_w_o[j]
        else:
            value_mix = None if j == 0 else (rw_v0[j - 1], rw_v1[j - 1], rw_v2[j - 1])
            r, k, v, z, a, g = _rwkv_proj(x, seq, rw_mu[j], rw_w_rkv[j], rw_w0[j], rw_w1[j], rw_w2[j], rw_a0[j],
                                          rw_a1[j], rw_a2[j], rw_g1[j], rw_g2[j], value_mix, v_first)
            if value_mix is None:
                v_first = v
            mixed = _wkv(r, k, v, z, a, g, rw_k_k[j], rw_k_a[j], rw_r_k[j].reshape(d), rw_lnx_g[j], rw_lnx_b[j],
                         batch, seq)
            w_o = rw_w_o[j]
        n_router = MOE_GROUPS + MOE_EXPERTS
        w_router = _pad_cols(jnp.concatenate([moe_rg_w[i], moe_re_w[i]], axis=1), LANES)
        wr_hi = w_router.astype(BF16)
        wr_lo = (w_router - wr_hi.astype(F32)).astype(BF16)
        w_router = jnp.concatenate([wr_hi, wr_hi, wr_lo], axis=0)
        b_router = _pad_cols(jnp.concatenate([moe_rg_b[i], moe_re_b[i]]).reshape(1, n_router), LANES)
        x1, x1_packed, logits = _post_mixer(mixed, x, w_o.astype(BF16), ln1_g[i], ln1_b[i], w_router, b_router,
                                            alpha)
        x = _hier_moe_block(x1, x1_packed, logits, moe_w_gu.reshape((-1,) + moe_w_gu.shape[2:]),
                            moe_w_down.reshape((-1,) + moe_w_down.shape[2:]), i, ln2_g[i], ln2_b[i], alpha)
    return x.reshape(batch, seq, d)
```

```python
import functools
import math

import jax
import jax.numpy as jnp
from jax import lax
from jax.experimental import pallas as pl
from jax.experimental.pallas import tpu as pltpu

F32 = jnp.float32
BF16 = jnp.bfloat16

LANES = 128
VMEM_LIMIT = 56 * 1024 * 1024

DA_HEADS = 8
DA_HEAD_DIM = 64
RMS_EPS = 1e-5
RW_HEAD = 64
RW_GN_EPS = 64e-5
MOE_GROUPS = 4
MOE_EPG = 8
MOE_EXPERTS = MOE_GROUPS * MOE_EPG
MOE_TOPK = 2
LN_EPS = 1e-5
N_MIXERS = 2

ROW_TILE = 256
ROUTER_TILE = 512
MOE_TOKEN_TILE = 512
ATTN_TQ = 512
ATTN_TK = 512
WKV_CHUNK = 64
WKV_TBLOCK = 512
WKV_PAIRS = 4
MOE_BLOCK = 256
DMA_UNROLL = 8
COMBINE_GROUPS = 8
NEG_BIG = -1e30


def _dot(a, b, precision=None):
    return jnp.dot(a, b, preferred_element_type=F32, precision=precision)


def _dot_nt(a, b):
    return lax.dot_general(a, b, (((1,), (1,)), ((), ())), preferred_element_type=F32)


def _dot_tn(a, b):
    return lax.dot_general(a, b, (((0,), (0,)), ((), ())), preferred_element_type=F32)


def _params(*sem):
    return pltpu.CompilerParams(dimension_semantics=sem, vmem_limit_bytes=VMEM_LIMIT)


def _const_spec(shape):
    nd = len(shape)
    return pl.BlockSpec(shape, lambda *_: (0,) * nd)


def _proj_kernel(x_ref, w_ref, o_ref):
    o_ref[...] = _dot(x_ref[...].astype(BF16), w_ref[...]).astype(o_ref.dtype)


def _proj(x, w, out_dtype):
    n, d = x.shape
    m = w.shape[1]
    tm = 512
    return pl.pallas_call(
        _proj_kernel,
        grid=(n // tm,),
        in_specs=[pl.BlockSpec((tm, d), lambda i: (i, 0)), _const_spec((d, m))],
        out_specs=pl.BlockSpec((tm, m), lambda i: (i, 0)),
        out_shape=jax.ShapeDtypeStruct((n, m), out_dtype),
        compiler_params=_params("parallel"),
        name="qkv_proj",
    )(x, w)


def _attn_kernel(lam_ref, g_ref, q_ref, k_ref, v_ref, o_ref, vx_ref, m_ref, acc_ref, sa_ref, sb_ref, *, tq, tk,
                 out_scale):
    qi = pl.program_id(2)
    hd = DA_HEAD_DIM
    hw = 2 * hd

    @pl.when(qi == 0)
    def _():
        vx_ref[:, :hw] = v_ref[0]
        vx_ref[:, hw:] = jnp.ones((vx_ref.shape[0], hw), BF16)

    q = q_ref[0]
    lane = lax.broadcasted_iota(jnp.int32, q.shape, 1)
    zero = jnp.zeros_like(q)
    q_stack = jnp.concatenate([jnp.where(lane < hd, q, zero), jnp.where(lane >= hd, q, zero)], axis=0)
    m_ref[...] = jnp.full(m_ref.shape, NEG_BIG, F32)
    acc_ref[...] = jnp.zeros(acc_ref.shape, F32)
    row0 = qi * tq

    def scores(j, s_ref):
        start = pl.multiple_of(j * tk, tk)
        s_ref[...] = _dot_nt(q_stack, k_ref[0, pl.ds(start, tk), :])

    def consume(j, s_ref, masked):
        start = pl.multiple_of(j * tk, tk)
        vb = vx_ref[pl.ds(start, tk), :]
        if masked:
            row = row0 + lax.broadcasted_iota(jnp.int32, (tq, tk), 0)
            col = start + lax.broadcasted_iota(jnp.int32, (tq, tk), 1)
            keep = col <= row
        for c in range(2):
            s = s_ref[c * tq:(c + 1) * tq, :]
            if masked:
                s = jnp.where(keep, s, NEG_BIG)
            chunks = [s[:, i * LANES:(i + 1) * LANES] for i in range(tk // LANES)]
            m_old = m_ref[c]
            m_new = jnp.maximum(m_old, jnp.max(functools.reduce(jnp.maximum, chunks), axis=-1, keepdims=True))
            alpha = jnp.exp2(m_old - m_new)
            m_ref[c] = m_new
            p = jnp.concatenate([jnp.exp2(ch - m_new).astype(BF16) for ch in chunks], axis=1)
            acc_ref[c] = acc_ref[c] * jnp.concatenate([alpha, alpha], axis=1) + _dot(p, vb)

    n = row0 // tk + 1
    n_pairs = (n - 1) // 2
    scores(0, sa_ref)

    def pair(jj, carry):
        j = 2 * jj
        scores(j + 1, sb_ref)
        consume(j, sa_ref, False)
        scores(j + 2, sa_ref)
        consume(j + 1, sb_ref, False)
        return carry

    lax.fori_loop(0, n_pairs, pair, 0)
    jb = 2 * n_pairs

    @pl.when(n % 2 == 1)
    def _():
        consume(jb, sa_ref, True)

    @pl.when(n % 2 == 0)
    def _():
        scores(jb + 1, sb_ref)
        consume(jb, sa_ref, False)
        consume(jb + 1, sb_ref, True)

    a0, a1 = acc_ref[0], acc_ref[1]
    o = a0[:, :hw] * (1.0 / a0[:, hw:]) - lam_ref[...] * (a1[:, :hw] * (1.0 / a1[:, hw:]))
    ms = jnp.mean(o * o, axis=-1, keepdims=True)
    o = o * lax.rsqrt(ms + RMS_EPS) * g_ref[...] * out_scale
    o_ref[0] = o.astype(o_ref.dtype)


def _diff_attention(qkv, lam_row, subln_g, lambda_init, batch, seq):
    d = DA_HEADS * 2 * DA_HEAD_DIM
    qkv = qkv.reshape(batch, seq, 3 * d)
    tq, tk = ATTN_TQ, ATTN_TK
    assert tq <= tk and tk % tq == 0 and seq % tk == 0
    hw = 2 * DA_HEAD_DIM
    out = pl.pallas_call(
        functools.partial(_attn_kernel, tq=tq, tk=tk, out_scale=1.0 - lambda_init),
        grid=(batch, DA_HEADS, seq // tq),
        in_specs=[
            _const_spec((1, hw)),
            _const_spec((1, hw)),
            pl.BlockSpec((1, tq, hw), lambda b, h, i: (b, i, h)),
            pl.BlockSpec((1, seq, hw), lambda b, h, i: (b, 0, DA_HEADS + h)),
            pl.BlockSpec((1, seq, hw), lambda b, h, i: (b, 0, 2 * DA_HEADS + h)),
        ],
        out_specs=pl.BlockSpec((1, tq, hw), lambda b, h, i: (b, i, h)),
        out_shape=jax.ShapeDtypeStruct((batch, seq, d), BF16),
        scratch_shapes=[pltpu.VMEM((seq, 2 * hw), BF16), pltpu.VMEM((2, tq, hw), F32),
                        pltpu.VMEM((2, tq, 2 * hw), F32), pltpu.VMEM((2 * tq, tk), F32),
                        pltpu.VMEM((2 * tq, tk), F32)],
        compiler_params=_params("parallel", "parallel", "arbitrary"),
        name="diff_attn",
    )(lam_row, subln_g.reshape(1, hw), qkv, qkv, qkv)
    return out.reshape(batch * seq, d)


def _layer_norm_rows(z, g, b):
    mu = jnp.mean(z, axis=-1, keepdims=True)
    zc = z - mu
    var = jnp.mean(zc * zc, axis=-1, keepdims=True)
    return zc * lax.rsqrt(var + LN_EPS) * g + b


HIGH16 = 0xFFFF0000


def _pack_bf16_halves(x_hi):
    half = x_hi.shape[1] // 2
    bits = lax.bitcast_convert_type(x_hi, jnp.uint32)
    return (bits[:, :half] >> 16) | (bits[:, half:] & jnp.uint32(HIGH16))


def _unpack_bf16_halves(packed):
    lo = lax.bitcast_convert_type(packed << 16, F32)
    hi = lax.bitcast_convert_type(packed & jnp.uint32(HIGH16), F32)
    return jnp.concatenate([lo, hi], axis=1).astype(BF16)


def _post_mixer_kernel(o_ref, x_ref, w_ref, lg_ref, lb_ref, wr_ref, br_ref, x1_ref, x1p_ref, lo_ref, *, alpha):
    h = _dot(o_ref[...], w_ref[...])
    x1 = _layer_norm_rows(alpha * x_ref[...] + h, lg_ref[...], lb_ref[...])
    x1_ref[...] = x1
    x_hi = x1.astype(BF16)
    x_hi32 = x_hi.astype(F32)
    x_lo = (x1 - x_hi32).astype(BF16)
    x1p_ref[...] = _pack_bf16_halves(x_hi32)
    lo_ref[...] = _dot(jnp.concatenate([x_hi, x_lo, x_hi], axis=1), wr_ref[...]) + br_ref[...]


def _post_mixer(o, x, w_o, ln_g, ln_b, w_router, b_router, alpha):
    n, d = x.shape
    tm = ROW_TILE
    row = lambda i: (i, 0)
    return pl.pallas_call(
        functools.partial(_post_mixer_kernel, alpha=alpha),
        grid=(n // tm,),
        in_specs=[
            pl.BlockSpec((tm, d), row), pl.BlockSpec((tm, d), row), _const_spec((d, d)),
            _const_spec((1, d)), _const_spec((1, d)), _const_spec((3 * d, LANES)), _const_spec((1, LANES)),
        ],
        out_specs=[pl.BlockSpec((tm, d), row), pl.BlockSpec((tm, d // 2), row), pl.BlockSpec((tm, LANES), row)],
        out_shape=[jax.ShapeDtypeStruct((n, d), F32), jax.ShapeDtypeStruct((n, d // 2), jnp.uint32),
                   jax.ShapeDtypeStruct((n, LANES), F32)],
        compiler_params=_params("parallel"),
        name="post_mixer",
    )(o, x, w_o, ln_g.reshape(1, d), ln_b.reshape(1, d), w_router, b_router)


def _router_kernel(lo_ref, idx_ref, gate_ref, cnt_ref, carry_ref):
    step = pl.program_id(0)

    @pl.when(step == 0)
    def _():
        carry_ref[...] = jnp.zeros_like(carry_ref)

    lo = lo_ref[...]
    tm = lo.shape[0]
    lane = lax.broadcasted_iota(jnp.int32, lo.shape, 1)
    big = jnp.int32(LANES)

    def softmax_over(mask):
        mx = jnp.max(jnp.where(mask, lo, NEG_BIG), axis=-1, keepdims=True)
        ex = jnp.where(mask, jnp.exp(lo - mx), 0.0)
        return ex / jnp.sum(ex, axis=-1, keepdims=True)

    def top1(p, mask):
        best = jnp.max(jnp.where(mask, p, -1.0), axis=-1, keepdims=True)
        arg = jnp.min(jnp.where(mask & (p == best), lane, big), axis=-1, keepdims=True)
        return best, arg

    g_mask = lane < MOE_GROUPS
    g_p, g_idx = top1(softmax_over(g_mask), g_mask)
    first = MOE_GROUPS + MOE_EPG * g_idx
    e_mask = (lane >= first) & (lane < first + MOE_EPG)
    e_prob = softmax_over(e_mask)
    p1, i1 = top1(e_prob, e_mask)
    rest = e_mask & (lane != i1)
    p2, i2 = top1(e_prob, rest)
    denom = p1 + p2
    gate1 = g_p * (p1 / denom)
    gate2 = g_p * (p2 / denom)
    e1 = i1 - MOE_GROUPS
    e2 = i2 - MOE_GROUPS

    oh1 = (lane == e1).astype(F32)
    oh2 = (lane == e2).astype(F32)
    both = oh1 + oh2
    r_i = lax.broadcasted_iota(jnp.int32, (tm, tm), 0)
    c_i = lax.broadcasted_iota(jnp.int32, (tm, tm), 1)
    before = (c_i < r_i).astype(BF16)
    seen = _dot(before, both.astype(BF16)) + carry_ref[...]
    pos1 = jnp.sum(seen * oh1, axis=-1, keepdims=True).astype(jnp.int32)
    pos2 = jnp.sum(seen * oh2, axis=-1, keepdims=True).astype(jnp.int32)
    carry_ref[...] += jnp.sum(both, axis=0, keepdims=True)
    cnt_ref[...] = carry_ref[...].astype(jnp.int32)

    zero_i = jnp.zeros(lo.shape, jnp.int32)
    idx_ref[...] = (jnp.where(lane == 0, e1, zero_i) + jnp.where(lane == 1, e2, zero_i)
                    + jnp.where(lane == 2, pos1, zero_i) + jnp.where(lane == 3, pos2, zero_i))
    gate_ref[...] = jnp.where(lane == 0, gate1, 0.0) + jnp.where(lane == 1, gate2, 0.0)


def _router(logits):
    n = logits.shape[0]
    tm = ROUTER_TILE
    row = lambda i: (i, 0)
    return pl.pallas_call(
        _router_kernel,
        grid=(n // tm,),
        in_specs=[pl.BlockSpec((tm, LANES), row)],
        out_specs=[pl.BlockSpec((tm, LANES), row), pl.BlockSpec((tm, LANES), row), _const_spec((1, LANES))],
        out_shape=[jax.ShapeDtypeStruct((n, LANES), jnp.int32), jax.ShapeDtypeStruct((n, LANES), F32),
                   jax.ShapeDtypeStruct((1, LANES), jnp.int32)],
        scratch_shapes=[pltpu.VMEM((1, LANES), F32)],
        compiler_params=_params("arbitrary"),
        name="moe_router",
    )(logits)


def _row_copy(src_ref, src_row, dst_ref, dst_row, sem):
    return pltpu.make_async_copy(src_ref.at[pl.ds(src_row, 1)], dst_ref.at[pl.ds(dst_row, 1)], sem)


def _dispatch_kernel(dest_ref, x_ref, buf_in_ref, buf_ref, sem):
    del buf_in_ref
    tm = x_ref.shape[0]

    def copies(r):
        return [_row_copy(x_ref, r, buf_ref, dest_ref[0, 0, MOE_TOPK * r + s], sem) for s in range(MOE_TOPK)]

    def start(r, carry):
        for cp in copies(r):
            cp.start()
        return carry

    def wait(r, carry):
        for cp in copies(r):
            cp.wait()
        return carry

    lax.fori_loop(0, tm, start, 0, unroll=DMA_UNROLL)
    lax.fori_loop(0, tm, wait, 0, unroll=DMA_UNROLL)


def _dispatch(x1, dest_tiles, n_rows):
    n, d = x1.shape
    tm = MOE_TOKEN_TILE
    zeros = jnp.zeros((n_rows, d), x1.dtype)
    return pl.pallas_call(
        _dispatch_kernel,
        grid=(n // tm,),
        in_specs=[
            pl.BlockSpec((1, 1, MOE_TOPK * tm), lambda i: (i, 0, 0), memory_space=pltpu.SMEM),
            pl.BlockSpec((tm, d), lambda i: (i, 0)),
            pl.BlockSpec(memory_space=pl.ANY),
        ],
        out_specs=pl.BlockSpec(memory_space=pl.ANY),
        out_shape=jax.ShapeDtypeStruct((n_rows, d), x1.dtype),
        scratch_shapes=[pltpu.SemaphoreType.DMA(())],
        input_output_aliases={2: 0},
        compiler_params=_params("arbitrary"),
        name="moe_dispatch",
    )(dest_tiles, x1, zeros)


def _expert_kernel(be_ref, x_ref, wgu_ref, wd_ref, y_ref, wgu_bf_ref, wd_bf_ref, *, hidden):
    i = pl.program_id(0)

    @pl.when((i == 0) | (be_ref[i] != be_ref[jnp.maximum(i - 1, 0)]))
    def _():
        wgu_bf_ref[...] = wgu_ref[0].astype(BF16)
        wd_bf_ref[...] = wd_ref[0].astype(BF16)

    n_parts = 2
    part = x_ref.shape[0] // n_parts
    rows = [pl.ds(p * part, part) for p in range(n_parts)]
    hs = [_dot(_unpack_bf16_halves(x_ref[r, :]), wgu_bf_ref[...]) for r in rows]
    for r, h in zip(rows, hs):
        hg = h[:, :hidden]
        hu = h[:, hidden:]
        act = hg * (1.0 / (1.0 + jnp.exp(-hg))) * hu
        y_ref[r, :] = _dot(act.astype(BF16), wd_bf_ref[...])


def _experts(x_buf, block_e, w_gu, w_down):
    n_rows = x_buf.shape[0]
    hidden, d = w_down.shape[1:]
    tm = MOE_BLOCK
    grid_spec = pltpu.PrefetchScalarGridSpec(
        num_scalar_prefetch=1,
        grid=(n_rows // tm,),
        in_specs=[
            pl.BlockSpec((tm, d // 2), lambda i, be: (i, 0)),
            pl.BlockSpec((1, d, 2 * hidden), lambda i, be: (be[i], 0, 0)),
            pl.BlockSpec((1, hidden, d), lambda i, be: (be[i], 0, 0)),
        ],
        out_specs=pl.BlockSpec((tm, d), lambda i, be: (i, 0)),
        scratch_shapes=[pltpu.VMEM((d, 2 * hidden), BF16), pltpu.VMEM((hidden, d), BF16)],
    )
    return pl.pallas_call(
        functools.partial(_expert_kernel, hidden=hidden),
        grid_spec=grid_spec,
        out_shape=jax.ShapeDtypeStruct((n_rows, d), F32),
        compiler_params=_params("arbitrary"),
        name="moe_experts",
    )(block_e, x_buf, w_gu, w_down)


def _combine_kernel(dest_ref, dest_next_ref, y_ref, gate_ref, x_ref, lg_ref, lb_ref, o_ref, rows_ref, sems, *, alpha):
    i = pl.program_id(0)
    last = pl.num_programs(0) - 1
    tm = x_ref.shape[0]
    slot = i % 2

    def copies(d_ref, r, to_slot):
        return [_row_copy(y_ref, d_ref[0, 0, MOE_TOPK * r + s], rows_ref.at[to_slot, s], r, sems.at[to_slot])
                for s in range(MOE_TOPK)]

    def start_all(d_ref, to_slot):
        def body(r, carry):
            for cp in copies(d_ref, r, to_slot):
                cp.start()
            return carry
        lax.fori_loop(0, tm, body, 0, unroll=DMA_UNROLL)

    def wait_all(d_ref, to_slot):
        def body(r, carry):
            for cp in copies(d_ref, r, to_slot):
                cp.wait()
            return carry
        lax.fori_loop(0, tm, body, 0, unroll=DMA_UNROLL)

    @pl.when(i == 0)
    def _():
        start_all(dest_ref, slot)

    wait_all(dest_ref, slot)
    group = tm // COMBINE_GROUPS
    for g in range(COMBINE_GROUPS):
        for r in range(g * group, (g + 1) * group):
            for cp in copies(dest_next_ref, r, 1 - slot):
                cp.start()
        rows = pl.ds(g * group, group)
        gate = gate_ref[rows, :]
        f = sum(gate[:, s:s + 1] * rows_ref[slot, s, rows, :] for s in range(MOE_TOPK))
        o_ref[rows, :] = _layer_norm_rows(alpha * x_ref[rows, :] + f, lg_ref[...], lb_ref[...])

    @pl.when(i == last)
    def _():
        wait_all(dest_next_ref, 1 - slot)


def _combine(y, dest_tiles, gates, x1, ln_g, ln_b, alpha):
    n, d = x1.shape
    tm = MOE_TOKEN_TILE
    n_tiles = n // tm
    row = lambda i: (i, 0)
    dest_spec = lambda index: pl.BlockSpec((1, 1, MOE_TOPK * tm), index, memory_space=pltpu.SMEM)
    return pl.pallas_call(
        functools.partial(_combine_kernel, alpha=alpha),
        grid=(n_tiles,),
        in_specs=[
            dest_spec(lambda i: (i, 0, 0)),
            dest_spec(lambda i: (jnp.minimum(i + 1, n_tiles - 1), 0, 0)),
            pl.BlockSpec(memory_space=pl.ANY),
            pl.BlockSpec((tm, LANES), row), pl.BlockSpec((tm, d), row), _const_spec((1, d)), _const_spec((1, d)),
        ],
        out_specs=pl.BlockSpec((tm, d), row),
        out_shape=jax.ShapeDtypeStruct((n, d), F32),
        scratch_shapes=[pltpu.VMEM((2, MOE_TOPK, tm, d), F32), pltpu.SemaphoreType.DMA((2,))],
        compiler_params=_params("arbitrary"),
        name="moe_combine",
    )(dest_tiles, dest_tiles, y, gates, x1, ln_g.reshape(1, d), ln_b.reshape(1, d))


def _hier_moe_block(x1, x1_packed, logits, w_gu, w_down, layer, ln_g, ln_b, alpha):
    n_tok, d = x1.shape
    idx, gates, counts = _router(logits)
    counts = counts[0, :MOE_EXPERTS]
    padded = (counts + MOE_BLOCK - 1) // MOE_BLOCK * MOE_BLOCK
    pad_end = jnp.cumsum(padded)
    pad_start = pad_end - padded
    expert = idx[:, :MOE_TOPK]
    pos = idx[:, MOE_TOPK:2 * MOE_TOPK]
    e_iota = jnp.arange(MOE_EXPERTS, dtype=jnp.int32)
    base = jnp.sum(jnp.where(expert[:, :, None] == e_iota, pad_start, 0), axis=-1)
    dest = (base + pos).astype(jnp.int32)
    n_blocks = n_tok * MOE_TOPK // MOE_BLOCK + MOE_EXPERTS
    n_rows = n_blocks * MOE_BLOCK
    block_start = jnp.arange(n_blocks, dtype=jnp.int32) * MOE_BLOCK
    block_e = jnp.minimum(jnp.sum((pad_end[None, :] <= block_start[:, None]).astype(jnp.int32), axis=1),
                          MOE_EXPERTS - 1) + layer * MOE_EXPERTS
    dest_tiles = dest.reshape(n_tok // MOE_TOKEN_TILE, 1, MOE_TOPK * MOE_TOKEN_TILE)
    x_buf = _dispatch(x1_packed, dest_tiles, n_rows)
    y = _experts(x_buf, block_e, w_gu, w_down)
    return _combine(y, dest_tiles, gates, x1, ln_g, ln_b, alpha)


def _rwkv_proj_kernel(*refs, seq, has_mix):
    if has_mix:
        (x_ref, xp_ref, mu_ref, wrkv_ref, w0_ref, w1_ref, w2_ref, a0_ref, a1_ref, a2_ref, g1_ref, g2_ref,
         v0_ref, v1_ref, v2_ref, vf_ref, r_ref, k_ref, v_ref, z_ref, a_ref, g_ref) = refs
    else:
        (x_ref, xp_ref, mu_ref, wrkv_ref, w0_ref, w1_ref, w2_ref, a0_ref, a1_ref, a2_ref, g1_ref, g2_ref,
         r_ref, k_ref, v_ref, z_ref, a_ref, g_ref) = refs
    x = x_ref[...]
    tm = x.shape[0]
    row = lax.broadcasted_iota(jnp.int32, x.shape, 0)
    at_seq_start = (pl.program_id(0) * tm) % seq == 0
    prev_last = jnp.where(at_seq_start, 0.0, xp_ref[7:8, :])
    shifted = jnp.where(row == 0, prev_last, pltpu.roll(x, 1, axis=0))
    xx = shifted - x

    def mixed(n):
        return (x + xx * mu_ref[n:n + 1, :]).astype(BF16)

    def sigmoid(t):
        return 1.0 / (1.0 + jnp.exp(-t))

    r_ref[...] = _dot(mixed(0), wrkv_ref[0])
    k_ref[...] = _dot(mixed(1), wrkv_ref[1])
    xv = mixed(2)
    v = _dot(xv, wrkv_ref[2])
    if has_mix:
        mix = sigmoid(v0_ref[...] + _dot(_dot(xv, v1_ref[...]).astype(BF16), v2_ref[...]))
        v = v + (vf_ref[...] - v) * mix
    v_ref[...] = v
    z_ref[...] = w0_ref[...] + _dot(jnp.tanh(_dot(mixed(3), w1_ref[...])).astype(BF16), w2_ref[...])
    a_ref[...] = sigmoid(a0_ref[...] + _dot(_dot(mixed(4), a1_ref[...]).astype(BF16), a2_ref[...]))
    g_ref[...] = _dot(sigmoid(_dot(mixed(5), g1_ref[...])).astype(BF16), g2_ref[...])


def _pad_cols(w, width):
    return jnp.pad(w, ((0, 0), (0, width - w.shape[1])))


def _pad_rows(w, height):
    return jnp.pad(w, ((0, height - w.shape[0]), (0, 0)))


def _rwkv_proj(x, seq, mu, w_rkv, w0, w1, w2, a0, a1, a2, g1, g2, value_mix, v_first):
    n, d = x.shape
    tm = ROW_TILE
    row = lambda i: (i, 0)
    lora = lambda w_in, w_out, width: (_pad_cols(w_in, width).astype(BF16), _pad_rows(w_out, width).astype(BF16))
    w1p, w2p = lora(w1, w2, LANES)
    a1p, a2p = lora(a1, a2, LANES)
    g1p, g2p = lora(g1, g2, 2 * LANES)
    mu8 = _pad_rows(mu, 8)
    ins = [x, x, mu8, w_rkv.astype(BF16), w0.reshape(1, d), w1p, w2p, a0.reshape(1, d), a1p, a2p, g1p, g2p]
    specs = [
        pl.BlockSpec((tm, d), row),
        pl.BlockSpec((8, d), lambda i: (jnp.maximum(i * (tm // 8) - 1, 0), 0)),
        _const_spec((8, d)), _const_spec((3, d, d)), _const_spec((1, d)),
        _const_spec((d, LANES)), _const_spec((LANES, d)), _const_spec((1, d)),
        _const_spec((d, LANES)), _const_spec((LANES, d)),
        _const_spec((d, 2 * LANES)), _const_spec((2 * LANES, d)),
    ]
    has_mix = value_mix is not None
    if has_mix:
        v0, v1, v2 = value_mix
        v1p, v2p = lora(v1, v2, LANES)
        ins += [v0.reshape(1, d), v1p, v2p, v_first]
        specs += [_const_spec((1, d)), _const_spec((d, LANES)), _const_spec((LANES, d)), pl.BlockSpec((tm, d), row)]
    out = jax.ShapeDtypeStruct((n, d), F32)
    return pl.pallas_call(
        functools.partial(_rwkv_proj_kernel, seq=seq, has_mix=has_mix),
        grid=(n // tm,),
        in_specs=specs,
        out_specs=[pl.BlockSpec((tm, d), row)] * 6,
        out_shape=[out] * 6,
        compiler_params=_params("parallel"),
        name="rwkv_proj",
    )(*ins)


def _wkv_kernel(r_ref, k_ref, v_ref, z_ref, a_ref, g_ref, kk_ref, ka_ref, rk_ref, lg_ref, lb_ref, o_ref, s_ref,
                *, tblock, npair):
    c = WKV_CHUNK
    hn = RW_HEAD
    w = 2 * hn
    wt = npair * w

    @pl.when(pl.program_id(2) == 0)
    def _():
        s_ref[...] = jnp.zeros_like(s_ref)

    def iota2(shape, dim):
        return lax.broadcasted_iota(jnp.int32, shape, dim)

    lane_c = iota2((c, w), 1)
    head0 = lane_c < hn
    rw, cw = iota2((w, w), 0), iota2((w, w), 1)
    same_head = (rw // hn) == (cw // hn)
    t_row, s_col = iota2((c, w), 0), lane_c % hn
    strict = s_col < t_row
    incl = s_col <= t_row
    blk16 = (t_row // 16) == (s_col // 16)
    blk32 = (t_row // 32) == (s_col // 32)
    eye = (t_row == s_col).astype(F32)
    r2, c2 = iota2((2 * w, w), 0), iota2((2 * w, w), 1)
    seg_ones2 = (((r2 % w) // hn) == (c2 // hn)).astype(BF16)
    tri3 = (iota2((c, 3 * c), 1) % c <= iota2((c, 3 * c), 0)).astype(BF16)

    def split(t, pieces):
        out = []
        for _ in range(pieces - 1):
            out.append(t.astype(BF16))
            t = t - out[-1].astype(F32)
        return out + [t.astype(BF16)]

    def seg_sum(t):
        return jnp.concatenate(
            [_dot(jnp.concatenate(split(t[:, p * w:(p + 1) * w], 2), axis=1), seg_ones2) for p in range(npair)],
            axis=1)

    def stack(t):
        zero = jnp.zeros_like(t)
        return jnp.concatenate([jnp.where(head0, t, zero), jnp.where(head0, zero, t)], axis=0)

    def bf(t):
        return t.astype(BF16)

    nch = tblock // c
    chunks = range(nch)
    items = [(ci, p) for ci in chunks for p in range(npair)]

    def rows(t, ci):
        return t[ci * c:(ci + 1) * c]

    def sub(t, it):
        ci, p = it
        return t[ci * c:(ci + 1) * c, p * w:(p + 1) * w]

    r, k, v, z, a, g = (ref[0] for ref in (r_ref, k_ref, v_ref, z_ref, a_ref, g_ref))
    kk = k * kk_ref[...]
    kk = kk / jnp.maximum(jnp.sqrt(seg_sum(kk * kk)), 1e-12)
    kmod = k * (1.0 + (a - 1.0) * ka_ref[...])
    bvec = kk * a
    lw = (-math.exp(-0.5)) / (1.0 + jnp.exp(-z))
    cum = jnp.concatenate([_dot(tri3, jnp.concatenate(split(rows(lw, ci), 3), axis=0)) for ci in chunks], axis=0)
    at = -kk * jnp.exp(cum - lw)
    rt = r * jnp.exp(cum)
    inv = jnp.exp(-cum)
    bt = bvec * inv
    kt = kmod * inv
    totals = [cum[(ci + 1) * c - 1:(ci + 1) * c, :] for ci in chunks]
    to_end = jnp.concatenate([jnp.exp(totals[ci] - rows(cum, ci)) for ci in chunks], axis=0)
    bh = bvec * to_end
    kh = kmod * to_end

    n_it = range(len(items))
    atb, rtb = bf(at), bf(rt)
    ast = [stack(sub(atb, it)) for it in items]
    bk = [jnp.concatenate([stack(bf(sub(bt, it))), stack(bf(sub(kt, it)))], axis=0) for it in items]
    prods = [_dot_nt(jnp.concatenate([sub(atb, items[i]), sub(rtb, items[i])], axis=0), bk[i]) for i in n_it]
    l_full = [jnp.where(strict, prods[i][:c, :w], 0.0) for i in n_it]
    a_ak = [bf(jnp.where(strict, prods[i][:c, w:], 0.0)) for i in n_it]
    a_rb = [bf(jnp.where(incl, prods[i][c:, :w], 0.0)) for i in n_it]
    a_rk = [bf(jnp.where(incl, prods[i][c:, w:], 0.0)) for i in n_it]
    vst = [stack(bf(sub(v, it))) for it in items]
    x_loc = [_dot(a_ak[i], vst[i]) for i in n_it]
    y_loc = [_dot(a_rk[i], vst[i]) for i in n_it]

    xf = [bf(jnp.where(blk16, l_full[i], 0.0)) for i in n_it]
    xs = [stack(xf[i]) for i in n_it]
    tm = [eye + jnp.where(blk16, l_full[i], 0.0) for i in n_it]
    for _ in range(3):
        xf = [bf(_dot(xf[i], xs[i])) for i in n_it]
        xs = [stack(xf[i]) for i in n_it]
        tm = [tm[i] + _dot(bf(tm[i]), xs[i]) for i in n_it]
    for inside, outside in ((blk32, blk16), (None, blk32)):
        keep = jnp.logical_not(outside) if inside is None else inside & jnp.logical_not(outside)
        off = [stack(bf(jnp.where(keep, l_full[i], 0.0))) for i in n_it]
        tmb = [bf(tm[i]) for i in n_it]
        half = [bf(_dot(tmb[i], off[i])) for i in n_it]
        tm = [tm[i] + _dot(half[i], stack(tmb[i])) for i in n_it]
    t_fold = [bf(tm[i]) for i in n_it]

    au = [_dot(t_fold[i], jnp.concatenate([ast[i], bf(stack(x_loc[i]))], axis=1)) for i in n_it]
    ah = [au[i][:, :w] for i in n_it]
    ul = [au[i][:, w:] for i in n_it]
    ry = [_dot(a_rb[i], jnp.concatenate([bf(stack(ah[i])), bf(stack(ul[i]))], axis=1)) for i in n_it]
    rh = [bf(sub(rt, items[i]) + ry[i][:, :w]) for i in n_it]
    yl = [y_loc[i] + ry[i][:, w:] for i in n_it]
    w_mat = [bf(jnp.where(same_head, _dot_tn(bf(ah[i]), bf(sub(bh, items[i]))), 0.0)) for i in n_it]
    g_mat = [jnp.where(same_head,
                       _dot_tn(bf(jnp.concatenate([ul[i], sub(v, items[i])], axis=0)),
                               bf(jnp.concatenate([sub(bh, items[i]), sub(kh, items[i])], axis=0))), 0.0)
             for i in n_it]
    decay_c = [jnp.exp(totals[ci][:, p * w:(p + 1) * w]) for ci, p in items]

    s = [s_ref[p] for p in range(npair)]
    ys = [[None] * npair for _ in chunks]
    for i, (ci, p) in enumerate(items):
        sb = bf(s[p])
        ys[ci][p] = _dot_nt(rh[i], sb) + yl[i]
        s[p] = s[p] * decay_c[i] + _dot(sb, w_mat[i]) + g_mat[i]
    for p in range(npair):
        s_ref[p] = s[p]

    y = jnp.concatenate([jnp.concatenate(ys[ci], axis=1) if npair > 1 else ys[ci][0] for ci in chunks], axis=0)
    mean = seg_sum(y) * (1.0 / hn)
    yc = y - mean
    var = seg_sum(yc * yc) * (1.0 / hn)
    yn = yc * lax.rsqrt(var + RW_GN_EPS) * lg_ref[...] + lb_ref[...]
    yn = yn + seg_sum(r * kmod * rk_ref[...]) * v
    o_ref[0] = (yn * g).astype(o_ref.dtype)


def _wkv(r, k, v, z, a, g, k_k, k_a, r_k, lnx_g, lnx_b, batch, seq):
    n, d = r.shape
    w = 2 * RW_HEAD
    npair = WKV_PAIRS
    wt = npair * w
    tb = WKV_TBLOCK
    seq_spec = pl.BlockSpec((1, tb, wt), lambda b, h, t: (b, t, h))
    par_spec = pl.BlockSpec((1, wt), lambda b, h, t: (0, h))
    r3 = lambda t: t.reshape(batch, seq, d)
    out = pl.pallas_call(
        functools.partial(_wkv_kernel, tblock=tb, npair=npair),
        grid=(batch, d // wt, seq // tb),
        in_specs=[seq_spec] * 6 + [par_spec] * 5,
        out_specs=seq_spec,
        out_shape=jax.ShapeDtypeStruct((batch, seq, d), BF16),
        scratch_shapes=[pltpu.VMEM((npair, w, w), F32)],
        compiler_params=_params("parallel", "parallel", "arbitrary"),
        name="wkv7_chunked",
    )(r3(r), r3(k), r3(v), r3(z), r3(a), r3(g), k_k.reshape(1, d), k_a.reshape(1, d), r_k.reshape(1, d),
      lnx_g.reshape(1, d), lnx_b.reshape(1, d))
    return out.reshape(n, d)


def kernel(x, ln1_g, ln1_b, ln2_g, ln2_b, attn_w_qkv, attn_w_o, attn_lambda, attn_subln_g, rw_mu, rw_w_rkv, rw_w_o, rw_w0, rw_w1, rw_w2, rw_a0, rw_a1, rw_a2, rw_g1, rw_g2, rw_k_k, rw_k_a, rw_r_k, rw_lnx_g, rw_lnx_b, rw_v0, rw_v1, rw_v2, moe_rg_w, moe_rg_b, moe_re_w, moe_re_b, moe_w_gu, moe_w_down):
    batch, seq, d = x.shape
    depth = ln1_g.shape[0]
    n = batch * seq
    alpha = (2 * depth) ** 0.25
    x = x.reshape(n, d)
    v_first = None
    for i in range(depth):
        j = i // N_MIXERS
        if i % N_MIXERS == 0:
            lambda_init = 0.8 - 0.6 * math.exp(-0.3 * i)
            lam = attn_lambda[j]
            lam_full = jnp.exp(jnp.sum(lam[0] * lam[1])) - jnp.exp(jnp.sum(lam[2] * lam[3])) + lambda_init
            lam_row = jnp.full((1, 2 * DA_HEAD_DIM), lam_full, F32)
            q_scale = DA_HEAD_DIM ** -0.5 * math.log2(math.e)
            col_scale = jnp.concatenate([jnp.full((d,), q_scale, F32), jnp.ones((2 * d,), F32)])
            w_qkv = (attn_w_qkv[j] * col_scale[None, :]).astype(BF16)
            qkv = _proj(x, w_qkv, BF16)
            mixed = _diff_attention(qkv, lam_row, attn_subln_g[j], lambda_init, batch, seq)
            w_o = attn_w_o[j]
        else:
            value_mix = None if j == 0 else (rw_v0[j - 1], rw_v1[j - 1], rw_v2[j - 1])
            r, k, v, z, a, g = _rwkv_proj(x, seq, rw_mu[j], rw_w_rkv[j], rw_w0[j], rw_w1[j], rw_w2[j], rw_a0[j],
                                          rw_a1[j], rw_a2[j], rw_g1[j], rw_g2[j], value_mix, v_first)
            if value_mix is None:
                v_first = v
            mixed = _wkv(r, k, v, z, a, g, rw_k_k[j], rw_k_a[j], rw_r_k[j].reshape(d), rw_lnx_g[j], rw_lnx_b[j],
                         batch, seq)
            w_o = rw_w_o[j]
        n_router = MOE_GROUPS + MOE_EXPERTS
        w_router = _pad_cols(jnp.concatenate([moe_rg_w[i], moe_re_w[i]], axis=1), LANES)
        wr_hi = w_router.astype(BF16)
        wr_lo = (w_router - wr_hi.astype(F32)).astype(BF16)
        w_router = jnp.concatenate([wr_hi, wr_hi, wr_lo], axis=0)
        b_router = _pad_cols(jnp.concatenate([moe_rg_b[i], moe_re_b[i]]).reshape(1, n_router), LANES)
        x1, x1_packed, logits = _post_mixer(mixed, x, w_o.astype(BF16), ln1_g[i], ln1_b[i], w_router, b_router,
                                            alpha)
        x = _hier_moe_block(x1, x1_packed, logits, moe_w_gu.reshape((-1,) + moe_w_gu.shape[2:]),
                            moe_w_down.reshape((-1,) + moe_w_down.shape[2:]), i, ln2_g[i], ln2_b[i], alpha)
    return x.reshape(batch, seq, d)
```

```python
import functools
import math

import jax
import jax.numpy as jnp
from jax import lax
from jax.experimental import pallas as pl
from jax.experimental.pallas import tpu as pltpu

F32 = jnp.float32
BF16 = jnp.bfloat16

LANES = 128
VMEM_LIMIT = 56 * 1024 * 1024

DA_HEADS = 8
DA_HEAD_DIM = 64
RMS_EPS = 1e-5
RW_HEAD = 64
RW_GN_EPS = 64e-5
MOE_GROUPS = 4
MOE_EPG = 8
MOE_EXPERTS = MOE_GROUPS * MOE_EPG
MOE_TOPK = 2
LN_EPS = 1e-5
N_MIXERS = 2

ROW_TILE = 256
ROUTER_TILE = 512
MOE_TOKEN_TILE = 512
ATTN_TQ = 512
ATTN_TK = 512
WKV_CHUNK = 64
WKV_TBLOCK = 512
WKV_PAIRS = 4
MOE_BLOCK = 256
DMA_UNROLL = 8
COMBINE_GROUPS = 8
NEG_BIG = -1e30


def _dot(a, b, precision=None):
    return jnp.dot(a, b, preferred_element_type=F32, precision=precision)


def _dot_nt(a, b):
    return lax.dot_general(a, b, (((1,), (1,)), ((), ())), preferred_element_type=F32)


def _dot_tn(a, b):
    return lax.dot_general(a, b, (((0,), (0,)), ((), ())), preferred_element_type=F32)


def _params(*sem):
    return pltpu.CompilerParams(dimension_semantics=sem, vmem_limit_bytes=VMEM_LIMIT)


def _const_spec(shape):
    nd = len(shape)
    return pl.BlockSpec(shape, lambda *_: (0,) * nd)


def _proj_kernel(x_ref, w_ref, o_ref):
    o_ref[...] = _dot(x_ref[...].astype(BF16), w_ref[...]).astype(o_ref.dtype)


def _proj(x, w, out_dtype):
    n, d = x.shape
    m = w.shape[1]
    tm = 512
    return pl.pallas_call(
        _proj_kernel,
        grid=(n // tm,),
        in_specs=[pl.BlockSpec((tm, d), lambda i: (i, 0)), _const_spec((d, m))],
        out_specs=pl.BlockSpec((tm, m), lambda i: (i, 0)),
        out_shape=jax.ShapeDtypeStruct((n, m), out_dtype),
        compiler_params=_params("parallel"),
        name="qkv_proj",
    )(x, w)


def _attn_kernel(lam_ref, g_ref, q_ref, k_ref, v_ref, o_ref, vx_ref, m_ref, acc_ref, sa_ref, sb_ref, *, tq, tk,
                 out_scale):
    qi = pl.program_id(2)
    hd = DA_HEAD_DIM
    hw = 2 * hd

    @pl.when(qi == 0)
    def _():
        vx_ref[:, :hw] = v_ref[0]
        vx_ref[:, hw:] = jnp.ones((vx_ref.shape[0], hw), BF16)

    q = q_ref[0]
    lane = lax.broadcasted_iota(jnp.int32, q.shape, 1)
    zero = jnp.zeros_like(q)
    q_stack = jnp.concatenate([jnp.where(lane < hd, q, zero), jnp.where(lane >= hd, q, zero)], axis=0)
    m_ref[...] = jnp.full(m_ref.shape, NEG_BIG, F32)
    acc_ref[...] = jnp.zeros(acc_ref.shape, F32)
    row0 = qi * tq

    def scores(j, s_ref):
        start = pl.multiple_of(j * tk, tk)
        s_ref[...] = _dot_nt(q_stack, k_ref[0, pl.ds(start, tk), :])

    def consume(j, s_ref, masked):
        start = pl.multiple_of(j * tk, tk)
        vb = vx_ref[pl.ds(start, tk), :]
        if masked:
            row = row0 + lax.broadcasted_iota(jnp.int32, (tq, tk), 0)
            col = start + lax.broadcasted_iota(jnp.int32, (tq, tk), 1)
            keep = col <= row
        for c in range(2):
            s = s_ref[c * tq:(c + 1) * tq, :]
            if masked:
                s = jnp.where(keep, s, NEG_BIG)
            chunks = [s[:, i * LANES:(i + 1) * LANES] for i in range(tk // LANES)]
            m_old = m_ref[c]
            m_new = jnp.maximum(m_old, jnp.max(functools.reduce(jnp.maximum, chunks), axis=-1, keepdims=True))
            alpha = jnp.exp2(m_old - m_new)
            m_ref[c] = m_new
            p = jnp.concatenate([jnp.exp2(ch - m_new).astype(BF16) for ch in chunks], axis=1)
            acc_ref[c] = acc_ref[c] * jnp.concatenate([alpha, alpha], axis=1) + _dot(p, vb)

    n = row0 // tk + 1
    n_pairs = (n - 1) // 2
    scores(0, sa_ref)

    def pair(jj, carry):
        j = 2 * jj
        scores(j + 1, sb_ref)
        consume(j, sa_ref, False)
        scores(j + 2, sa_ref)
        consume(j + 1, sb_ref, False)
        return carry

    lax.fori_loop(0, n_pairs, pair, 0)
    jb = 2 * n_pairs

    @pl.when(n % 2 == 1)
    def _():
        consume(jb, sa_ref, True)

    @pl.when(n % 2 == 0)
    def _():
        scores(jb + 1, sb_ref)
        consume(jb, sa_ref, False)
        consume(jb + 1, sb_ref, True)

    a0, a1 = acc_ref[0], acc_ref[1]
    o = a0[:, :hw] * (1.0 / a0[:, hw:]) - lam_ref[...] * (a1[:, :hw] * (1.0 / a1[:, hw:]))
    ms = jnp.mean(o * o, axis=-1, keepdims=True)
    o = o * lax.rsqrt(ms + RMS_EPS) * g_ref[...] * out_scale
    o_ref[0] = o.astype(o_ref.dtype)


def _diff_attention(qkv, lam_row, subln_g, lambda_init, batch, seq):
    d = DA_HEADS * 2 * DA_HEAD_DIM
    qkv = qkv.reshape(batch, seq, 3 * d)
    tq, tk = ATTN_TQ, ATTN_TK
    assert tq <= tk and tk % tq == 0 and seq % tk == 0
    hw = 2 * DA_HEAD_DIM
    out = pl.pallas_call(
        functools.partial(_attn_kernel, tq=tq, tk=tk, out_scale=1.0 - lambda_init),
        grid=(batch, DA_HEADS, seq // tq),
        in_specs=[
            _const_spec((1, hw)),
            _const_spec((1, hw)),
            pl.BlockSpec((1, tq, hw), lambda b, h, i: (b, i, h)),
            pl.BlockSpec((1, seq, hw), lambda b, h, i: (b, 0, DA_HEADS + h)),
            pl.BlockSpec((1, seq, hw), lambda b, h, i: (b, 0, 2 * DA_HEADS + h)),
        ],
        out_specs=pl.BlockSpec((1, tq, hw), lambda b, h, i: (b, i, h)),
        out_shape=jax.ShapeDtypeStruct((batch, seq, d), BF16),
        scratch_shapes=[pltpu.VMEM((seq, 2 * hw), BF16), pltpu.VMEM((2, tq, hw), F32),
                        pltpu.VMEM((2, tq, 2 * hw), F32), pltpu.VMEM((2 * tq, tk), F32),
                        pltpu.VMEM((2 * tq, tk), F32)],
        compiler_params=_params("parallel", "parallel", "arbitrary"),
        name="diff_attn",
    )(lam_row, subln_g.reshape(1, hw), qkv, qkv, qkv)
    return out.reshape(batch * seq, d)


def _layer_norm_rows(z, g, b):
    mu = jnp.mean(z, axis=-1, keepdims=True)
    zc = z - mu
    var = jnp.mean(zc * zc, axis=-1, keepdims=True)
    return zc * lax.rsqrt(var + LN_EPS) * g + b


HIGH16 = 0xFFFF0000


def _pack_bf16_halves(x_hi):
    half = x_hi.shape[1] // 2
    bits = lax.bitcast_convert_type(x_hi, jnp.uint32)
    return (bits[:, :half] >> 16) | (bits[:, half:] & jnp.uint32(HIGH16))


def _unpack_bf16_halves(packed):
    lo = lax.bitcast_convert_type(packed << 16, F32)
    hi = lax.bitcast_convert_type(packed & jnp.uint32(HIGH16), F32)
    return jnp.concatenate([lo, hi], axis=1)


def _post_mixer_kernel(o_ref, x_ref, w_ref, lg_ref, lb_ref, wr_ref, br_ref, x1_ref, x1p_ref, lo_ref, *, alpha):
    h = _dot(o_ref[...], w_ref[...])
    x1 = _layer_norm_rows(alpha * x_ref[...] + h, lg_ref[...], lb_ref[...])
    x1_ref[...] = x1
    x_hi = x1.astype(BF16)
    x_hi32 = x_hi.astype(F32)
    x_lo = (x1 - x_hi32).astype(BF16)
    x1p_ref[...] = _pack_bf16_halves(x_hi32)
    lo_ref[...] = _dot(jnp.concatenate([x_hi, x_lo, x_hi], axis=1), wr_ref[...]) + br_ref[...]


def _post_mixer(o, x, w_o, ln_g, ln_b, w_router, b_router, alpha):
    n, d = x.shape
    tm = ROW_TILE
    row = lambda i: (i, 0)
    return pl.pallas_call(
        functools.partial(_post_mixer_kernel, alpha=alpha),
        grid=(n // tm,),
        in_specs=[
            pl.BlockSpec((tm, d), row), pl.BlockSpec((tm, d), row), _const_spec((d, d)),
            _const_spec((1, d)), _const_spec((1, d)), _const_spec((3 * d, LANES)), _const_spec((1, LANES)),
        ],
        out_specs=[pl.BlockSpec((tm, d), row), pl.BlockSpec((tm, d // 2), row), pl.BlockSpec((tm, LANES), row)],
        out_shape=[jax.ShapeDtypeStruct((n, d), F32), jax.ShapeDtypeStruct((n, d // 2), jnp.uint32),
                   jax.ShapeDtypeStruct((n, LANES), F32)],
        compiler_params=_params("parallel"),
        name="post_mixer",
    )(o, x, w_o, ln_g.reshape(1, d), ln_b.reshape(1, d), w_router, b_router)


def _router_kernel(lo_ref, idx_ref, gate_ref, cnt_ref, carry_ref):
    step = pl.program_id(0)

    @pl.when(step == 0)
    def _():
        carry_ref[...] = jnp.zeros_like(carry_ref)

    lo = lo_ref[...]
    tm = lo.shape[0]
    lane = lax.broadcasted_iota(jnp.int32, lo.shape, 1)
    big = jnp.int32(LANES)

    def softmax_over(mask):
        mx = jnp.max(jnp.where(mask, lo, NEG_BIG), axis=-1, keepdims=True)
        ex = jnp.where(mask, jnp.exp(lo - mx), 0.0)
        return ex / jnp.sum(ex, axis=-1, keepdims=True)

    def top1(p, mask):
        best = jnp.max(jnp.where(mask, p, -1.0), axis=-1, keepdims=True)
        arg = jnp.min(jnp.where(mask & (p == best), lane, big), axis=-1, keepdims=True)
        return best, arg

    g_mask = lane < MOE_GROUPS
    g_p, g_idx = top1(softmax_over(g_mask), g_mask)
    first = MOE_GROUPS + MOE_EPG * g_idx
    e_mask = (lane >= first) & (lane < first + MOE_EPG)
    e_prob = softmax_over(e_mask)
    p1, i1 = top1(e_prob, e_mask)
    rest = e_mask & (lane != i1)
    p2, i2 = top1(e_prob, rest)
    denom = p1 + p2
    gate1 = g_p * (p1 / denom)
    gate2 = g_p * (p2 / denom)
    e1 = i1 - MOE_GROUPS
    e2 = i2 - MOE_GROUPS

    oh1 = (lane == e1).astype(F32)
    oh2 = (lane == e2).astype(F32)
    both = oh1 + oh2
    r_i = lax.broadcasted_iota(jnp.int32, (tm, tm), 0)
    c_i = lax.broadcasted_iota(jnp.int32, (tm, tm), 1)
    before = (c_i < r_i).astype(BF16)
    seen = _dot(before, both.astype(BF16)) + carry_ref[...]
    pos1 = jnp.sum(seen * oh1, axis=-1, keepdims=True).astype(jnp.int32)
    pos2 = jnp.sum(seen * oh2, axis=-1, keepdims=True).astype(jnp.int32)
    carry_ref[...] += jnp.sum(both, axis=0, keepdims=True)
    cnt_ref[...] = carry_ref[...].astype(jnp.int32)

    zero_i = jnp.zeros(lo.shape, jnp.int32)
    idx_ref[...] = (jnp.where(lane == 0, e1, zero_i) + jnp.where(lane == 1, e2, zero_i)
                    + jnp.where(lane == 2, pos1, zero_i) + jnp.where(lane == 3, pos2, zero_i))
    gate_ref[...] = jnp.where(lane == 0, gate1, 0.0) + jnp.where(lane == 1, gate2, 0.0)


def _router(logits):
    n = logits.shape[0]
    tm = ROUTER_TILE
    row = lambda i: (i, 0)
    return pl.pallas_call(
        _router_kernel,
        grid=(n // tm,),
        in_specs=[pl.BlockSpec((tm, LANES), row)],
        out_specs=[pl.BlockSpec((tm, LANES), row), pl.BlockSpec((tm, LANES), row), _const_spec((1, LANES))],
        out_shape=[jax.ShapeDtypeStruct((n, LANES), jnp.int32), jax.ShapeDtypeStruct((n, LANES), F32),
                   jax.ShapeDtypeStruct((1, LANES), jnp.int32)],
        scratch_shapes=[pltpu.VMEM((1, LANES), F32)],
        compiler_params=_params("arbitrary"),
        name="moe_router",
    )(logits)


def _row_copy(src_ref, src_row, dst_ref, dst_row, sem):
    return pltpu.make_async_copy(src_ref.at[pl.ds(src_row, 1)], dst_ref.at[pl.ds(dst_row, 1)], sem)


def _dispatch_kernel(dest_ref, x_ref, buf_in_ref, buf_ref, sem):
    del buf_in_ref
    tm = x_ref.shape[0]

    def copies(r):
        return [_row_copy(x_ref, r, buf_ref, dest_ref[0, 0, MOE_TOPK * r + s], sem) for s in range(MOE_TOPK)]

    def start(r, carry):
        for cp in copies(r):
            cp.start()
        return carry

    def wait(r, carry):
        for cp in copies(r):
            cp.wait()
        return carry

    lax.fori_loop(0, tm, start, 0, unroll=DMA_UNROLL)
    lax.fori_loop(0, tm, wait, 0, unroll=DMA_UNROLL)


def _dispatch(x1, dest_tiles, n_rows):
    n, d = x1.shape
    tm = MOE_TOKEN_TILE
    zeros = jnp.zeros((n_rows, d), x1.dtype)
    return pl.pallas_call(
        _dispatch_kernel,
        grid=(n // tm,),
        in_specs=[
            pl.BlockSpec((1, 1, MOE_TOPK * tm), lambda i: (i, 0, 0), memory_space=pltpu.SMEM),
            pl.BlockSpec((tm, d), lambda i: (i, 0)),
            pl.BlockSpec(memory_space=pl.ANY),
        ],
        out_specs=pl.BlockSpec(memory_space=pl.ANY),
        out_shape=jax.ShapeDtypeStruct((n_rows, d), x1.dtype),
        scratch_shapes=[pltpu.SemaphoreType.DMA(())],
        input_output_aliases={2: 0},
        compiler_params=_params("arbitrary"),
        name="moe_dispatch",
    )(dest_tiles, x1, zeros)


def _expert_kernel(be_ref, x_ref, wgu_ref, wd_ref, y_ref, wgu_bf_ref, wd_bf_ref, *, hidden):
    i = pl.program_id(0)

    @pl.when((i == 0) | (be_ref[i] != be_ref[jnp.maximum(i - 1, 0)]))
    def _():
        wgu_bf_ref[...] = wgu_ref[0].astype(BF16)
        wd_bf_ref[...] = wd_ref[0].astype(BF16)

    n_parts = 2
    part = x_ref.shape[0] // n_parts
    rows = [pl.ds(p * part, part) for p in range(n_parts)]
    hs = [_dot(_unpack_bf16_halves(x_ref[r, :]).astype(BF16), wgu_bf_ref[...]) for r in rows]
    for r, h in zip(rows, hs):
        hg = h[:, :hidden]
        hu = h[:, hidden:]
        act = hg * (1.0 / (1.0 + jnp.exp(-hg))) * hu
        y = _dot(act.astype(BF16), wd_bf_ref[...])
        y_ref[r, :] = _pack_bf16_halves(y.astype(BF16).astype(F32))


def _experts(x_buf, block_e, w_gu, w_down):
    n_rows = x_buf.shape[0]
    hidden, d = w_down.shape[1:]
    tm = MOE_BLOCK
    grid_spec = pltpu.PrefetchScalarGridSpec(
        num_scalar_prefetch=1,
        grid=(n_rows // tm,),
        in_specs=[
            pl.BlockSpec((tm, d // 2), lambda i, be: (i, 0)),
            pl.BlockSpec((1, d, 2 * hidden), lambda i, be: (be[i], 0, 0)),
            pl.BlockSpec((1, hidden, d), lambda i, be: (be[i], 0, 0)),
        ],
        out_specs=pl.BlockSpec((tm, d // 2), lambda i, be: (i, 0)),
        scratch_shapes=[pltpu.VMEM((d, 2 * hidden), BF16), pltpu.VMEM((hidden, d), BF16)],
    )
    return pl.pallas_call(
        functools.partial(_expert_kernel, hidden=hidden),
        grid_spec=grid_spec,
        out_shape=jax.ShapeDtypeStruct((n_rows, d // 2), jnp.uint32),
        compiler_params=_params("arbitrary"),
        name="moe_experts",
    )(block_e, x_buf, w_gu, w_down)


def _combine_kernel(dest_ref, dest_next_ref, y_ref, gate_ref, x_ref, lg_ref, lb_ref, o_ref, rows_ref, sems, *, alpha):
    i = pl.program_id(0)
    last = pl.num_programs(0) - 1
    tm = x_ref.shape[0]
    slot = i % 2

    def copies(d_ref, r, to_slot):
        return [_row_copy(y_ref, d_ref[0, 0, MOE_TOPK * r + s], rows_ref.at[to_slot, s], r, sems.at[to_slot])
                for s in range(MOE_TOPK)]

    def start_all(d_ref, to_slot):
        def body(r, carry):
            for cp in copies(d_ref, r, to_slot):
                cp.start()
            return carry
        lax.fori_loop(0, tm, body, 0, unroll=DMA_UNROLL)

    def wait_all(d_ref, to_slot):
        def body(r, carry):
            for cp in copies(d_ref, r, to_slot):
                cp.wait()
            return carry
        lax.fori_loop(0, tm, body, 0, unroll=DMA_UNROLL)

    @pl.when(i == 0)
    def _():
        start_all(dest_ref, slot)

    wait_all(dest_ref, slot)
    group = tm // COMBINE_GROUPS
    for g in range(COMBINE_GROUPS):
        for r in range(g * group, (g + 1) * group):
            for cp in copies(dest_next_ref, r, 1 - slot):
                cp.start()
        rows = pl.ds(g * group, group)
        gate = gate_ref[rows, :]
        f = sum(gate[:, s:s + 1] * _unpack_bf16_halves(rows_ref[slot, s, rows, :]) for s in range(MOE_TOPK))
        o_ref[rows, :] = _layer_norm_rows(alpha * x_ref[rows, :] + f, lg_ref[...], lb_ref[...])

    @pl.when(i == last)
    def _():
        wait_all(dest_next_ref, 1 - slot)


def _combine(y, dest_tiles, gates, x1, ln_g, ln_b, alpha):
    n, d = x1.shape
    tm = MOE_TOKEN_TILE
    n_tiles = n // tm
    row = lambda i: (i, 0)
    dest_spec = lambda index: pl.BlockSpec((1, 1, MOE_TOPK * tm), index, memory_space=pltpu.SMEM)
    return pl.pallas_call(
        functools.partial(_combine_kernel, alpha=alpha),
        grid=(n_tiles,),
        in_specs=[
            dest_spec(lambda i: (i, 0, 0)),
            dest_spec(lambda i: (jnp.minimum(i + 1, n_tiles - 1), 0, 0)),
            pl.BlockSpec(memory_space=pl.ANY),
            pl.BlockSpec((tm, LANES), row), pl.BlockSpec((tm, d), row), _const_spec((1, d)), _const_spec((1, d)),
        ],
        out_specs=pl.BlockSpec((tm, d), row),
        out_shape=jax.ShapeDtypeStruct((n, d), F32),
        scratch_shapes=[pltpu.VMEM((2, MOE_TOPK, tm, d // 2), jnp.uint32), pltpu.SemaphoreType.DMA((2,))],
        compiler_params=_params("arbitrary"),
        name="moe_combine",
    )(dest_tiles, dest_tiles, y, gates, x1, ln_g.reshape(1, d), ln_b.reshape(1, d))


def _hier_moe_block(x1, x1_packed, logits, w_gu, w_down, layer, ln_g, ln_b, alpha):
    n_tok, d = x1.shape
    idx, gates, counts = _router(logits)
    counts = counts[0, :MOE_EXPERTS]
    padded = (counts + MOE_BLOCK - 1) // MOE_BLOCK * MOE_BLOCK
    pad_end = jnp.cumsum(padded)
    pad_start = pad_end - padded
    expert = idx[:, :MOE_TOPK]
    pos = idx[:, MOE_TOPK:2 * MOE_TOPK]
    e_iota = jnp.arange(MOE_EXPERTS, dtype=jnp.int32)
    base = jnp.sum(jnp.where(expert[:, :, None] == e_iota, pad_start, 0), axis=-1)
    dest = (base + pos).astype(jnp.int32)
    n_blocks = n_tok * MOE_TOPK // MOE_BLOCK + MOE_EXPERTS
    n_rows = n_blocks * MOE_BLOCK
    block_start = jnp.arange(n_blocks, dtype=jnp.int32) * MOE_BLOCK
    block_e = jnp.minimum(jnp.sum((pad_end[None, :] <= block_start[:, None]).astype(jnp.int32), axis=1),
                          MOE_EXPERTS - 1) + layer * MOE_EXPERTS
    dest_tiles = dest.reshape(n_tok // MOE_TOKEN_TILE, 1, MOE_TOPK * MOE_TOKEN_TILE)
    x_buf = _dispatch(x1_packed, dest_tiles, n_rows)
    y = _experts(x_buf, block_e, w_gu, w_down)
    return _combine(y, dest_tiles, gates, x1, ln_g, ln_b, alpha)


def _rwkv_proj_kernel(*refs, seq, has_mix):
    if has_mix:
        (x_ref, xp_ref, mu_ref, wrkv_ref, w0_ref, w1_ref, w2_ref, a0_ref, a1_ref, a2_ref, g1_ref, g2_ref,
         v0_ref, v1_ref, v2_ref, vf_ref, r_ref, k_ref, v_ref, z_ref, a_ref, g_ref) = refs
    else:
        (x_ref, xp_ref, mu_ref, wrkv_ref, w0_ref, w1_ref, w2_ref, a0_ref, a1_ref, a2_ref, g1_ref, g2_ref,
         r_ref, k_ref, v_ref, z_ref, a_ref, g_ref) = refs
    x = x_ref[...]
    tm = x.shape[0]
    row = lax.broadcasted_iota(jnp.int32, x.shape, 0)
    at_seq_start = (pl.program_id(0) * tm) % seq == 0
    prev_last = jnp.where(at_seq_start, 0.0, xp_ref[7:8, :])
    shifted = jnp.where(row == 0, prev_last, pltpu.roll(x, 1, axis=0))
    xx = shifted - x

    def mixed(n):
        return (x + xx * mu_ref[n:n + 1, :]).astype(BF16)

    def sigmoid(t):
        return 1.0 / (1.0 + jnp.exp(-t))

    r_ref[...] = _dot(mixed(0), wrkv_ref[0])
    k_ref[...] = _dot(mixed(1), wrkv_ref[1])
    xv = mixed(2)
    v = _dot(xv, wrkv_ref[2])
    if has_mix:
        mix = sigmoid(v0_ref[...] + _dot(_dot(xv, v1_ref[...]).astype(BF16), v2_ref[...]))
        v = v + (vf_ref[...] - v) * mix
    v_ref[...] = v
    z_ref[...] = w0_ref[...] + _dot(jnp.tanh(_dot(mixed(3), w1_ref[...])).astype(BF16), w2_ref[...])
    a = sigmoid(a0_ref[...] + _dot(_dot(mixed(4), a1_ref[...]).astype(BF16), a2_ref[...]))
    a_ref[...] = a.astype(a_ref.dtype)
    g_ref[...] = _dot(sigmoid(_dot(mixed(5), g1_ref[...])).astype(BF16), g2_ref[...]).astype(g_ref.dtype)


def _pad_cols(w, width):
    return jnp.pad(w, ((0, 0), (0, width - w.shape[1])))


def _pad_rows(w, height):
    return jnp.pad(w, ((0, height - w.shape[0]), (0, 0)))


def _rwkv_proj(x, seq, mu, w_rkv, w0, w1, w2, a0, a1, a2, g1, g2, value_mix, v_first):
    n, d = x.shape
    tm = ROW_TILE
    row = lambda i: (i, 0)
    lora = lambda w_in, w_out, width: (_pad_cols(w_in, width).astype(BF16), _pad_rows(w_out, width).astype(BF16))
    w1p, w2p = lora(w1, w2, LANES)
    a1p, a2p = lora(a1, a2, LANES)
    g1p, g2p = lora(g1, g2, 2 * LANES)
    mu8 = _pad_rows(mu, 8)
    ins = [x, x, mu8, w_rkv.astype(BF16), w0.reshape(1, d), w1p, w2p, a0.reshape(1, d), a1p, a2p, g1p, g2p]
    specs = [
        pl.BlockSpec((tm, d), row),
        pl.BlockSpec((8, d), lambda i: (jnp.maximum(i * (tm // 8) - 1, 0), 0)),
        _const_spec((8, d)), _const_spec((3, d, d)), _const_spec((1, d)),
        _const_spec((d, LANES)), _const_spec((LANES, d)), _const_spec((1, d)),
        _const_spec((d, LANES)), _const_spec((LANES, d)),
        _const_spec((d, 2 * LANES)), _const_spec((2 * LANES, d)),
    ]
    has_mix = value_mix is not None
    if has_mix:
        v0, v1, v2 = value_mix
        v1p, v2p = lora(v1, v2, LANES)
        ins += [v0.reshape(1, d), v1p, v2p, v_first]
        specs += [_const_spec((1, d)), _const_spec((d, LANES)), _const_spec((LANES, d)), pl.BlockSpec((tm, d), row)]
    out = jax.ShapeDtypeStruct((n, d), F32)
    out_bf = jax.ShapeDtypeStruct((n, d), BF16)
    return pl.pallas_call(
        functools.partial(_rwkv_proj_kernel, seq=seq, has_mix=has_mix),
        grid=(n // tm,),
        in_specs=specs,
        out_specs=[pl.BlockSpec((tm, d), row)] * 6,
        out_shape=[out] * 4 + [out_bf] * 2,
        compiler_params=_params("parallel"),
        name="rwkv_proj",
    )(*ins)


def _wkv_kernel(r_ref, k_ref, v_ref, z_ref, a_ref, g_ref, kk_ref, ka_ref, rk_ref, lg_ref, lb_ref, o_ref, s_ref,
                *, tblock, npair):
    c = WKV_CHUNK
    hn = RW_HEAD
    w = 2 * hn
    wt = npair * w

    @pl.when(pl.program_id(2) == 0)
    def _():
        s_ref[...] = jnp.zeros_like(s_ref)

    def iota2(shape, dim):
        return lax.broadcasted_iota(jnp.int32, shape, dim)

    lane_c = iota2((c, w), 1)
    head0 = lane_c < hn
    rw, cw = iota2((w, w), 0), iota2((w, w), 1)
    same_head = (rw // hn) == (cw // hn)
    t_row, s_col = iota2((c, w), 0), lane_c % hn
    strict = s_col < t_row
    incl = s_col <= t_row
    blk16 = (t_row // 16) == (s_col // 16)
    blk32 = (t_row // 32) == (s_col // 32)
    eye = (t_row == s_col).astype(F32)
    r2, c2 = iota2((2 * w, w), 0), iota2((2 * w, w), 1)
    seg_ones2 = (((r2 % w) // hn) == (c2 // hn)).astype(BF16)
    tri3 = (iota2((c, 3 * c), 1) % c <= iota2((c, 3 * c), 0)).astype(BF16)

    def split(t, pieces):
        out = []
        for _ in range(pieces - 1):
            out.append(t.astype(BF16))
            t = t - out[-1].astype(F32)
        return out + [t.astype(BF16)]

    def seg_sum(t):
        return jnp.concatenate(
            [_dot(jnp.concatenate(split(t[:, p * w:(p + 1) * w], 2), axis=1), seg_ones2) for p in range(npair)],
            axis=1)

    def stack(t):
        zero = jnp.zeros_like(t)
        return jnp.concatenate([jnp.where(head0, t, zero), jnp.where(head0, zero, t)], axis=0)

    def bf(t):
        return t.astype(BF16)

    nch = tblock // c
    chunks = range(nch)
    items = [(ci, p) for ci in chunks for p in range(npair)]

    def rows(t, ci):
        return t[ci * c:(ci + 1) * c]

    def sub(t, it):
        ci, p = it
        return t[ci * c:(ci + 1) * c, p * w:(p + 1) * w]

    r, k, v, z, a, g = (ref[0].astype(F32) for ref in (r_ref, k_ref, v_ref, z_ref, a_ref, g_ref))
    kk = k * kk_ref[...]
    kk = kk * jnp.minimum(lax.rsqrt(seg_sum(kk * kk)), 1e12)
    kmod = k * (1.0 + (a - 1.0) * ka_ref[...])
    bvec = kk * a
    lw = (-0.5 * math.exp(-0.5)) * (1.0 + jnp.tanh(0.5 * z))
    cum = jnp.concatenate([_dot(tri3, jnp.concatenate(split(rows(lw, ci), 3), axis=0)) for ci in chunks], axis=0)
    at = -kk * jnp.exp(cum - lw)
    rt = r * jnp.exp(cum)
    inv = jnp.exp(-cum)
    bt = bvec * inv
    kt = kmod * inv
    totals = [cum[(ci + 1) * c - 1:(ci + 1) * c, :] for ci in chunks]
    to_end = jnp.concatenate([jnp.exp(totals[ci] - rows(cum, ci)) for ci in chunks], axis=0)
    bh = bvec * to_end
    kh = kmod * to_end

    n_it = range(len(items))
    atb, rtb = bf(at), bf(rt)
    ast = [stack(sub(atb, it)) for it in items]
    bk = [jnp.concatenate([stack(bf(sub(bt, it))), stack(bf(sub(kt, it)))], axis=0) for it in items]
    prods = [_dot_nt(jnp.concatenate([sub(atb, items[i]), sub(rtb, items[i])], axis=0), bk[i]) for i in n_it]
    l_full = [jnp.where(strict, prods[i][:c, :w], 0.0) for i in n_it]
    a_ak = [bf(jnp.where(strict, prods[i][:c, w:], 0.0)) for i in n_it]
    a_rb = [bf(jnp.where(incl, prods[i][c:, :w], 0.0)) for i in n_it]
    a_rk = [bf(jnp.where(incl, prods[i][c:, w:], 0.0)) for i in n_it]
    vst = [stack(bf(sub(v, it))) for it in items]
    x_loc = [_dot(a_ak[i], vst[i]) for i in n_it]
    y_loc = [_dot(a_rk[i], vst[i]) for i in n_it]

    xf = [bf(jnp.where(blk16, l_full[i], 0.0)) for i in n_it]
    xs = [stack(xf[i]) for i in n_it]
    tm = [eye + jnp.where(blk16, l_full[i], 0.0) for i in n_it]
    for _ in range(3):
        xf = [bf(_dot(xf[i], xs[i])) for i in n_it]
        xs = [stack(xf[i]) for i in n_it]
        tm = [tm[i] + _dot(bf(tm[i]), xs[i]) for i in n_it]
    for inside, outside in ((blk32, blk16), (None, blk32)):
        keep = jnp.logical_not(outside) if inside is None else inside & jnp.logical_not(outside)
        off = [stack(bf(jnp.where(keep, l_full[i], 0.0))) for i in n_it]
        tmb = [bf(tm[i]) for i in n_it]
        half = [bf(_dot(tmb[i], off[i])) for i in n_it]
        tm = [tm[i] + _dot(half[i], stack(tmb[i])) for i in n_it]
    t_fold = [bf(tm[i]) for i in n_it]

    au = [_dot(t_fold[i], jnp.concatenate([ast[i], bf(stack(x_loc[i]))], axis=1)) for i in n_it]
    ah = [au[i][:, :w] for i in n_it]
    ul = [au[i][:, w:] for i in n_it]
    ry = [_dot(a_rb[i], jnp.concatenate([bf(stack(ah[i])), bf(stack(ul[i]))], axis=1)) for i in n_it]
    rh = [bf(sub(rt, items[i]) + ry[i][:, :w]) for i in n_it]
    yl = [y_loc[i] + ry[i][:, w:] for i in n_it]
    w_mat = [bf(jnp.where(same_head, _dot_tn(bf(ah[i]), bf(sub(bh, items[i]))), 0.0)) for i in n_it]
    g_mat = [jnp.where(same_head,
                       _dot_tn(bf(jnp.concatenate([ul[i], sub(v, items[i])], axis=0)),
                               bf(jnp.concatenate([sub(bh, items[i]), sub(kh, items[i])], axis=0))), 0.0)
             for i in n_it]
    decay_c = [jnp.exp(totals[ci][:, p * w:(p + 1) * w]) for ci, p in items]

    s = [s_ref[p] for p in range(npair)]
    ys = [[None] * npair for _ in chunks]
    for i, (ci, p) in enumerate(items):
        sb = bf(s[p])
        ys[ci][p] = _dot_nt(rh[i], sb) + yl[i]
        s[p] = s[p] * decay_c[i] + _dot(sb, w_mat[i]) + g_mat[i]
    for p in range(npair):
        s_ref[p] = s[p]

    y = jnp.concatenate([jnp.concatenate(ys[ci], axis=1) if npair > 1 else ys[ci][0] for ci in chunks], axis=0)
    mean = seg_sum(y) * (1.0 / hn)
    yc = y - mean
    var = seg_sum(yc * yc) * (1.0 / hn)
    yn = yc * lax.rsqrt(var + RW_GN_EPS) * lg_ref[...] + lb_ref[...]
    yn = yn + seg_sum(r * kmod * rk_ref[...]) * v
    o_ref[0] = (yn * g).astype(o_ref.dtype)


def _wkv(r, k, v, z, a, g, k_k, k_a, r_k, lnx_g, lnx_b, batch, seq):
    n, d = r.shape
    w = 2 * RW_HEAD
    npair = WKV_PAIRS
    wt = npair * w
    tb = WKV_TBLOCK
    seq_spec = pl.BlockSpec((1, tb, wt), lambda b, h, t: (b, t, h))
    par_spec = pl.BlockSpec((1, wt), lambda b, h, t: (0, h))
    r3 = lambda t: t.reshape(batch, seq, d)
    out = pl.pallas_call(
        functools.partial(_wkv_kernel, tblock=tb, npair=npair),
        grid=(batch, d // wt, seq // tb),
        in_specs=[seq_spec] * 6 + [par_spec] * 5,
        out_specs=seq_spec,
        out_shape=jax.ShapeDtypeStruct((batch, seq, d), BF16),
        scratch_shapes=[pltpu.VMEM((npair, w, w), F32)],
        compiler_params=_params("parallel", "parallel", "arbitrary"),
        name="wkv7_chunked",
    )(r3(r), r3(k), r3(v), r3(z), r3(a), r3(g), k_k.reshape(1, d), k_a.reshape(1, d), r_k.reshape(1, d),
      lnx_g.reshape(1, d), lnx_b.reshape(1, d))
    return out.reshape(n, d)


def kernel(x, ln1_g, ln1_b, ln2_g, ln2_b, attn_w_qkv, attn_w_o, attn_lambda, attn_subln_g, rw_mu, rw_w_rkv, rw_w_o, rw_w0, rw_w1, rw_w2, rw_a0, rw_a1, rw_a2, rw_g1, rw_g2, rw_k_k, rw_k_a, rw_r_k, rw_lnx_g, rw_lnx_b, rw_v0, rw_v1, rw_v2, moe_rg_w, moe_rg_b, moe_re_w, moe_re_b, moe_w_gu, moe_w_down):
    batch, seq, d = x.shape
    depth = ln1_g.shape[0]
    n = batch * seq
    alpha = (2 * depth) ** 0.25
    x = x.reshape(n, d)
    v_first = None
    for i in range(depth):
        j = i // N_MIXERS
        if i % N_MIXERS == 0:
            lambda_init = 0.8 - 0.6 * math.exp(-0.3 * i)
            lam = attn_lambda[j]
            lam_full = jnp.exp(jnp.sum(lam[0] * lam[1])) - jnp.exp(jnp.sum(lam[2] * lam[3])) + lambda_init
            lam_row = jnp.full((1, 2 * DA_HEAD_DIM), lam_full, F32)
            q_scale = DA_HEAD_DIM ** -0.5 * math.log2(math.e)
            col_scale = jnp.concatenate([jnp.full((d,), q_scale, F32), jnp.ones((2 * d,), F32)])
            w_qkv = (attn_w_qkv[j] * col_scale[None, :]).astype(BF16)
            qkv = _proj(x, w_qkv, BF16)
            mixed = _diff_attention(qkv, lam_row, attn_subln_g[j], lambda_init, batch, seq)
            w_o = attn_w_o[j]
        else:
            value_mix = None if j == 0 else (rw_v0[j - 1], rw_v1[j - 1], rw_v2[j - 1])
            r, k, v, z, a, g = _rwkv_proj(x, seq, rw_mu[j], rw_w_rkv[j], rw_w0[j], rw_w1[j], rw_w2[j], rw_a0[j],
                                          rw_a1[j], rw_a2[j], rw_g1[j], rw_g2[j], value_mix, v_first)
            if value_mix is None:
                v_first = v
            mixed = _wkv(r, k, v, z, a, g, rw_k_k[j], rw_k_a[j], rw_r_k[j].reshape(d), rw_lnx_g[j], rw_lnx_b[j],
                         batch, seq)
            w_o = rw_w_o[j]
        n_router = MOE_GROUPS + MOE_EXPERTS
        w_router = _pad_cols(jnp.concatenate([moe_rg_w[i], moe_re_w[i]], axis=1), LANES)
        wr_hi = w_router.astype(BF16)
        wr_lo = (w_router - wr_hi.astype(F32)).astype(BF16)
        w_router = jnp.concatenate([wr_hi, wr_hi, wr_lo], axis=0)
        b_router = _pad_cols(jnp.concatenate([moe_rg_b[i], moe_re_b[i]]).reshape(1, n_router), LANES)
        x1, x1_packed, logits = _post_mixer(mixed, x, w_o.astype(BF16), ln1_g[i], ln1_b[i], w_router, b_router,
                                            alpha)
        x = _hier_moe_block(x1, x1_packed, logits, moe_w_gu.reshape((-1,) + moe_w_gu.shape[2:]),
                            moe_w_down.reshape((-1,) + moe_w_down.shape[2:]), i, ln2_g[i], ln2_b[i], alpha)
    return x.reshape(batch, seq, d)
```

```python
import functools
import math

import jax
import jax.numpy as jnp
from jax import lax
from jax.experimental import pallas as pl
from jax.experimental.pallas import tpu as pltpu

F32 = jnp.float32
BF16 = jnp.bfloat16

LANES = 128
VMEM_LIMIT = 56 * 1024 * 1024

DA_HEADS = 8
DA_HEAD_DIM = 64
RMS_EPS = 1e-5
RW_HEAD = 64
RW_GN_EPS = 64e-5
MOE_GROUPS = 4
MOE_EPG = 8
MOE_EXPERTS = MOE_GROUPS * MOE_EPG
MOE_TOPK = 2
LN_EPS = 1e-5
N_MIXERS = 2

ROW_TILE = 256
ROUTER_TILE = 512
MOE_TOKEN_TILE = 512
ATTN_TQ = 512
ATTN_TK = 512
WKV_CHUNK = 64
WKV_TBLOCK = 512
WKV_PAIRS = 4
MOE_BLOCK = 512
DMA_UNROLL = 8
COMBINE_GROUPS = 8
NEG_BIG = -1e30


def _dot(a, b, precision=None):
    return jnp.dot(a, b, preferred_element_type=F32, precision=precision)


def _dot_nt(a, b):
    return lax.dot_general(a, b, (((1,), (1,)), ((), ())), preferred_element_type=F32)


def _dot_tn(a, b):
    return lax.dot_general(a, b, (((0,), (0,)), ((), ())), preferred_element_type=F32)


def _params(*sem):
    return pltpu.CompilerParams(dimension_semantics=sem, vmem_limit_bytes=VMEM_LIMIT)


def _const_spec(shape):
    nd = len(shape)
    return pl.BlockSpec(shape, lambda *_: (0,) * nd)


def _proj_kernel(x_ref, w_ref, o_ref):
    o_ref[...] = _dot(x_ref[...].astype(BF16), w_ref[...]).astype(o_ref.dtype)


def _proj(x, w, out_dtype):
    n, d = x.shape
    m = w.shape[1]
    tm = 512
    return pl.pallas_call(
        _proj_kernel,
        grid=(n // tm,),
        in_specs=[pl.BlockSpec((tm, d), lambda i: (i, 0)), _const_spec((d, m))],
        out_specs=pl.BlockSpec((tm, m), lambda i: (i, 0)),
        out_shape=jax.ShapeDtypeStruct((n, m), out_dtype),
        compiler_params=_params("parallel"),
        name="qkv_proj",
    )(x, w)


def _attn_kernel(lam_ref, g_ref, q_ref, k_ref, v_ref, o_ref, vx_ref, m_ref, acc_ref, sa_ref, sb_ref, *, tq, tk,
                 out_scale):
    qi = pl.program_id(2)
    hd = DA_HEAD_DIM
    hw = 2 * hd

    @pl.when(qi == 0)
    def _():
        vx_ref[:, :hw] = v_ref[0]
        vx_ref[:, hw:] = jnp.ones((vx_ref.shape[0], hw), BF16)

    q = q_ref[0]
    lane = lax.broadcasted_iota(jnp.int32, q.shape, 1)
    zero = jnp.zeros_like(q)
    q_stack = jnp.concatenate([jnp.where(lane < hd, q, zero), jnp.where(lane >= hd, q, zero)], axis=0)
    m_ref[...] = jnp.full(m_ref.shape, NEG_BIG, F32)
    acc_ref[...] = jnp.zeros(acc_ref.shape, F32)
    row0 = qi * tq

    def scores(j, s_ref):
        start = pl.multiple_of(j * tk, tk)
        s_ref[...] = _dot_nt(q_stack, k_ref[0, pl.ds(start, tk), :])

    def consume(j, s_ref, masked):
        start = pl.multiple_of(j * tk, tk)
        vb = vx_ref[pl.ds(start, tk), :]
        if masked:
            row = row0 + lax.broadcasted_iota(jnp.int32, (tq, tk), 0)
            col = start + lax.broadcasted_iota(jnp.int32, (tq, tk), 1)
            keep = col <= row
        for c in range(2):
            s = s_ref[c * tq:(c + 1) * tq, :]
            if masked:
                s = jnp.where(keep, s, NEG_BIG)
            chunks = [s[:, i * LANES:(i + 1) * LANES] for i in range(tk // LANES)]
            m_old = m_ref[c]
            m_new = jnp.maximum(m_old, jnp.max(functools.reduce(jnp.maximum, chunks), axis=-1, keepdims=True))
            alpha = jnp.exp2(m_old - m_new)
            m_ref[c] = m_new
            p = jnp.concatenate([jnp.exp2(ch - m_new).astype(BF16) for ch in chunks], axis=1)
            acc_ref[c] = acc_ref[c] * jnp.concatenate([alpha, alpha], axis=1) + _dot(p, vb)

    n = row0 // tk + 1
    n_pairs = (n - 1) // 2
    scores(0, sa_ref)

    def pair(jj, carry):
        j = 2 * jj
        scores(j + 1, sb_ref)
        consume(j, sa_ref, False)
        scores(j + 2, sa_ref)
        consume(j + 1, sb_ref, False)
        return carry

    lax.fori_loop(0, n_pairs, pair, 0)
    jb = 2 * n_pairs

    @pl.when(n % 2 == 1)
    def _():
        consume(jb, sa_ref, True)

    @pl.when(n % 2 == 0)
    def _():
        scores(jb + 1, sb_ref)
        consume(jb, sa_ref, False)
        consume(jb + 1, sb_ref, True)

    a0, a1 = acc_ref[0], acc_ref[1]
    o = a0[:, :hw] * (1.0 / a0[:, hw:]) - lam_ref[...] * (a1[:, :hw] * (1.0 / a1[:, hw:]))
    ms = jnp.mean(o * o, axis=-1, keepdims=True)
    o = o * lax.rsqrt(ms + RMS_EPS) * g_ref[...] * out_scale
    o_ref[0] = o.astype(o_ref.dtype)


def _diff_attention(qkv, lam_row, subln_g, lambda_init, batch, seq):
    d = DA_HEADS * 2 * DA_HEAD_DIM
    qkv = qkv.reshape(batch, seq, 3 * d)
    tq, tk = ATTN_TQ, ATTN_TK
    assert tq <= tk and tk % tq == 0 and seq % tk == 0
    hw = 2 * DA_HEAD_DIM
    out = pl.pallas_call(
        functools.partial(_attn_kernel, tq=tq, tk=tk, out_scale=1.0 - lambda_init),
        grid=(batch, DA_HEADS, seq // tq),
        in_specs=[
            _const_spec((1, hw)),
            _const_spec((1, hw)),
            pl.BlockSpec((1, tq, hw), lambda b, h, i: (b, i, h)),
            pl.BlockSpec((1, seq, hw), lambda b, h, i: (b, 0, DA_HEADS + h)),
            pl.BlockSpec((1, seq, hw), lambda b, h, i: (b, 0, 2 * DA_HEADS + h)),
        ],
        out_specs=pl.BlockSpec((1, tq, hw), lambda b, h, i: (b, i, h)),
        out_shape=jax.ShapeDtypeStruct((batch, seq, d), BF16),
        scratch_shapes=[pltpu.VMEM((seq, 2 * hw), BF16), pltpu.VMEM((2, tq, hw), F32),
                        pltpu.VMEM((2, tq, 2 * hw), F32), pltpu.VMEM((2 * tq, tk), F32),
                        pltpu.VMEM((2 * tq, tk), F32)],
        compiler_params=_params("parallel", "parallel", "arbitrary"),
        name="diff_attn",
    )(lam_row, subln_g.reshape(1, hw), qkv, qkv, qkv)
    return out.reshape(batch * seq, d)


def _layer_norm_rows(z, g, b):
    mu = jnp.mean(z, axis=-1, keepdims=True)
    zc = z - mu
    var = jnp.mean(zc * zc, axis=-1, keepdims=True)
    return zc * lax.rsqrt(var + LN_EPS) * g + b


HIGH16 = 0xFFFF0000


def _pack_bf16_halves(x_hi):
    half = x_hi.shape[1] // 2
    bits = lax.bitcast_convert_type(x_hi, jnp.uint32)
    return (bits[:, :half] >> 16) | (bits[:, half:] & jnp.uint32(HIGH16))


def _unpack_bf16_halves(packed):
    lo = lax.bitcast_convert_type(packed << 16, F32)
    hi = lax.bitcast_convert_type(packed & jnp.uint32(HIGH16), F32)
    return jnp.concatenate([lo, hi], axis=1)


def _post_mixer_kernel(o_ref, x_ref, w_ref, lg_ref, lb_ref, wr_ref, br_ref, x1_ref, x1p_ref, lo_ref, *, alpha):
    h = _dot(o_ref[...], w_ref[...])
    x1 = _layer_norm_rows(alpha * x_ref[...] + h, lg_ref[...], lb_ref[...])
    x1_ref[...] = x1
    x_hi = x1.astype(BF16)
    x_hi32 = x_hi.astype(F32)
    x_lo = (x1 - x_hi32).astype(BF16)
    x1p_ref[...] = _pack_bf16_halves(x_hi32)
    lo_ref[...] = _dot(jnp.concatenate([x_hi, x_lo, x_hi], axis=1), wr_ref[...]) + br_ref[...]


def _post_mixer(o, x, w_o, ln_g, ln_b, w_router, b_router, alpha):
    n, d = x.shape
    tm = ROW_TILE
    row = lambda i: (i, 0)
    return pl.pallas_call(
        functools.partial(_post_mixer_kernel, alpha=alpha),
        grid=(n // tm,),
        in_specs=[
            pl.BlockSpec((tm, d), row), pl.BlockSpec((tm, d), row), _const_spec((d, d)),
            _const_spec((1, d)), _const_spec((1, d)), _const_spec((3 * d, LANES)), _const_spec((1, LANES)),
        ],
        out_specs=[pl.BlockSpec((tm, d), row), pl.BlockSpec((tm, d // 2), row), pl.BlockSpec((tm, LANES), row)],
        out_shape=[jax.ShapeDtypeStruct((n, d), F32), jax.ShapeDtypeStruct((n, d // 2), jnp.uint32),
                   jax.ShapeDtypeStruct((n, LANES), F32)],
        compiler_params=_params("parallel"),
        name="post_mixer",
    )(o, x, w_o, ln_g.reshape(1, d), ln_b.reshape(1, d), w_router, b_router)


def _router_kernel(lo_ref, idx_ref, gate_ref, cnt_ref, carry_ref):
    step = pl.program_id(0)

    @pl.when(step == 0)
    def _():
        carry_ref[...] = jnp.zeros_like(carry_ref)

    lo = lo_ref[...]
    tm = lo.shape[0]
    lane = lax.broadcasted_iota(jnp.int32, lo.shape, 1)
    big = jnp.int32(LANES)

    def softmax_over(mask):
        mx = jnp.max(jnp.where(mask, lo, NEG_BIG), axis=-1, keepdims=True)
        ex = jnp.where(mask, jnp.exp(lo - mx), 0.0)
        return ex / jnp.sum(ex, axis=-1, keepdims=True)

    def top1(p, mask):
        best = jnp.max(jnp.where(mask, p, -1.0), axis=-1, keepdims=True)
        arg = jnp.min(jnp.where(mask & (p == best), lane, big), axis=-1, keepdims=True)
        return best, arg

    g_mask = lane < MOE_GROUPS
    g_p, g_idx = top1(softmax_over(g_mask), g_mask)
    first = MOE_GROUPS + MOE_EPG * g_idx
    e_mask = (lane >= first) & (lane < first + MOE_EPG)
    e_prob = softmax_over(e_mask)
    p1, i1 = top1(e_prob, e_mask)
    rest = e_mask & (lane != i1)
    p2, i2 = top1(e_prob, rest)
    denom = p1 + p2
    gate1 = g_p * (p1 / denom)
    gate2 = g_p * (p2 / denom)
    e1 = i1 - MOE_GROUPS
    e2 = i2 - MOE_GROUPS

    oh1 = (lane == e1).astype(F32)
    oh2 = (lane == e2).astype(F32)
    both = oh1 + oh2
    r_i = lax.broadcasted_iota(jnp.int32, (tm, tm), 0)
    c_i = lax.broadcasted_iota(jnp.int32, (tm, tm), 1)
    before = (c_i < r_i).astype(BF16)
    seen = _dot(before, both.astype(BF16)) + carry_ref[...]
    pos1 = jnp.sum(seen * oh1, axis=-1, keepdims=True).astype(jnp.int32)
    pos2 = jnp.sum(seen * oh2, axis=-1, keepdims=True).astype(jnp.int32)
    carry_ref[...] += jnp.sum(both, axis=0, keepdims=True)
    cnt_ref[...] = carry_ref[...].astype(jnp.int32)

    zero_i = jnp.zeros(lo.shape, jnp.int32)
    idx_ref[...] = (jnp.where(lane == 0, e1, zero_i) + jnp.where(lane == 1, e2, zero_i)
                    + jnp.where(lane == 2, pos1, zero_i) + jnp.where(lane == 3, pos2, zero_i))
    gate_ref[...] = jnp.where(lane == 0, gate1, 0.0) + jnp.where(lane == 1, gate2, 0.0)


def _router(logits):
    n = logits.shape[0]
    tm = ROUTER_TILE
    row = lambda i: (i, 0)
    return pl.pallas_call(
        _router_kernel,
        grid=(n // tm,),
        in_specs=[pl.BlockSpec((tm, LANES), row)],
        out_specs=[pl.BlockSpec((tm, LANES), row), pl.BlockSpec((tm, LANES), row), _const_spec((1, LANES))],
        out_shape=[jax.ShapeDtypeStruct((n, LANES), jnp.int32), jax.ShapeDtypeStruct((n, LANES), F32),
                   jax.ShapeDtypeStruct((1, LANES), jnp.int32)],
        scratch_shapes=[pltpu.VMEM((1, LANES), F32)],
        compiler_params=_params("arbitrary"),
        name="moe_router",
    )(logits)


def _row_copy(src_ref, src_row, dst_ref, dst_row, sem):
    return pltpu.make_async_copy(src_ref.at[pl.ds(src_row, 1)], dst_ref.at[pl.ds(dst_row, 1)], sem)


def _dispatch_kernel(dest_ref, x_ref, buf_in_ref, buf_ref, sem):
    del buf_in_ref
    tm = x_ref.shape[0]

    def copies(r):
        return [_row_copy(x_ref, r, buf_ref, dest_ref[0, 0, MOE_TOPK * r + s], sem) for s in range(MOE_TOPK)]

    def start(r, carry):
        for cp in copies(r):
            cp.start()
        return carry

    def wait(r, carry):
        for cp in copies(r):
            cp.wait()
        return carry

    lax.fori_loop(0, tm, start, 0, unroll=DMA_UNROLL)
    lax.fori_loop(0, tm, wait, 0, unroll=DMA_UNROLL)


def _dispatch(x1, dest_tiles, n_rows):
    n, d = x1.shape
    tm = MOE_TOKEN_TILE
    zeros = jnp.zeros((n_rows, d), x1.dtype)
    return pl.pallas_call(
        _dispatch_kernel,
        grid=(n // tm,),
        in_specs=[
            pl.BlockSpec((1, 1, MOE_TOPK * tm), lambda i: (i, 0, 0), memory_space=pltpu.SMEM),
            pl.BlockSpec((tm, d), lambda i: (i, 0)),
            pl.BlockSpec(memory_space=pl.ANY),
        ],
        out_specs=pl.BlockSpec(memory_space=pl.ANY),
        out_shape=jax.ShapeDtypeStruct((n_rows, d), x1.dtype),
        scratch_shapes=[pltpu.SemaphoreType.DMA(())],
        input_output_aliases={2: 0},
        compiler_params=_params("arbitrary"),
        name="moe_dispatch",
    )(dest_tiles, x1, zeros)


def _expert_kernel(be_ref, used_ref, x_ref, wgu_ref, wd_ref, y_ref, wgu_bf_ref, wd_bf_ref, *, hidden):
    i = pl.program_id(0)
    in_use = i < used_ref[0]

    @pl.when(in_use & ((i == 0) | (be_ref[i] != be_ref[jnp.maximum(i - 1, 0)])))
    def _():
        wgu_bf_ref[...] = wgu_ref[0].astype(BF16)
        wd_bf_ref[...] = wd_ref[0].astype(BF16)

    @pl.when(in_use)
    def _():
        n_parts = 2
        part = x_ref.shape[0] // n_parts
        rows = [pl.ds(p * part, part) for p in range(n_parts)]
        hs = [_dot(_unpack_bf16_halves(x_ref[r, :]).astype(BF16), wgu_bf_ref[...]) for r in rows]
        for r, h in zip(rows, hs):
            hg = h[:, :hidden]
            hu = h[:, hidden:]
            act = hg * (1.0 / (1.0 + jnp.exp(-hg))) * hu
            y_ref[r, :] = _dot(act.astype(BF16), wd_bf_ref[...])

    @pl.when(jnp.logical_not(in_use))
    def _():
        y_ref[...] = jnp.zeros_like(y_ref)


def _experts(x_buf, block_e, n_used, w_gu, w_down):
    n_rows = x_buf.shape[0]
    hidden, d = w_down.shape[1:]
    tm = MOE_BLOCK

    def live(i, used):
        return jnp.minimum(i, used[0] - 1)

    grid_spec = pltpu.PrefetchScalarGridSpec(
        num_scalar_prefetch=2,
        grid=(n_rows // tm,),
        in_specs=[
            pl.BlockSpec((tm, d // 2), lambda i, be, used: (live(i, used), 0)),
            pl.BlockSpec((1, d, 2 * hidden), lambda i, be, used: (be[live(i, used)], 0, 0)),
            pl.BlockSpec((1, hidden, d), lambda i, be, used: (be[live(i, used)], 0, 0)),
        ],
        out_specs=pl.BlockSpec((tm, d), lambda i, be, used: (i, 0)),
        scratch_shapes=[pltpu.VMEM((d, 2 * hidden), BF16), pltpu.VMEM((hidden, d), BF16)],
    )
    return pl.pallas_call(
        functools.partial(_expert_kernel, hidden=hidden),
        grid_spec=grid_spec,
        out_shape=jax.ShapeDtypeStruct((n_rows, d), F32),
        compiler_params=_params("arbitrary"),
        name="moe_experts",
    )(block_e, n_used, x_buf, w_gu, w_down)


def _combine_kernel(dest_ref, dest_next_ref, y_ref, gate_ref, x_ref, lg_ref, lb_ref, o_ref, rows_ref, sems, *, alpha):
    i = pl.program_id(0)
    last = pl.num_programs(0) - 1
    tm = x_ref.shape[0]
    slot = i % 2

    def copies(d_ref, r, to_slot):
        return [_row_copy(y_ref, d_ref[0, 0, MOE_TOPK * r + s], rows_ref.at[to_slot, s], r, sems.at[to_slot])
                for s in range(MOE_TOPK)]

    def start_all(d_ref, to_slot):
        def body(r, carry):
            for cp in copies(d_ref, r, to_slot):
                cp.start()
            return carry
        lax.fori_loop(0, tm, body, 0, unroll=DMA_UNROLL)

    def wait_all(d_ref, to_slot):
        def body(r, carry):
            for cp in copies(d_ref, r, to_slot):
                cp.wait()
            return carry
        lax.fori_loop(0, tm, body, 0, unroll=DMA_UNROLL)

    @pl.when(i == 0)
    def _():
        start_all(dest_ref, slot)

    wait_all(dest_ref, slot)
    group = tm // COMBINE_GROUPS
    for g in range(COMBINE_GROUPS):
        for r in range(g * group, (g + 1) * group):
            for cp in copies(dest_next_ref, r, 1 - slot):
                cp.start()
        rows = pl.ds(g * group, group)
        gate = gate_ref[rows, :]
        f = sum(gate[:, s:s + 1] * rows_ref[slot, s, rows, :] for s in range(MOE_TOPK))
        o_ref[rows, :] = _layer_norm_rows(alpha * x_ref[rows, :] + f, lg_ref[...], lb_ref[...])

    @pl.when(i == last)
    def _():
        wait_all(dest_next_ref, 1 - slot)


def _combine(y, dest_tiles, gates, x1, ln_g, ln_b, alpha):
    n, d = x1.shape
    tm = MOE_TOKEN_TILE
    n_tiles = n // tm
    row = lambda i: (i, 0)
    dest_spec = lambda index: pl.BlockSpec((1, 1, MOE_TOPK * tm), index, memory_space=pltpu.SMEM)
    return pl.pallas_call(
        functools.partial(_combine_kernel, alpha=alpha),
        grid=(n_tiles,),
        in_specs=[
            dest_spec(lambda i: (i, 0, 0)),
            dest_spec(lambda i: (jnp.minimum(i + 1, n_tiles - 1), 0, 0)),
            pl.BlockSpec(memory_space=pl.ANY),
            pl.BlockSpec((tm, LANES), row), pl.BlockSpec((tm, d), row), _const_spec((1, d)), _const_spec((1, d)),
        ],
        out_specs=pl.BlockSpec((tm, d), row),
        out_shape=jax.ShapeDtypeStruct((n, d), F32),
        scratch_shapes=[pltpu.VMEM((2, MOE_TOPK, tm, d), F32), pltpu.SemaphoreType.DMA((2,))],
        compiler_params=_params("arbitrary"),
        name="moe_combine",
    )(dest_tiles, dest_tiles, y, gates, x1, ln_g.reshape(1, d), ln_b.reshape(1, d))


def _hier_moe_block(x1, x1_packed, logits, w_gu, w_down, layer, ln_g, ln_b, alpha):
    n_tok, d = x1.shape
    idx, gates, counts = _router(logits)
    counts = counts[0, :MOE_EXPERTS]
    padded = (counts + MOE_BLOCK - 1) // MOE_BLOCK * MOE_BLOCK
    pad_end = jnp.cumsum(padded)
    pad_start = pad_end - padded
    expert = idx[:, :MOE_TOPK]
    pos = idx[:, MOE_TOPK:2 * MOE_TOPK]
    e_iota = jnp.arange(MOE_EXPERTS, dtype=jnp.int32)
    base = jnp.sum(jnp.where(expert[:, :, None] == e_iota, pad_start, 0), axis=-1)
    dest = (base + pos).astype(jnp.int32)
    n_blocks = n_tok * MOE_TOPK // MOE_BLOCK + MOE_EXPERTS
    n_rows = n_blocks * MOE_BLOCK
    block_start = jnp.arange(n_blocks, dtype=jnp.int32) * MOE_BLOCK
    block_e = jnp.minimum(jnp.sum((pad_end[None, :] <= block_start[:, None]).astype(jnp.int32), axis=1),
                          MOE_EXPERTS - 1) + layer * MOE_EXPERTS
    dest_tiles = dest.reshape(n_tok // MOE_TOKEN_TILE, 1, MOE_TOPK * MOE_TOKEN_TILE)
    x_buf = _dispatch(x1_packed, dest_tiles, n_rows)
    n_used = (pad_end[MOE_EXPERTS - 1:] // MOE_BLOCK).astype(jnp.int32)
    y = _experts(x_buf, block_e, n_used, w_gu, w_down)
    return _combine(y, dest_tiles, gates, x1, ln_g, ln_b, alpha)


def _rwkv_proj_kernel(*refs, seq, has_mix):
    if has_mix:
        (x_ref, xp_ref, mu_ref, wrkv_ref, w0_ref, w1_ref, w2_ref, a0_ref, a1_ref, a2_ref, g1_ref, g2_ref,
         v0_ref, v1_ref, v2_ref, vf_ref, r_ref, k_ref, v_ref, z_ref, a_ref, g_ref) = refs
    else:
        (x_ref, xp_ref, mu_ref, wrkv_ref, w0_ref, w1_ref, w2_ref, a0_ref, a1_ref, a2_ref, g1_ref, g2_ref,
         r_ref, k_ref, v_ref, z_ref, a_ref, g_ref) = refs
    x = x_ref[...]
    tm = x.shape[0]
    row = lax.broadcasted_iota(jnp.int32, x.shape, 0)
    at_seq_start = (pl.program_id(0) * tm) % seq == 0
    prev_last = jnp.where(at_seq_start, 0.0, xp_ref[7:8, :])
    shifted = jnp.where(row == 0, prev_last, pltpu.roll(x, 1, axis=0))
    xx = shifted - x

    def mixed(n):
        return (x + xx * mu_ref[n:n + 1, :]).astype(BF16)

    def sigmoid(t):
        return 0.5 * (1.0 + jnp.tanh(0.5 * t))

    r_ref[...] = _dot(mixed(0), wrkv_ref[0])
    k_ref[...] = _dot(mixed(1), wrkv_ref[1])
    xv = mixed(2)
    v = _dot(xv, wrkv_ref[2])
    if has_mix:
        mix = sigmoid(v0_ref[...] + _dot(_dot(xv, v1_ref[...]).astype(BF16), v2_ref[...]))
        v = v + (vf_ref[...] - v) * mix
    v_ref[...] = v
    z_ref[...] = w0_ref[...] + _dot(jnp.tanh(_dot(mixed(3), w1_ref[...])).astype(BF16), w2_ref[...])
    a_ref[...] = sigmoid(a0_ref[...] + _dot(_dot(mixed(4), a1_ref[...]).astype(BF16), a2_ref[...]))
    g_ref[...] = _dot(sigmoid(_dot(mixed(5), g1_ref[...])).astype(BF16), g2_ref[...])


def _pad_cols(w, width):
    return jnp.pad(w, ((0, 0), (0, width - w.shape[1])))


def _pad_rows(w, height):
    return jnp.pad(w, ((0, height - w.shape[0]), (0, 0)))


def _rwkv_proj(x, seq, mu, w_rkv, w0, w1, w2, a0, a1, a2, g1, g2, value_mix, v_first):
    n, d = x.shape
    tm = ROW_TILE
    row = lambda i: (i, 0)
    lora = lambda w_in, w_out, width: (_pad_cols(w_in, width).astype(BF16), _pad_rows(w_out, width).astype(BF16))
    w1p, w2p = lora(w1, w2, LANES)
    a1p, a2p = lora(a1, a2, LANES)
    g1p, g2p = lora(g1, g2, 2 * LANES)
    mu8 = _pad_rows(mu, 8)
    ins = [x, x, mu8, w_rkv.astype(BF16), w0.reshape(1, d), w1p, w2p, a0.reshape(1, d), a1p, a2p, g1p, g2p]
    specs = [
        pl.BlockSpec((tm, d), row),
        pl.BlockSpec((8, d), lambda i: (jnp.maximum(i * (tm // 8) - 1, 0), 0)),
        _const_spec((8, d)), _const_spec((3, d, d)), _const_spec((1, d)),
        _const_spec((d, LANES)), _const_spec((LANES, d)), _const_spec((1, d)),
        _const_spec((d, LANES)), _const_spec((LANES, d)),
        _const_spec((d, 2 * LANES)), _const_spec((2 * LANES, d)),
    ]
    has_mix = value_mix is not None
    if has_mix:
        v0, v1, v2 = value_mix
        v1p, v2p = lora(v1, v2, LANES)
        ins += [v0.reshape(1, d), v1p, v2p, v_first]
        specs += [_const_spec((1, d)), _const_spec((d, LANES)), _const_spec((LANES, d)), pl.BlockSpec((tm, d), row)]
    out = jax.ShapeDtypeStruct((n, d), F32)
    return pl.pallas_call(
        functools.partial(_rwkv_proj_kernel, seq=seq, has_mix=has_mix),
        grid=(n // tm,),
        in_specs=specs,
        out_specs=[pl.BlockSpec((tm, d), row)] * 6,
        out_shape=[out] * 6,
        compiler_params=_params("parallel"),
        name="rwkv_proj",
    )(*ins)


def _wkv_kernel(r_ref, k_ref, v_ref, z_ref, a_ref, g_ref, kk_ref, ka_ref, rk_ref, lg_ref, lb_ref, o_ref, s_ref,
                *, tblock, npair):
    c = WKV_CHUNK
    hn = RW_HEAD
    w = 2 * hn
    wt = npair * w

    @pl.when(pl.program_id(2) == 0)
    def _():
        s_ref[...] = jnp.zeros_like(s_ref)

    def iota2(shape, dim):
        return lax.broadcasted_iota(jnp.int32, shape, dim)

    lane_c = iota2((c, w), 1)
    head0 = lane_c < hn
    rw, cw = iota2((w, w), 0), iota2((w, w), 1)
    same_head = (rw // hn) == (cw // hn)
    t_row, s_col = iota2((c, w), 0), lane_c % hn
    strict = s_col < t_row
    incl = s_col <= t_row
    blk16 = (t_row // 16) == (s_col // 16)
    blk32 = (t_row // 32) == (s_col // 32)
    eye = (t_row == s_col).astype(F32)
    r2, c2 = iota2((2 * w, w), 0), iota2((2 * w, w), 1)
    seg_ones2 = (((r2 % w) // hn) == (c2 // hn)).astype(BF16)
    tri3 = (iota2((c, 3 * c), 1) % c <= iota2((c, 3 * c), 0)).astype(BF16)

    def split(t, pieces):
        out = []
        for _ in range(pieces - 1):
            out.append(t.astype(BF16))
            t = t - out[-1].astype(F32)
        return out + [t.astype(BF16)]

    def seg_sum(t):
        return jnp.concatenate(
            [_dot(jnp.concatenate(split(t[:, p * w:(p + 1) * w], 2), axis=1), seg_ones2) for p in range(npair)],
            axis=1)

    def stack(t):
        zero = jnp.zeros_like(t)
        return jnp.concatenate([jnp.where(head0, t, zero), jnp.where(head0, zero, t)], axis=0)

    def bf(t):
        return t.astype(BF16)

    nch = tblock // c
    chunks = range(nch)
    items = [(ci, p) for ci in chunks for p in range(npair)]

    def rows(t, ci):
        return t[ci * c:(ci + 1) * c]

    def sub(t, it):
        ci, p = it
        return t[ci * c:(ci + 1) * c, p * w:(p + 1) * w]

    r, k, v, z, a, g = (ref[0] for ref in (r_ref, k_ref, v_ref, z_ref, a_ref, g_ref))
    kk = k * kk_ref[...]
    kk = kk * jnp.minimum(lax.rsqrt(seg_sum(kk * kk)), 1e12)
    kmod = k * (1.0 + (a - 1.0) * ka_ref[...])
    bvec = kk * a
    lw = (-0.5 * math.exp(-0.5)) * (1.0 + jnp.tanh(0.5 * z))
    cum = jnp.concatenate([_dot(tri3, jnp.concatenate(split(rows(lw, ci), 3), axis=0)) for ci in chunks], axis=0)
    at = -kk * jnp.exp(cum - lw)
    rt = r * jnp.exp(cum)
    inv = jnp.exp(-cum)
    bt = bvec * inv
    kt = kmod * inv
    totals = [cum[(ci + 1) * c - 1:(ci + 1) * c, :] for ci in chunks]
    to_end = jnp.concatenate([jnp.exp(totals[ci] - rows(cum, ci)) for ci in chunks], axis=0)
    bh = bvec * to_end
    kh = kmod * to_end

    n_it = range(len(items))
    atb, rtb = bf(at), bf(rt)
    ast = [stack(sub(atb, it)) for it in items]
    bk = [jnp.concatenate([stack(bf(sub(bt, it))), stack(bf(sub(kt, it)))], axis=0) for it in items]
    prods = [_dot_nt(jnp.concatenate([sub(atb, items[i]), sub(rtb, items[i])], axis=0), bk[i]) for i in n_it]
    l_full = [jnp.where(strict, prods[i][:c, :w], 0.0) for i in n_it]
    a_ak = [bf(jnp.where(strict, prods[i][:c, w:], 0.0)) for i in n_it]
    a_rb = [bf(jnp.where(incl, prods[i][c:, :w], 0.0)) for i in n_it]
    a_rk = [bf(jnp.where(incl, prods[i][c:, w:], 0.0)) for i in n_it]
    vst = [stack(bf(sub(v, it))) for it in items]
    x_loc = [_dot(a_ak[i], vst[i]) for i in n_it]
    y_loc = [_dot(a_rk[i], vst[i]) for i in n_it]

    xf = [bf(jnp.where(blk16, l_full[i], 0.0)) for i in n_it]
    xs = [stack(xf[i]) for i in n_it]
    tm = [eye + jnp.where(blk16, l_full[i], 0.0) for i in n_it]
    for _ in range(3):
        xf = [bf(_dot(xf[i], xs[i])) for i in n_it]
        xs = [stack(xf[i]) for i in n_it]
        tm = [tm[i] + _dot(bf(tm[i]), xs[i]) for i in n_it]
    for inside, outside in ((blk32, blk16), (None, blk32)):
        keep = jnp.logical_not(outside) if inside is None else inside & jnp.logical_not(outside)
        off = [stack(bf(jnp.where(keep, l_full[i], 0.0))) for i in n_it]
        tmb = [bf(tm[i]) for i in n_it]
        half = [bf(_dot(tmb[i], off[i])) for i in n_it]
        tm = [tm[i] + _dot(half[i], stack(tmb[i])) for i in n_it]
    t_fold = [bf(tm[i]) for i in n_it]

    au = [_dot(t_fold[i], jnp.concatenate([ast[i], bf(stack(x_loc[i]))], axis=1)) for i in n_it]
    ah = [au[i][:, :w] for i in n_it]
    ul = [au[i][:, w:] for i in n_it]
    ry = [_dot(a_rb[i], jnp.concatenate([bf(stack(ah[i])), bf(stack(ul[i]))], axis=1)) for i in n_it]
    rh = [bf(sub(rt, items[i]) + ry[i][:, :w]) for i in n_it]
    yl = [y_loc[i] + ry[i][:, w:] for i in n_it]
    w_mat = [bf(jnp.where(same_head, _dot_tn(bf(ah[i]), bf(sub(bh, items[i]))), 0.0)) for i in n_it]
    g_mat = [jnp.where(same_head,
                       _dot_tn(bf(jnp.concatenate([ul[i], sub(v, items[i])], axis=0)),
                               bf(jnp.concatenate([sub(bh, items[i]), sub(kh, items[i])], axis=0))), 0.0)
             for i in n_it]
    decay_c = [jnp.exp(totals[ci][:, p * w:(p + 1) * w]) for ci, p in items]

    s = [s_ref[p] for p in range(npair)]
    ys = [[None] * npair for _ in chunks]
    for i, (ci, p) in enumerate(items):
        sb = bf(s[p])
        ys[ci][p] = _dot_nt(rh[i], sb) + yl[i]
        s[p] = s[p] * decay_c[i] + _dot(sb, w_mat[i]) + g_mat[i]
    for p in range(npair):
        s_ref[p] = s[p]

    y = jnp.concatenate([jnp.concatenate(ys[ci], axis=1) if npair > 1 else ys[ci][0] for ci in chunks], axis=0)
    mean = seg_sum(y) * (1.0 / hn)
    yc = y - mean
    var = seg_sum(yc * yc) * (1.0 / hn)
    yn = yc * lax.rsqrt(var + RW_GN_EPS) * lg_ref[...] + lb_ref[...]
    yn = yn + seg_sum(r * kmod * rk_ref[...]) * v
    o_ref[0] = (yn * g).astype(o_ref.dtype)


def _wkv(r, k, v, z, a, g, k_k, k_a, r_k, lnx_g, lnx_b, batch, seq):
    n, d = r.shape
    w = 2 * RW_HEAD
    npair = WKV_PAIRS
    wt = npair * w
    tb = WKV_TBLOCK
    seq_spec = pl.BlockSpec((1, tb, wt), lambda b, h, t: (b, t, h))
    par_spec = pl.BlockSpec((1, wt), lambda b, h, t: (0, h))
    r3 = lambda t: t.reshape(batch, seq, d)
    out = pl.pallas_call(
        functools.partial(_wkv_kernel, tblock=tb, npair=npair),
        grid=(batch, d // wt, seq // tb),
        in_specs=[seq_spec] * 6 + [par_spec] * 5,
        out_specs=seq_spec,
        out_shape=jax.ShapeDtypeStruct((batch, seq, d), BF16),
        scratch_shapes=[pltpu.VMEM((npair, w, w), F32)],
        compiler_params=_params("parallel", "parallel", "arbitrary"),
        name="wkv7_chunked",
    )(r3(r), r3(k), r3(v), r3(z), r3(a), r3(g), k_k.reshape(1, d), k_a.reshape(1, d), r_k.reshape(1, d),
      lnx_g.reshape(1, d), lnx_b.reshape(1, d))
    return out.reshape(n, d)


def kernel(x, ln1_g, ln1_b, ln2_g, ln2_b, attn_w_qkv, attn_w_o, attn_lambda, attn_subln_g, rw_mu, rw_w_rkv, rw_w_o, rw_w0, rw_w1, rw_w2, rw_a0, rw_a1, rw_a2, rw_g1, rw_g2, rw_k_k, rw_k_a, rw_r_k, rw_lnx_g, rw_lnx_b, rw_v0, rw_v1, rw_v2, moe_rg_w, moe_rg_b, moe_re_w, moe_re_b, moe_w_gu, moe_w_down):
    batch, seq, d = x.shape
    depth = ln1_g.shape[0]
    n = batch * seq
    alpha = (2 * depth) ** 0.25
    x = x.reshape(n, d)
    v_first = None
    for i in range(depth):
        j = i // N_MIXERS
        if i % N_MIXERS == 0:
            lambda_init = 0.8 - 0.6 * math.exp(-0.3 * i)
            lam = attn_lambda[j]
            lam_full = jnp.exp(jnp.sum(lam[0] * lam[1])) - jnp.exp(jnp.sum(lam[2] * lam[3])) + lambda_init
            lam_row = jnp.full((1, 2 * DA_HEAD_DIM), lam_full, F32)
            q_scale = DA_HEAD_DIM ** -0.5 * math.log2(math.e)
            col_scale = jnp.concatenate([jnp.full((d,), q_scale, F32), jnp.ones((2 * d,), F32)])
            w_qkv = (attn_w_qkv[j] * col_scale[None, :]).astype(BF16)
            qkv = _proj(x, w_qkv, BF16)
            mixed = _diff_attention(qkv, lam_row, attn_subln_g[j], lambda_init, batch, seq)
            w_o = attn_w_o[j]
        else:
            value_mix = None if j == 0 else (rw_v0[j - 1], rw_v1[j - 1], rw_v2[j - 1])
            r, k, v, z, a, g = _rwkv_proj(x, seq, rw_mu[j], rw_w_rkv[j], rw_w0[j], rw_w1[j], rw_w2[j], rw_a0[j],
                                          rw_a1[j], rw_a2[j], rw_g1[j], rw_g2[j], value_mix, v_first)
            if value_mix is None:
                v_first = v
            mixed = _wkv(r, k, v, z, a, g, rw_k_k[j], rw_k_a[j], rw_r_k[j].reshape(d), rw_lnx_g[j], rw_lnx_b[j],
                         batch, seq)
            w_o = rw_w_o[j]
        n_router = MOE_GROUPS + MOE_EXPERTS
        w_router = _pad_cols(jnp.concatenate([moe_rg_w[i], moe_re_w[i]], axis=1), LANES)
        wr_hi = w_router.astype(BF16)
        wr_lo = (w_router - wr_hi.astype(F32)).astype(BF16)
        w_router = jnp.concatenate([wr_hi, wr_hi, wr_lo], axis=0)
        b_router = _pad_cols(jnp.concatenate([moe_rg_b[i], moe_re_b[i]]).reshape(1, n_router), LANES)
        x1, x1_packed, logits = _post_mixer(mixed, x, w_o.astype(BF16), ln1_g[i], ln1_b[i], w_router, b_router,
                                            alpha)
        x = _hier_moe_block(x1, x1_packed, logits, moe_w_gu.reshape((-1,) + moe_w_gu.shape[2:]),
                            moe_w_down.reshape((-1,) + moe_w_down.shape[2:]), i, ln2_g[i], ln2_b[i], alpha)
    return x.reshape(batch, seq, d)
```

```python
import functools
import math

import jax
import jax.numpy as jnp
from jax import lax
from jax.experimental import pallas as pl
from jax.experimental.pallas import tpu as pltpu

F32 = jnp.float32
BF16 = jnp.bfloat16

LANES = 128
VMEM_LIMIT = 56 * 1024 * 1024

DA_HEADS = 8
DA_HEAD_DIM = 64
RMS_EPS = 1e-5
RW_HEAD = 64
RW_GN_EPS = 64e-5
MOE_GROUPS = 4
MOE_EPG = 8
MOE_EXPERTS = MOE_GROUPS * MOE_EPG
MOE_TOPK = 2
LN_EPS = 1e-5
N_MIXERS = 2

ROW_TILE = 256
ROUTER_TILE = 512
MOE_TOKEN_TILE = 512
ATTN_TQ = 512
ATTN_TK = 512
WKV_CHUNK = 64
WKV_TBLOCK = 512
WKV_PAIRS = 4
MOE_BLOCK = 512
DMA_UNROLL = 8
COMBINE_GROUPS = 8
FUSED_LN_GROUPS = 4
FUSED_GROUPS = 16
PROJ_PIECE = 256
NEG_BIG = -1e30


def _dot(a, b, precision=None):
    return jnp.dot(a, b, preferred_element_type=F32, precision=precision)


def _dot_nt(a, b):
    return lax.dot_general(a, b, (((1,), (1,)), ((), ())), preferred_element_type=F32)


def _dot_tn(a, b):
    return lax.dot_general(a, b, (((0,), (0,)), ((), ())), preferred_element_type=F32)


def _params(*sem):
    return pltpu.CompilerParams(dimension_semantics=sem, vmem_limit_bytes=VMEM_LIMIT)


def _const_spec(shape):
    nd = len(shape)
    return pl.BlockSpec(shape, lambda *_: (0,) * nd)


def _proj_kernel(x_ref, w_ref, o_ref):
    o_ref[...] = _dot(x_ref[...].astype(BF16), w_ref[...]).astype(o_ref.dtype)


def _proj(x, w, out_dtype):
    n, d = x.shape
    m = w.shape[1]
    tm = 512
    return pl.pallas_call(
        _proj_kernel,
        grid=(n // tm,),
        in_specs=[pl.BlockSpec((tm, d), lambda i: (i, 0)), _const_spec((d, m))],
        out_specs=pl.BlockSpec((tm, m), lambda i: (i, 0)),
        out_shape=jax.ShapeDtypeStruct((n, m), out_dtype),
        compiler_params=_params("parallel"),
        name="qkv_proj",
    )(x, w)


def _attn_kernel(lam_ref, g_ref, q_ref, k_ref, v_ref, o_ref, vx_ref, m_ref, acc_ref, sa_ref, sb_ref, *, tq, tk,
                 out_scale):
    qi = pl.program_id(2)
    hd = DA_HEAD_DIM
    hw = 2 * hd

    @pl.when(qi == 0)
    def _():
        vx_ref[:, :hw] = v_ref[0]
        vx_ref[:, hw:] = jnp.ones((vx_ref.shape[0], hw), BF16)

    q = q_ref[0]
    lane = lax.broadcasted_iota(jnp.int32, q.shape, 1)
    zero = jnp.zeros_like(q)
    q_stack = jnp.concatenate([jnp.where(lane < hd, q, zero), jnp.where(lane >= hd, q, zero)], axis=0)
    m_ref[...] = jnp.full(m_ref.shape, NEG_BIG, F32)
    acc_ref[...] = jnp.zeros(acc_ref.shape, F32)
    row0 = qi * tq

    def scores(j, s_ref):
        start = pl.multiple_of(j * tk, tk)
        s_ref[...] = _dot_nt(q_stack, k_ref[0, pl.ds(start, tk), :])

    def consume(j, s_ref, masked):
        start = pl.multiple_of(j * tk, tk)
        vb = vx_ref[pl.ds(start, tk), :]
        if masked:
            row = row0 + lax.broadcasted_iota(jnp.int32, (tq, tk), 0)
            col = start + lax.broadcasted_iota(jnp.int32, (tq, tk), 1)
            keep = col <= row
        for c in range(2):
            s = s_ref[c * tq:(c + 1) * tq, :]
            if masked:
                s = jnp.where(keep, s, NEG_BIG)
            chunks = [s[:, i * LANES:(i + 1) * LANES] for i in range(tk // LANES)]
            m_old = m_ref[c]
            m_new = jnp.maximum(m_old, jnp.max(functools.reduce(jnp.maximum, chunks), axis=-1, keepdims=True))
            alpha = jnp.exp2(m_old - m_new)
            m_ref[c] = m_new
            p = jnp.concatenate([jnp.exp2(ch - m_new).astype(BF16) for ch in chunks], axis=1)
            acc_ref[c] = acc_ref[c] * jnp.concatenate([alpha, alpha], axis=1) + _dot(p, vb)

    n = row0 // tk + 1
    n_pairs = (n - 1) // 2
    scores(0, sa_ref)

    def pair(jj, carry):
        j = 2 * jj
        scores(j + 1, sb_ref)
        consume(j, sa_ref, False)
        scores(j + 2, sa_ref)
        consume(j + 1, sb_ref, False)
        return carry

    lax.fori_loop(0, n_pairs, pair, 0)
    jb = 2 * n_pairs

    @pl.when(n % 2 == 1)
    def _():
        consume(jb, sa_ref, True)

    @pl.when(n % 2 == 0)
    def _():
        scores(jb + 1, sb_ref)
        consume(jb, sa_ref, False)
        consume(jb + 1, sb_ref, True)

    a0, a1 = acc_ref[0], acc_ref[1]
    o = a0[:, :hw] * (1.0 / a0[:, hw:]) - lam_ref[...] * (a1[:, :hw] * (1.0 / a1[:, hw:]))
    ms = jnp.mean(o * o, axis=-1, keepdims=True)
    o = o * lax.rsqrt(ms + RMS_EPS) * g_ref[...] * out_scale
    o_ref[0] = o.astype(o_ref.dtype)


def _diff_attention(qkv, lam_row, subln_g, lambda_init, batch, seq):
    d = DA_HEADS * 2 * DA_HEAD_DIM
    qkv = qkv.reshape(batch, seq, 3 * d)
    tq, tk = ATTN_TQ, ATTN_TK
    assert tq <= tk and tk % tq == 0 and seq % tk == 0
    hw = 2 * DA_HEAD_DIM
    out = pl.pallas_call(
        functools.partial(_attn_kernel, tq=tq, tk=tk, out_scale=1.0 - lambda_init),
        grid=(batch, DA_HEADS, seq // tq),
        in_specs=[
            _const_spec((1, hw)),
            _const_spec((1, hw)),
            pl.BlockSpec((1, tq, hw), lambda b, h, i: (b, i, h)),
            pl.BlockSpec((1, seq, hw), lambda b, h, i: (b, 0, DA_HEADS + h)),
            pl.BlockSpec((1, seq, hw), lambda b, h, i: (b, 0, 2 * DA_HEADS + h)),
        ],
        out_specs=pl.BlockSpec((1, tq, hw), lambda b, h, i: (b, i, h)),
        out_shape=jax.ShapeDtypeStruct((batch, seq, d), BF16),
        scratch_shapes=[pltpu.VMEM((seq, 2 * hw), BF16), pltpu.VMEM((2, tq, hw), F32),
                        pltpu.VMEM((2, tq, 2 * hw), F32), pltpu.VMEM((2 * tq, tk), F32),
                        pltpu.VMEM((2 * tq, tk), F32)],
        compiler_params=_params("parallel", "parallel", "arbitrary"),
        name="diff_attn",
    )(lam_row, subln_g.reshape(1, hw), qkv, qkv, qkv)
    return out.reshape(batch * seq, d)


def _layer_norm_rows(z, g, b):
    mu = jnp.mean(z, axis=-1, keepdims=True)
    zc = z - mu
    var = jnp.mean(zc * zc, axis=-1, keepdims=True)
    return zc * lax.rsqrt(var + LN_EPS) * g + b


HIGH16 = 0xFFFF0000


def _pack_bf16_halves(x_hi):
    half = x_hi.shape[1] // 2
    bits = lax.bitcast_convert_type(x_hi, jnp.uint32)
    return (bits[:, :half] >> 16) | (bits[:, half:] & jnp.uint32(HIGH16))


def _unpack_bf16_halves(packed):
    lo = lax.bitcast_convert_type(packed << 16, F32)
    hi = lax.bitcast_convert_type(packed & jnp.uint32(HIGH16), F32)
    return jnp.concatenate([lo, hi], axis=1)


def _post_mixer_kernel(o_ref, x_ref, w_ref, lg_ref, lb_ref, wr_ref, br_ref, x1_ref, x1p_ref, lo_ref, *, alpha):
    h = _dot(o_ref[...], w_ref[...])
    x1 = _layer_norm_rows(alpha * x_ref[...] + h, lg_ref[...], lb_ref[...])
    x1_ref[...] = x1
    x_hi = x1.astype(BF16)
    x_hi32 = x_hi.astype(F32)
    x_lo = (x1 - x_hi32).astype(BF16)
    x1p_ref[...] = _pack_bf16_halves(x_hi32)
    lo_ref[...] = _dot(jnp.concatenate([x_hi, x_lo, x_hi], axis=1), wr_ref[...]) + br_ref[...]


def _post_mixer(o, x, w_o, ln_g, ln_b, w_router, b_router, alpha):
    n, d = x.shape
    tm = ROW_TILE
    row = lambda i: (i, 0)
    return pl.pallas_call(
        functools.partial(_post_mixer_kernel, alpha=alpha),
        grid=(n // tm,),
        in_specs=[
            pl.BlockSpec((tm, d), row), pl.BlockSpec((tm, d), row), _const_spec((d, d)),
            _const_spec((1, d)), _const_spec((1, d)), _const_spec((3 * d, LANES)), _const_spec((1, LANES)),
        ],
        out_specs=[pl.BlockSpec((tm, d), row), pl.BlockSpec((tm, d // 2), row), pl.BlockSpec((tm, LANES), row)],
        out_shape=[jax.ShapeDtypeStruct((n, d), F32), jax.ShapeDtypeStruct((n, d // 2), jnp.uint32),
                   jax.ShapeDtypeStruct((n, LANES), F32)],
        compiler_params=_params("parallel"),
        name="post_mixer",
    )(o, x, w_o, ln_g.reshape(1, d), ln_b.reshape(1, d), w_router, b_router)


def _router_kernel(lo_ref, idx_ref, gate_ref, cnt_ref, carry_ref):
    step = pl.program_id(0)

    @pl.when(step == 0)
    def _():
        carry_ref[...] = jnp.zeros_like(carry_ref)

    lo = lo_ref[...]
    tm = lo.shape[0]
    lane = lax.broadcasted_iota(jnp.int32, lo.shape, 1)
    big = jnp.int32(LANES)

    def softmax_over(mask):
        mx = jnp.max(jnp.where(mask, lo, NEG_BIG), axis=-1, keepdims=True)
        ex = jnp.where(mask, jnp.exp(lo - mx), 0.0)
        return ex / jnp.sum(ex, axis=-1, keepdims=True)

    def top1(p, mask):
        best = jnp.max(jnp.where(mask, p, -1.0), axis=-1, keepdims=True)
        arg = jnp.min(jnp.where(mask & (p == best), lane, big), axis=-1, keepdims=True)
        return best, arg

    g_mask = lane < MOE_GROUPS
    g_p, g_idx = top1(softmax_over(g_mask), g_mask)
    first = MOE_GROUPS + MOE_EPG * g_idx
    e_mask = (lane >= first) & (lane < first + MOE_EPG)
    e_prob = softmax_over(e_mask)
    p1, i1 = top1(e_prob, e_mask)
    rest = e_mask & (lane != i1)
    p2, i2 = top1(e_prob, rest)
    denom = p1 + p2
    gate1 = g_p * (p1 / denom)
    gate2 = g_p * (p2 / denom)
    e1 = i1 - MOE_GROUPS
    e2 = i2 - MOE_GROUPS

    oh1 = (lane == e1).astype(F32)
    oh2 = (lane == e2).astype(F32)
    both = oh1 + oh2
    r_i = lax.broadcasted_iota(jnp.int32, (tm, tm), 0)
    c_i = lax.broadcasted_iota(jnp.int32, (tm, tm), 1)
    before = (c_i < r_i).astype(BF16)
    seen = _dot(before, both.astype(BF16)) + carry_ref[...]
    pos1 = jnp.sum(seen * oh1, axis=-1, keepdims=True).astype(jnp.int32)
    pos2 = jnp.sum(seen * oh2, axis=-1, keepdims=True).astype(jnp.int32)
    carry_ref[...] += jnp.sum(both, axis=0, keepdims=True)
    cnt_ref[...] = carry_ref[...].astype(jnp.int32)

    zero_i = jnp.zeros(lo.shape, jnp.int32)
    idx_ref[...] = (jnp.where(lane == 0, e1, zero_i) + jnp.where(lane == 1, e2, zero_i)
                    + jnp.where(lane == 2, pos1, zero_i) + jnp.where(lane == 3, pos2, zero_i))
    gate_ref[...] = jnp.where(lane == 0, gate1, 0.0) + jnp.where(lane == 1, gate2, 0.0)


def _router(logits):
    n = logits.shape[0]
    tm = ROUTER_TILE
    row = lambda i: (i, 0)
    return pl.pallas_call(
        _router_kernel,
        grid=(n // tm,),
        in_specs=[pl.BlockSpec((tm, LANES), row)],
        out_specs=[pl.BlockSpec((tm, LANES), row), pl.BlockSpec((tm, LANES), row), _const_spec((1, LANES))],
        out_shape=[jax.ShapeDtypeStruct((n, LANES), jnp.int32), jax.ShapeDtypeStruct((n, LANES), F32),
                   jax.ShapeDtypeStruct((1, LANES), jnp.int32)],
        scratch_shapes=[pltpu.VMEM((1, LANES), F32)],
        compiler_params=_params("arbitrary"),
        name="moe_router",
    )(logits)


def _row_copy(src_ref, src_row, dst_ref, dst_row, sem):
    return pltpu.make_async_copy(src_ref.at[pl.ds(src_row, 1)], dst_ref.at[pl.ds(dst_row, 1)], sem)


def _dispatch_kernel(dest_ref, x_ref, buf_in_ref, buf_ref, sem):
    del buf_in_ref
    tm = x_ref.shape[0]

    def copies(r):
        return [_row_copy(x_ref, r, buf_ref, dest_ref[0, 0, MOE_TOPK * r + s], sem) for s in range(MOE_TOPK)]

    def start(r, carry):
        for cp in copies(r):
            cp.start()
        return carry

    def wait(r, carry):
        for cp in copies(r):
            cp.wait()
        return carry

    lax.fori_loop(0, tm, start, 0, unroll=DMA_UNROLL)
    lax.fori_loop(0, tm, wait, 0, unroll=DMA_UNROLL)


def _dispatch(x1, dest_tiles, n_rows):
    n, d = x1.shape
    tm = MOE_TOKEN_TILE
    zeros = jnp.zeros((n_rows, d), x1.dtype)
    return pl.pallas_call(
        _dispatch_kernel,
        grid=(n // tm,),
        in_specs=[
            pl.BlockSpec((1, 1, MOE_TOPK * tm), lambda i: (i, 0, 0), memory_space=pltpu.SMEM),
            pl.BlockSpec((tm, d), lambda i: (i, 0)),
            pl.BlockSpec(memory_space=pl.ANY),
        ],
        out_specs=pl.BlockSpec(memory_space=pl.ANY),
        out_shape=jax.ShapeDtypeStruct((n_rows, d), x1.dtype),
        scratch_shapes=[pltpu.SemaphoreType.DMA(())],
        input_output_aliases={2: 0},
        compiler_params=_params("arbitrary"),
        name="moe_dispatch",
    )(dest_tiles, x1, zeros)


def _expert_kernel(be_ref, used_ref, x_ref, wgu_ref, wd_ref, y_ref, wgu_bf_ref, wd_bf_ref, *, hidden):
    i = pl.program_id(0)
    in_use = i < used_ref[0]

    @pl.when(in_use & ((i == 0) | (be_ref[i] != be_ref[jnp.maximum(i - 1, 0)])))
    def _():
        wgu_bf_ref[...] = wgu_ref[0].astype(BF16)
        wd_bf_ref[...] = wd_ref[0].astype(BF16)

    @pl.when(in_use)
    def _():
        n_parts = 2
        part = x_ref.shape[0] // n_parts
        rows = [pl.ds(p * part, part) for p in range(n_parts)]
        hs = [_dot(_unpack_bf16_halves(x_ref[r, :]).astype(BF16), wgu_bf_ref[...]) for r in rows]
        for r, h in zip(rows, hs):
            hg = h[:, :hidden]
            hu = h[:, hidden:]
            act = hg * (1.0 / (1.0 + jnp.exp(-hg))) * hu
            y_ref[r, :] = _dot(act.astype(BF16), wd_bf_ref[...])

    @pl.when(jnp.logical_not(in_use))
    def _():
        y_ref[...] = jnp.zeros_like(y_ref)


def _experts(x_buf, block_e, n_used, w_gu, w_down):
    n_rows = x_buf.shape[0]
    hidden, d = w_down.shape[1:]
    tm = MOE_BLOCK

    def live(i, used):
        return jnp.minimum(i, used[0] - 1)

    grid_spec = pltpu.PrefetchScalarGridSpec(
        num_scalar_prefetch=2,
        grid=(n_rows // tm,),
        in_specs=[
            pl.BlockSpec((tm, d // 2), lambda i, be, used: (live(i, used), 0)),
            pl.BlockSpec((1, d, 2 * hidden), lambda i, be, used: (be[live(i, used)], 0, 0)),
            pl.BlockSpec((1, hidden, d), lambda i, be, used: (be[live(i, used)], 0, 0)),
        ],
        out_specs=pl.BlockSpec((tm, d), lambda i, be, used: (i, 0)),
        scratch_shapes=[pltpu.VMEM((d, 2 * hidden), BF16), pltpu.VMEM((hidden, d), BF16)],
    )
    return pl.pallas_call(
        functools.partial(_expert_kernel, hidden=hidden),
        grid_spec=grid_spec,
        out_shape=jax.ShapeDtypeStruct((n_rows, d), F32),
        compiler_params=_params("arbitrary"),
        name="moe_experts",
    )(block_e, n_used, x_buf, w_gu, w_down)


def _combine_tile(dest_ref, dest_next_ref, y_ref, gate_ref, x_ref, lg_ref, lb_ref, o_ref, rows_ref, sems, *, alpha,
                  n_groups, ln_groups):
    i = pl.program_id(0)
    last = pl.num_programs(0) - 1
    tm = x_ref.shape[0]
    slot = i % 2

    def copies(d_ref, r, to_slot):
        return [_row_copy(y_ref, d_ref[0, 0, MOE_TOPK * r + s], rows_ref.at[to_slot, s], r, sems.at[to_slot])
                for s in range(MOE_TOPK)]

    def start_all(d_ref, to_slot):
        def body(r, carry):
            for cp in copies(d_ref, r, to_slot):
                cp.start()
            return carry
        lax.fori_loop(0, tm, body, 0, unroll=DMA_UNROLL)

    def wait_all(d_ref, to_slot):
        def body(r, carry):
            for cp in copies(d_ref, r, to_slot):
                cp.wait()
            return carry
        lax.fori_loop(0, tm, body, 0, unroll=DMA_UNROLL)

    @pl.when(i == 0)
    def _():
        start_all(dest_ref, slot)

    wait_all(dest_ref, slot)
    per = tm // n_groups
    issued = [0]

    def issue():
        g = issued[0]
        if g < n_groups:
            issued[0] += 1
            for r in range(g * per, (g + 1) * per):
                for cp in copies(dest_next_ref, r, 1 - slot):
                    cp.start()

    group = tm // ln_groups
    for g in range(ln_groups):
        issue()
        rows = pl.ds(g * group, group)
        gate = gate_ref[rows, :]
        f = sum(gate[:, s:s + 1] * rows_ref[slot, s, rows, :] for s in range(MOE_TOPK))
        o_ref[rows, :] = _layer_norm_rows(alpha * x_ref[rows, :] + f, lg_ref[...], lb_ref[...])

    def finish():
        while issued[0] < n_groups:
            issue()

        @pl.when(i == last)
        def _():
            wait_all(dest_next_ref, 1 - slot)

    return issue, finish


def _combine_kernel(*refs, alpha):
    _, finish = _combine_tile(*refs, alpha=alpha, n_groups=COMBINE_GROUPS, ln_groups=COMBINE_GROUPS)
    finish()


def _combine_qkv_kernel(dest_ref, dest_next_ref, y_ref, gate_ref, x_ref, lg_ref, lb_ref, w_ref, o_ref, qkv_ref,
                        rows_ref, sems, *, alpha):
    n_pieces = w_ref.shape[1] // PROJ_PIECE
    issue, finish = _combine_tile(dest_ref, dest_next_ref, y_ref, gate_ref, x_ref, lg_ref, lb_ref, o_ref, rows_ref,
                                  sems, alpha=alpha, n_groups=FUSED_LN_GROUPS + n_pieces, ln_groups=FUSED_LN_GROUPS)
    xb = o_ref[...].astype(BF16)
    for c in range(n_pieces):
        issue()
        cols = pl.ds(c * PROJ_PIECE, PROJ_PIECE)
        qkv_ref[:, cols] = _dot(xb, w_ref[:, cols]).astype(qkv_ref.dtype)
    finish()


def _combine_rwkv_kernel(*refs, alpha, seq, has_mix):
    (dest_ref, dest_next_ref, y_ref, gate_ref, x_ref, lg_ref, lb_ref), refs = refs[:7], refs[7:]
    n_par = 14 if has_mix else 10
    par, (o_ref, *outs, rows_ref, sems, last_ref) = refs[:n_par], refs[n_par:]
    issue, finish = _combine_tile(dest_ref, dest_next_ref, y_ref, gate_ref, x_ref, lg_ref, lb_ref, o_ref, rows_ref,
                                  sems, alpha=alpha, n_groups=FUSED_GROUPS, ln_groups=FUSED_LN_GROUPS)
    i = pl.program_id(0)

    @pl.when(i == 0)
    def _():
        last_ref[...] = jnp.zeros_like(last_ref)

    x2 = o_ref[...]
    tm = x2.shape[0]
    prev_last = jnp.where((i * tm) % seq == 0, 0.0, last_ref[...])
    last_ref[...] = x2[tm - 1:tm, :]

    def dot(a, b):
        issue()
        return _dot(a, b)

    _rwkv_project(x2, prev_last, par, outs, has_mix, dot)
    finish()


def _combine(y, dest, gates, x1, ln_g, ln_b, alpha, follow=None):
    n, d = x1.shape
    row = lambda i: (i, 0)
    tm = MOE_TOKEN_TILE if follow is None else ROW_TILE
    n_tiles = n // tm
    dest_tiles = dest.reshape(n_tiles, 1, MOE_TOPK * tm)
    dest_spec = lambda index: pl.BlockSpec((1, 1, MOE_TOPK * tm), index, memory_space=pltpu.SMEM)
    ins = [dest_tiles, dest_tiles, y, gates, x1, ln_g.reshape(1, d), ln_b.reshape(1, d)]
    in_specs = [
        dest_spec(lambda i: (i, 0, 0)),
        dest_spec(lambda i: (jnp.minimum(i + 1, n_tiles - 1), 0, 0)),
        pl.BlockSpec(memory_space=pl.ANY),
        pl.BlockSpec((tm, LANES), row), pl.BlockSpec((tm, d), row), _const_spec((1, d)), _const_spec((1, d)),
    ]
    out_specs = [pl.BlockSpec((tm, d), row)]
    out_shape = [jax.ShapeDtypeStruct((n, d), F32)]
    scratch = [pltpu.VMEM((2, MOE_TOPK, tm, d), F32), pltpu.SemaphoreType.DMA((2,))]
    if follow is None:
        body, name = functools.partial(_combine_kernel, alpha=alpha), "moe_combine"
    elif follow[0] == "qkv":
        w_qkv = follow[1]
        ins.append(w_qkv)
        in_specs.append(_const_spec(w_qkv.shape))
        out_specs.append(pl.BlockSpec((tm, w_qkv.shape[1]), row))
        out_shape.append(jax.ShapeDtypeStruct((n, w_qkv.shape[1]), BF16))
        body, name = functools.partial(_combine_qkv_kernel, alpha=alpha), "moe_combine_qkv"
    else:
        _, seq, (par, par_specs), v_first = follow
        has_mix = v_first is not None
        ins += par
        in_specs += par_specs
        if has_mix:
            ins.append(v_first)
            in_specs.append(pl.BlockSpec((tm, d), row))
        out_specs += [pl.BlockSpec((tm, d), row)] * 6
        out_shape += [jax.ShapeDtypeStruct((n, d), F32)] * 6
        scratch.append(pltpu.VMEM((1, d), F32))
        body = functools.partial(_combine_rwkv_kernel, alpha=alpha, seq=seq, has_mix=has_mix)
        name = "moe_combine_rwkv"
    out = pl.pallas_call(
        body,
        grid=(n_tiles,),
        in_specs=in_specs,
        out_specs=out_specs,
        out_shape=out_shape,
        scratch_shapes=scratch,
        compiler_params=_params("arbitrary"),
        name=name,
    )(*ins)
    return out[0] if follow is None else out


def _hier_moe_block(x1, x1_packed, logits, w_gu, w_down, layer, ln_g, ln_b, alpha, follow):
    n_tok, d = x1.shape
    idx, gates, counts = _router(logits)
    counts = counts[0, :MOE_EXPERTS]
    padded = (counts + MOE_BLOCK - 1) // MOE_BLOCK * MOE_BLOCK
    pad_end = jnp.cumsum(padded)
    pad_start = pad_end - padded
    expert = idx[:, :MOE_TOPK]
    pos = idx[:, MOE_TOPK:2 * MOE_TOPK]
    e_iota = jnp.arange(MOE_EXPERTS, dtype=jnp.int32)
    base = jnp.sum(jnp.where(expert[:, :, None] == e_iota, pad_start, 0), axis=-1)
    dest = (base + pos).astype(jnp.int32)
    n_blocks = n_tok * MOE_TOPK // MOE_BLOCK + MOE_EXPERTS
    n_rows = n_blocks * MOE_BLOCK
    block_start = jnp.arange(n_blocks, dtype=jnp.int32) * MOE_BLOCK
    block_e = jnp.minimum(jnp.sum((pad_end[None, :] <= block_start[:, None]).astype(jnp.int32), axis=1),
                          MOE_EXPERTS - 1) + layer * MOE_EXPERTS
    dest_tiles = dest.reshape(n_tok // MOE_TOKEN_TILE, 1, MOE_TOPK * MOE_TOKEN_TILE)
    x_buf = _dispatch(x1_packed, dest_tiles, n_rows)
    n_used = (pad_end[MOE_EXPERTS - 1:] // MOE_BLOCK).astype(jnp.int32)
    y = _experts(x_buf, block_e, n_used, w_gu, w_down)
    return _combine(y, dest, gates, x1, ln_g, ln_b, alpha, follow)


def _rwkv_project(x, prev_last, par, outs, has_mix, dot):
    if has_mix:
        (mu_ref, wrkv_ref, w0_ref, w1_ref, w2_ref, a0_ref, a1_ref, a2_ref, g1_ref, g2_ref,
         v0_ref, v1_ref, v2_ref, vf_ref) = par
    else:
        mu_ref, wrkv_ref, w0_ref, w1_ref, w2_ref, a0_ref, a1_ref, a2_ref, g1_ref, g2_ref = par
    r_ref, k_ref, v_ref, z_ref, a_ref, g_ref = outs
    row = lax.broadcasted_iota(jnp.int32, x.shape, 0)
    shifted = jnp.where(row == 0, prev_last, pltpu.roll(x, 1, axis=0))
    xx = shifted - x

    def mixed(n):
        return (x + xx * mu_ref[n:n + 1, :]).astype(BF16)

    def sigmoid(t):
        return 0.5 * (1.0 + jnp.tanh(0.5 * t))

    r_ref[...] = dot(mixed(0), wrkv_ref[0])
    k_ref[...] = dot(mixed(1), wrkv_ref[1])
    xv = mixed(2)
    v = dot(xv, wrkv_ref[2])
    if has_mix:
        mix = sigmoid(v0_ref[...] + dot(dot(xv, v1_ref[...]).astype(BF16), v2_ref[...]))
        v = v + (vf_ref[...] - v) * mix
    v_ref[...] = v
    z_ref[...] = w0_ref[...] + dot(jnp.tanh(dot(mixed(3), w1_ref[...])).astype(BF16), w2_ref[...])
    a_ref[...] = sigmoid(a0_ref[...] + dot(dot(mixed(4), a1_ref[...]).astype(BF16), a2_ref[...]))
    g_ref[...] = dot(sigmoid(dot(mixed(5), g1_ref[...])).astype(BF16), g2_ref[...])


def _pad_cols(w, width):
    return jnp.pad(w, ((0, 0), (0, width - w.shape[1])))


def _pad_rows(w, height):
    return jnp.pad(w, ((0, height - w.shape[0]), (0, 0)))


def _rwkv_params(d, mu, w_rkv, w0, w1, w2, a0, a1, a2, g1, g2, value_mix):
    lora = lambda w_in, w_out, width: (_pad_cols(w_in, width).astype(BF16), _pad_rows(w_out, width).astype(BF16))
    w1p, w2p = lora(w1, w2, LANES)
    a1p, a2p = lora(a1, a2, LANES)
    g1p, g2p = lora(g1, g2, 2 * LANES)
    par = [_pad_rows(mu, 8), w_rkv.astype(BF16), w0.reshape(1, d), w1p, w2p, a0.reshape(1, d), a1p, a2p, g1p, g2p]
    specs = [
        _const_spec((8, d)), _const_spec((3, d, d)), _const_spec((1, d)),
        _const_spec((d, LANES)), _const_spec((LANES, d)), _const_spec((1, d)),
        _const_spec((d, LANES)), _const_spec((LANES, d)),
        _const_spec((d, 2 * LANES)), _const_spec((2 * LANES, d)),
    ]
    if value_mix is not None:
        v0, v1, v2 = value_mix
        v1p, v2p = lora(v1, v2, LANES)
        par += [v0.reshape(1, d), v1p, v2p]
        specs += [_const_spec((1, d)), _const_spec((d, LANES)), _const_spec((LANES, d))]
    return par, specs


def _wkv_kernel(r_ref, k_ref, v_ref, z_ref, a_ref, g_ref, kk_ref, ka_ref, rk_ref, lg_ref, lb_ref, o_ref, s_ref,
                *, tblock, npair):
    c = WKV_CHUNK
    hn = RW_HEAD
    w = 2 * hn
    wt = npair * w

    @pl.when(pl.program_id(2) == 0)
    def _():
        s_ref[...] = jnp.zeros_like(s_ref)

    def iota2(shape, dim):
        return lax.broadcasted_iota(jnp.int32, shape, dim)

    lane_c = iota2((c, w), 1)
    head0 = lane_c < hn
    rw, cw = iota2((w, w), 0), iota2((w, w), 1)
    same_head = (rw // hn) == (cw // hn)
    t_row, s_col = iota2((c, w), 0), lane_c % hn
    strict = s_col < t_row
    incl = s_col <= t_row
    blk16 = (t_row // 16) == (s_col // 16)
    blk32 = (t_row // 32) == (s_col // 32)
    eye = (t_row == s_col).astype(F32)
    r2, c2 = iota2((2 * w, w), 0), iota2((2 * w, w), 1)
    seg_ones2 = (((r2 % w) // hn) == (c2 // hn)).astype(BF16)
    tri3 = (iota2((c, 3 * c), 1) % c <= iota2((c, 3 * c), 0)).astype(BF16)

    def split(t, pieces):
        out = []
        for _ in range(pieces - 1):
            out.append(t.astype(BF16))
            t = t - out[-1].astype(F32)
        return out + [t.astype(BF16)]

    def seg_sum(t):
        return jnp.concatenate(
            [_dot(jnp.concatenate(split(t[:, p * w:(p + 1) * w], 2), axis=1), seg_ones2) for p in range(npair)],
            axis=1)

    def stack(t):
        zero = jnp.zeros_like(t)
        return jnp.concatenate([jnp.where(head0, t, zero), jnp.where(head0, zero, t)], axis=0)

    def bf(t):
        return t.astype(BF16)

    nch = tblock // c
    chunks = range(nch)
    items = [(ci, p) for ci in chunks for p in range(npair)]

    def rows(t, ci):
        return t[ci * c:(ci + 1) * c]

    def sub(t, it):
        ci, p = it
        return t[ci * c:(ci + 1) * c, p * w:(p + 1) * w]

    r, k, v, z, a, g = (ref[0] for ref in (r_ref, k_ref, v_ref, z_ref, a_ref, g_ref))
    kk = k * kk_ref[...]
    kk = kk * jnp.minimum(lax.rsqrt(seg_sum(kk * kk)), 1e12)
    kmod = k * (1.0 + (a - 1.0) * ka_ref[...])
    bvec = kk * a
    lw = (-0.5 * math.exp(-0.5)) * (1.0 + jnp.tanh(0.5 * z))
    cum = jnp.concatenate([_dot(tri3, jnp.concatenate(split(rows(lw, ci), 3), axis=0)) for ci in chunks], axis=0)
    at = -kk * jnp.exp(cum - lw)
    rt = r * jnp.exp(cum)
    inv = jnp.exp(-cum)
    bt = bvec * inv
    kt = kmod * inv
    totals = [cum[(ci + 1) * c - 1:(ci + 1) * c, :] for ci in chunks]
    to_end = jnp.concatenate([jnp.exp(totals[ci] - rows(cum, ci)) for ci in chunks], axis=0)
    bh = bvec * to_end
    kh = kmod * to_end

    n_it = range(len(items))
    atb, rtb = bf(at), bf(rt)
    ast = [stack(sub(atb, it)) for it in items]
    bk = [jnp.concatenate([stack(bf(sub(bt, it))), stack(bf(sub(kt, it)))], axis=0) for it in items]
    prods = [_dot_nt(jnp.concatenate([sub(atb, items[i]), sub(rtb, items[i])], axis=0), bk[i]) for i in n_it]
    l_full = [jnp.where(strict, prods[i][:c, :w], 0.0) for i in n_it]
    a_ak = [bf(jnp.where(strict, prods[i][:c, w:], 0.0)) for i in n_it]
    a_rb = [bf(jnp.where(incl, prods[i][c:, :w], 0.0)) for i in n_it]
    a_rk = [bf(jnp.where(incl, prods[i][c:, w:], 0.0)) for i in n_it]
    vst = [stack(bf(sub(v, it))) for it in items]
    x_loc = [_dot(a_ak[i], vst[i]) for i in n_it]
    y_loc = [_dot(a_rk[i], vst[i]) for i in n_it]

    xf = [bf(jnp.where(blk16, l_full[i], 0.0)) for i in n_it]
    xs = [stack(xf[i]) for i in n_it]
    tm = [eye + jnp.where(blk16, l_full[i], 0.0) for i in n_it]
    for _ in range(3):
        xf = [bf(_dot(xf[i], xs[i])) for i in n_it]
        xs = [stack(xf[i]) for i in n_it]
        tm = [tm[i] + _dot(bf(tm[i]), xs[i]) for i in n_it]
    for inside, outside in ((blk32, blk16), (None, blk32)):
        keep = jnp.logical_not(outside) if inside is None else inside & jnp.logical_not(outside)
        off = [stack(bf(jnp.where(keep, l_full[i], 0.0))) for i in n_it]
        tmb = [bf(tm[i]) for i in n_it]
        half = [bf(_dot(tmb[i], off[i])) for i in n_it]
        tm = [tm[i] + _dot(half[i], stack(tmb[i])) for i in n_it]
    t_fold = [bf(tm[i]) for i in n_it]

    au = [_dot(t_fold[i], jnp.concatenate([ast[i], bf(stack(x_loc[i]))], axis=1)) for i in n_it]
    ah = [au[i][:, :w] for i in n_it]
    ul = [au[i][:, w:] for i in n_it]
    ry = [_dot(a_rb[i], jnp.concatenate([bf(stack(ah[i])), bf(stack(ul[i]))], axis=1)) for i in n_it]
    rh = [bf(sub(rt, items[i]) + ry[i][:, :w]) for i in n_it]
    yl = [y_loc[i] + ry[i][:, w:] for i in n_it]
    w_mat = [bf(jnp.where(same_head, _dot_tn(bf(ah[i]), bf(sub(bh, items[i]))), 0.0)) for i in n_it]
    g_mat = [jnp.where(same_head,
                       _dot_tn(bf(jnp.concatenate([ul[i], sub(v, items[i])], axis=0)),
                               bf(jnp.concatenate([sub(bh, items[i]), sub(kh, items[i])], axis=0))), 0.0)
             for i in n_it]
    decay_c = [jnp.exp(totals[ci][:, p * w:(p + 1) * w]) for ci, p in items]

    s = [s_ref[p] for p in range(npair)]
    ys = [[None] * npair for _ in chunks]
    for i, (ci, p) in enumerate(items):
        sb = bf(s[p])
        ys[ci][p] = _dot_nt(rh[i], sb) + yl[i]
        s[p] = s[p] * decay_c[i] + _dot(sb, w_mat[i]) + g_mat[i]
    for p in range(npair):
        s_ref[p] = s[p]

    y = jnp.concatenate([jnp.concatenate(ys[ci], axis=1) if npair > 1 else ys[ci][0] for ci in chunks], axis=0)
    mean = seg_sum(y) * (1.0 / hn)
    yc = y - mean
    var = seg_sum(yc * yc) * (1.0 / hn)
    yn = yc * lax.rsqrt(var + RW_GN_EPS) * lg_ref[...] + lb_ref[...]
    yn = yn + seg_sum(r * kmod * rk_ref[...]) * v
    o_ref[0] = (yn * g).astype(o_ref.dtype)


def _wkv(r, k, v, z, a, g, k_k, k_a, r_k, lnx_g, lnx_b, batch, seq):
    n, d = r.shape
    w = 2 * RW_HEAD
    npair = WKV_PAIRS
    wt = npair * w
    tb = WKV_TBLOCK
    seq_spec = pl.BlockSpec((1, tb, wt), lambda b, h, t: (b, t, h))
    par_spec = pl.BlockSpec((1, wt), lambda b, h, t: (0, h))
    r3 = lambda t: t.reshape(batch, seq, d)
    out = pl.pallas_call(
        functools.partial(_wkv_kernel, tblock=tb, npair=npair),
        grid=(batch, d // wt, seq // tb),
        in_specs=[seq_spec] * 6 + [par_spec] * 5,
        out_specs=seq_spec,
        out_shape=jax.ShapeDtypeStruct((batch, seq, d), BF16),
        scratch_shapes=[pltpu.VMEM((npair, w, w), F32)],
        compiler_params=_params("parallel", "parallel", "arbitrary"),
        name="wkv7_chunked",
    )(r3(r), r3(k), r3(v), r3(z), r3(a), r3(g), k_k.reshape(1, d), k_a.reshape(1, d), r_k.reshape(1, d),
      lnx_g.reshape(1, d), lnx_b.reshape(1, d))
    return out.reshape(n, d)


def kernel(x, ln1_g, ln1_b, ln2_g, ln2_b, attn_w_qkv, attn_w_o, attn_lambda, attn_subln_g, rw_mu, rw_w_rkv, rw_w_o, rw_w0, rw_w1, rw_w2, rw_a0, rw_a1, rw_a2, rw_g1, rw_g2, rw_k_k, rw_k_a, rw_r_k, rw_lnx_g, rw_lnx_b, rw_v0, rw_v1, rw_v2, moe_rg_w, moe_rg_b, moe_re_w, moe_re_b, moe_w_gu, moe_w_down):
    batch, seq, d = x.shape
    depth = ln1_g.shape[0]
    n = batch * seq
    alpha = (2 * depth) ** 0.25
    x = x.reshape(n, d)

    def qkv_weights(j):
        q_scale = DA_HEAD_DIM ** -0.5 * math.log2(math.e)
        col_scale = jnp.concatenate([jnp.full((d,), q_scale, F32), jnp.ones((2 * d,), F32)])
        return (attn_w_qkv[j] * col_scale[None, :]).astype(BF16)

    def follow_of(i):
        j = i // N_MIXERS
        if i % N_MIXERS == 0:
            return ("qkv", qkv_weights(j))
        value_mix = None if j == 0 else (rw_v0[j - 1], rw_v1[j - 1], rw_v2[j - 1])
        params = _rwkv_params(d, rw_mu[j], rw_w_rkv[j], rw_w0[j], rw_w1[j], rw_w2[j], rw_a0[j], rw_a1[j],
                              rw_a2[j], rw_g1[j], rw_g2[j], value_mix)
        return ("rwkv", seq, params, None if value_mix is None else v_first)

    v_first = None
    proj = (_proj(x, qkv_weights(0), BF16),)
    for i in range(depth):
        j = i // N_MIXERS
        if i % N_MIXERS == 0:
            lambda_init = 0.8 - 0.6 * math.exp(-0.3 * i)
            lam = attn_lambda[j]
            lam_full = jnp.exp(jnp.sum(lam[0] * lam[1])) - jnp.exp(jnp.sum(lam[2] * lam[3])) + lambda_init
            lam_row = jnp.full((1, 2 * DA_HEAD_DIM), lam_full, F32)
            mixed = _diff_attention(proj[0], lam_row, attn_subln_g[j], lambda_init, batch, seq)
            w_o = attn_w_o[j]
        else:
            r, k, v, z, a, g = proj
            if j == 0:
                v_first = v
            mixed = _wkv(r, k, v, z, a, g, rw_k_k[j], rw_k_a[j], rw_r_k[j].reshape(d), rw_lnx_g[j], rw_lnx_b[j],
                         batch, seq)
            w_o = rw_w_o[j]
        n_router = MOE_GROUPS + MOE_EXPERTS
        w_router = _pad_cols(jnp.concatenate([moe_rg_w[i], moe_re_w[i]], axis=1), LANES)
        wr_hi = w_router.astype(BF16)
        wr_lo = (w_router - wr_hi.astype(F32)).astype(BF16)
        w_router = jnp.concatenate([wr_hi, wr_hi, wr_lo], axis=0)
        b_router = _pad_cols(jnp.concatenate([moe_rg_b[i], moe_re_b[i]]).reshape(1, n_router), LANES)
        x1, x1_packed, logits = _post_mixer(mixed, x, w_o.astype(BF16), ln1_g[i], ln1_b[i], w_router, b_router,
                                            alpha)
        follow = follow_of(i + 1) if i + 1 < depth else None
        out = _hier_moe_block(x1, x1_packed, logits, moe_w_gu.reshape((-1,) + moe_w_gu.shape[2:]),
                              moe_w_down.reshape((-1,) + moe_w_down.shape[2:]), i, ln2_g[i], ln2_b[i], alpha, follow)
        if follow is None:
            x = out
        else:
            x, *proj = out
    return x.reshape(batch, seq, d)
```

```python
import functools
import math

import jax
import jax.numpy as jnp
from jax import lax
from jax.experimental import pallas as pl
from jax.experimental.pallas import tpu as pltpu

F32 = jnp.float32
BF16 = jnp.bfloat16

LANES = 128
VMEM_LIMIT = 56 * 1024 * 1024

DA_HEADS = 8
DA_HEAD_DIM = 64
RMS_EPS = 1e-5
RW_HEAD = 64
RW_GN_EPS = 64e-5
MOE_GROUPS = 4
MOE_EPG = 8
MOE_EXPERTS = MOE_GROUPS * MOE_EPG
MOE_TOPK = 2
LN_EPS = 1e-5
N_MIXERS = 2

ROW_TILE = 256
PROJ_TILE = 512
ROUTER_TILE = 512
MOE_TOKEN_TILE = 512
ATTN_TQ = 512
ATTN_TK = 512
WKV_CHUNK = 64
WKV_TBLOCK = 512
WKV_PAIRS = 4
MOE_BLOCK = 512
DMA_UNROLL = 8
COMBINE_GROUPS = 8
EXPERT_PARTS = 2
NEG_BIG = -1e30
HIGH16 = 0xFFFF0000


def _dot(a, b):
    return jnp.dot(a, b, preferred_element_type=F32)


def _dot_nt(a, b):
    return lax.dot_general(a, b, (((1,), (1,)), ((), ())), preferred_element_type=F32)


def _dot_tn(a, b):
    return lax.dot_general(a, b, (((0,), (0,)), ((), ())), preferred_element_type=F32)


def _params(*sem):
    return pltpu.CompilerParams(dimension_semantics=sem, vmem_limit_bytes=VMEM_LIMIT)


def _const_spec(shape):
    nd = len(shape)
    return pl.BlockSpec(shape, lambda *_: (0,) * nd)


def _proj_kernel(x_ref, w_ref, o_ref):
    o_ref[...] = _dot(x_ref[...].astype(BF16), w_ref[...]).astype(o_ref.dtype)


def _proj(x, w, out_dtype):
    n, d = x.shape
    m = w.shape[1]
    tm = PROJ_TILE
    return pl.pallas_call(
        _proj_kernel,
        grid=(n // tm,),
        in_specs=[pl.BlockSpec((tm, d), lambda i: (i, 0)), _const_spec((d, m))],
        out_specs=pl.BlockSpec((tm, m), lambda i: (i, 0)),
        out_shape=jax.ShapeDtypeStruct((n, m), out_dtype),
        compiler_params=_params("parallel"),
        name="qkv_proj",
    )(x, w)


def _attn_kernel(lam_ref, g_ref, q_ref, k_ref, v_ref, o_ref, vx_ref, m_ref, acc_ref, sa_ref, sb_ref, *, tq, tk,
                 out_scale):
    qi = pl.program_id(2)
    hd = DA_HEAD_DIM
    hw = 2 * hd

    @pl.when(qi == 0)
    def _():
        vx_ref[:, :hw] = v_ref[0]
        vx_ref[:, hw:] = jnp.ones((vx_ref.shape[0], hw), BF16)

    q = q_ref[0]
    lane = lax.broadcasted_iota(jnp.int32, q.shape, 1)
    zero = jnp.zeros_like(q)
    q_stack = jnp.concatenate([jnp.where(lane < hd, q, zero), jnp.where(lane >= hd, q, zero)], axis=0)
    m_ref[...] = jnp.full(m_ref.shape, NEG_BIG, F32)
    acc_ref[...] = jnp.zeros(acc_ref.shape, F32)
    row0 = qi * tq

    def scores(j, s_ref):
        start = pl.multiple_of(j * tk, tk)
        s_ref[...] = _dot_nt(q_stack, k_ref[0, pl.ds(start, tk), :])

    def consume(j, s_ref, masked):
        start = pl.multiple_of(j * tk, tk)
        vb = vx_ref[pl.ds(start, tk), :]
        if masked:
            row = row0 + lax.broadcasted_iota(jnp.int32, (tq, tk), 0)
            col = start + lax.broadcasted_iota(jnp.int32, (tq, tk), 1)
            keep = col <= row
        for c in range(2):
            s = s_ref[c * tq:(c + 1) * tq, :]
            if masked:
                s = jnp.where(keep, s, NEG_BIG)
            chunks = [s[:, i * LANES:(i + 1) * LANES] for i in range(tk // LANES)]
            m_old = m_ref[c]
            m_new = jnp.maximum(m_old, jnp.max(functools.reduce(jnp.maximum, chunks), axis=-1, keepdims=True))
            alpha = jnp.exp2(m_old - m_new)
            m_ref[c] = m_new
            p = jnp.concatenate([jnp.exp2(ch - m_new).astype(BF16) for ch in chunks], axis=1)
            acc_ref[c] = acc_ref[c] * jnp.concatenate([alpha, alpha], axis=1) + _dot(p, vb)

    n = row0 // tk + 1
    n_pairs = (n - 1) // 2
    scores(0, sa_ref)

    def pair(jj, carry):
        j = 2 * jj
        scores(j + 1, sb_ref)
        consume(j, sa_ref, False)
        scores(j + 2, sa_ref)
        consume(j + 1, sb_ref, False)
        return carry

    lax.fori_loop(0, n_pairs, pair, 0)
    jb = 2 * n_pairs

    @pl.when(n % 2 == 1)
    def _():
        consume(jb, sa_ref, True)

    @pl.when(n % 2 == 0)
    def _():
        scores(jb + 1, sb_ref)
        consume(jb, sa_ref, False)
        consume(jb + 1, sb_ref, True)

    a0, a1 = acc_ref[0], acc_ref[1]
    o = a0[:, :hw] * (1.0 / a0[:, hw:]) - lam_ref[...] * (a1[:, :hw] * (1.0 / a1[:, hw:]))
    ms = jnp.mean(o * o, axis=-1, keepdims=True)
    o = o * lax.rsqrt(ms + RMS_EPS) * g_ref[...] * out_scale
    o_ref[0] = o.astype(o_ref.dtype)


def _diff_attention(qkv, lam_row, subln_g, lambda_init, batch, seq):
    d = DA_HEADS * 2 * DA_HEAD_DIM
    qkv = qkv.reshape(batch, seq, 3 * d)
    tq, tk = ATTN_TQ, ATTN_TK
    assert tq <= tk and tk % tq == 0 and seq % tk == 0
    hw = 2 * DA_HEAD_DIM
    out = pl.pallas_call(
        functools.partial(_attn_kernel, tq=tq, tk=tk, out_scale=1.0 - lambda_init),
        grid=(batch, DA_HEADS, seq // tq),
        in_specs=[
            _const_spec((1, hw)),
            _const_spec((1, hw)),
            pl.BlockSpec((1, tq, hw), lambda b, h, i: (b, i, h)),
            pl.BlockSpec((1, seq, hw), lambda b, h, i: (b, 0, DA_HEADS + h)),
            pl.BlockSpec((1, seq, hw), lambda b, h, i: (b, 0, 2 * DA_HEADS + h)),
        ],
        out_specs=pl.BlockSpec((1, tq, hw), lambda b, h, i: (b, i, h)),
        out_shape=jax.ShapeDtypeStruct((batch, seq, d), BF16),
        scratch_shapes=[pltpu.VMEM((seq, 2 * hw), BF16), pltpu.VMEM((2, tq, hw), F32),
                        pltpu.VMEM((2, tq, 2 * hw), F32), pltpu.VMEM((2 * tq, tk), F32),
                        pltpu.VMEM((2 * tq, tk), F32)],
        compiler_params=_params("parallel", "parallel", "arbitrary"),
        name="diff_attn",
    )(lam_row, subln_g.reshape(1, hw), qkv, qkv, qkv)
    return out.reshape(batch * seq, d)


def _layer_norm_rows(z, g, b):
    mu = jnp.mean(z, axis=-1, keepdims=True)
    zc = z - mu
    var = jnp.mean(zc * zc, axis=-1, keepdims=True)
    return zc * lax.rsqrt(var + LN_EPS) * g + b


def _pack_bf16_halves(x_hi):
    half = x_hi.shape[1] // 2
    bits = lax.bitcast_convert_type(x_hi, jnp.uint32)
    return (bits[:, :half] >> 16) | (bits[:, half:] & jnp.uint32(HIGH16))


def _unpack_bf16_halves(packed):
    lo = lax.bitcast_convert_type(packed << 16, F32)
    hi = lax.bitcast_convert_type(packed & jnp.uint32(HIGH16), F32)
    return jnp.concatenate([lo, hi], axis=1)


def _post_mixer_kernel(o_ref, x_ref, w_ref, lg_ref, lb_ref, wr_ref, br_ref, x1_ref, x1p_ref, lo_ref, *, alpha):
    h = _dot(o_ref[...], w_ref[...])
    x1 = _layer_norm_rows(alpha * x_ref[...] + h, lg_ref[...], lb_ref[...])
    x1_ref[...] = x1
    x_hi = x1.astype(BF16)
    x_hi32 = x_hi.astype(F32)
    x_lo = (x1 - x_hi32).astype(BF16)
    x1p_ref[...] = _pack_bf16_halves(x_hi32)
    lo_ref[...] = _dot(jnp.concatenate([x_hi, x_lo, x_hi], axis=1), wr_ref[...]) + br_ref[...]


def _post_mixer(o, x, w_o, ln_g, ln_b, w_router, b_router, alpha):
    n, d = x.shape
    tm = ROW_TILE
    row = lambda i: (i, 0)
    return pl.pallas_call(
        functools.partial(_post_mixer_kernel, alpha=alpha),
        grid=(n // tm,),
        in_specs=[
            pl.BlockSpec((tm, d), row), pl.BlockSpec((tm, d), row), _const_spec((d, d)),
            _const_spec((1, d)), _const_spec((1, d)), _const_spec((3 * d, LANES)), _const_spec((1, LANES)),
        ],
        out_specs=[pl.BlockSpec((tm, d), row), pl.BlockSpec((tm, d // 2), row), pl.BlockSpec((tm, LANES), row)],
        out_shape=[jax.ShapeDtypeStruct((n, d), F32), jax.ShapeDtypeStruct((n, d // 2), jnp.uint32),
                   jax.ShapeDtypeStruct((n, LANES), F32)],
        compiler_params=_params("parallel"),
        name="post_mixer",
    )(o, x, w_o, ln_g.reshape(1, d), ln_b.reshape(1, d), w_router, b_router)


def _router_kernel(lo_ref, idx_ref, gate_ref, cnt_ref, carry_ref):
    step = pl.program_id(0)

    @pl.when(step == 0)
    def _():
        carry_ref[...] = jnp.zeros_like(carry_ref)

    lo = lo_ref[...]
    tm = lo.shape[0]
    lane = lax.broadcasted_iota(jnp.int32, lo.shape, 1)
    big = jnp.int32(LANES)

    def softmax_over(mask):
        mx = jnp.max(jnp.where(mask, lo, NEG_BIG), axis=-1, keepdims=True)
        ex = jnp.where(mask, jnp.exp(lo - mx), 0.0)
        return ex / jnp.sum(ex, axis=-1, keepdims=True)

    def top1(p, mask):
        best = jnp.max(jnp.where(mask, p, -1.0), axis=-1, keepdims=True)
        arg = jnp.min(jnp.where(mask & (p == best), lane, big), axis=-1, keepdims=True)
        return best, arg

    g_mask = lane < MOE_GROUPS
    g_p, g_idx = top1(softmax_over(g_mask), g_mask)
    first = MOE_GROUPS + MOE_EPG * g_idx
    e_mask = (lane >= first) & (lane < first + MOE_EPG)
    e_prob = softmax_over(e_mask)
    p1, i1 = top1(e_prob, e_mask)
    rest = e_mask & (lane != i1)
    p2, i2 = top1(e_prob, rest)
    denom = p1 + p2
    gate1 = g_p * (p1 / denom)
    gate2 = g_p * (p2 / denom)
    e1 = i1 - MOE_GROUPS
    e2 = i2 - MOE_GROUPS

    oh1 = (lane == e1).astype(F32)
    oh2 = (lane == e2).astype(F32)
    both = oh1 + oh2
    r_i = lax.broadcasted_iota(jnp.int32, (tm, tm), 0)
    c_i = lax.broadcasted_iota(jnp.int32, (tm, tm), 1)
    before = (c_i < r_i).astype(BF16)
    seen = _dot(before, both.astype(BF16)) + carry_ref[...]
    pos1 = jnp.sum(seen * oh1, axis=-1, keepdims=True).astype(jnp.int32)
    pos2 = jnp.sum(seen * oh2, axis=-1, keepdims=True).astype(jnp.int32)
    carry_ref[...] += jnp.sum(both, axis=0, keepdims=True)
    cnt_ref[...] = carry_ref[...].astype(jnp.int32)

    zero_i = jnp.zeros(lo.shape, jnp.int32)
    idx_ref[...] = (jnp.where(lane == 0, e1, zero_i) + jnp.where(lane == 1, e2, zero_i)
                    + jnp.where(lane == 2, pos1, zero_i) + jnp.where(lane == 3, pos2, zero_i))
    gate_ref[...] = jnp.where(lane == 0, gate1, 0.0) + jnp.where(lane == 1, gate2, 0.0)


def _router(logits):
    n = logits.shape[0]
    tm = ROUTER_TILE
    row = lambda i: (i, 0)
    return pl.pallas_call(
        _router_kernel,
        grid=(n // tm,),
        in_specs=[pl.BlockSpec((tm, LANES), row)],
        out_specs=[pl.BlockSpec((tm, LANES), row), pl.BlockSpec((tm, LANES), row), _const_spec((1, LANES))],
        out_shape=[jax.ShapeDtypeStruct((n, LANES), jnp.int32), jax.ShapeDtypeStruct((n, LANES), F32),
                   jax.ShapeDtypeStruct((1, LANES), jnp.int32)],
        scratch_shapes=[pltpu.VMEM((1, LANES), F32)],
        compiler_params=_params("arbitrary"),
        name="moe_router",
    )(logits)


def _row_copy(src_ref, src_row, dst_ref, dst_row, sem):
    return pltpu.make_async_copy(src_ref.at[pl.ds(src_row, 1)], dst_ref.at[pl.ds(dst_row, 1)], sem)


def _dispatch_kernel(dest_ref, dest_prev_ref, x_ref, buf_in_ref, buf_ref, xs_ref, sems):
    del buf_in_ref
    i = pl.program_id(0)
    last = pl.num_programs(0) - 1
    tm = x_ref.shape[0]
    slot = i % 2

    def copies(d_ref, r, from_slot):
        return [_row_copy(xs_ref.at[from_slot], r, buf_ref, d_ref[0, 0, MOE_TOPK * r + s], sems.at[from_slot])
                for s in range(MOE_TOPK)]

    def start_all(d_ref, from_slot):
        def body(r, carry):
            for cp in copies(d_ref, r, from_slot):
                cp.start()
            return carry
        lax.fori_loop(0, tm, body, 0, unroll=DMA_UNROLL)

    def wait_all(d_ref, from_slot):
        def body(r, carry):
            for cp in copies(d_ref, r, from_slot):
                cp.wait()
            return carry
        lax.fori_loop(0, tm, body, 0, unroll=DMA_UNROLL)

    xs_ref[slot] = x_ref[...]
    start_all(dest_ref, slot)

    @pl.when(i > 0)
    def _():
        wait_all(dest_prev_ref, 1 - slot)

    @pl.when(i == last)
    def _():
        wait_all(dest_ref, slot)


def _dispatch(x1, dest_tiles, n_rows):
    n, d = x1.shape
    tm = MOE_TOKEN_TILE
    zeros = jnp.zeros((n_rows, d), x1.dtype)
    dest_spec = lambda index: pl.BlockSpec((1, 1, MOE_TOPK * tm), index, memory_space=pltpu.SMEM)
    return pl.pallas_call(
        _dispatch_kernel,
        grid=(n // tm,),
        in_specs=[
            dest_spec(lambda i: (i, 0, 0)),
            dest_spec(lambda i: (jnp.maximum(i - 1, 0), 0, 0)),
            pl.BlockSpec((tm, d), lambda i: (i, 0)),
            pl.BlockSpec(memory_space=pl.ANY),
        ],
        out_specs=pl.BlockSpec(memory_space=pl.ANY),
        out_shape=jax.ShapeDtypeStruct((n_rows, d), x1.dtype),
        scratch_shapes=[pltpu.VMEM((2, tm, d), x1.dtype), pltpu.SemaphoreType.DMA((2,))],
        input_output_aliases={3: 0},
        compiler_params=_params("arbitrary"),
        name="moe_dispatch",
    )(dest_tiles, dest_tiles, x1, zeros)


def _expert_kernel(be_ref, used_ref, x_ref, wgu_ref, wd_ref, y_ref, wgu_bf_ref, wd_bf_ref, *, hidden):
    i = pl.program_id(0)
    in_use = i < used_ref[0]

    @pl.when(in_use & ((i == 0) | (be_ref[i] != be_ref[jnp.maximum(i - 1, 0)])))
    def _():
        wgu_bf_ref[...] = wgu_ref[0].astype(BF16)
        wd_bf_ref[...] = wd_ref[0].astype(BF16)

    @pl.when(in_use)
    def _():
        part = x_ref.shape[0] // EXPERT_PARTS
        rows = [pl.ds(p * part, part) for p in range(EXPERT_PARTS)]
        hs = [_dot(_unpack_bf16_halves(x_ref[r, :]).astype(BF16), wgu_bf_ref[...]) for r in rows]
        for r, h in zip(rows, hs):
            hg = h[:, :hidden]
            hu = h[:, hidden:]
            act = hg * (1.0 / (1.0 + jnp.exp(-hg))) * hu
            y_ref[r, :] = _dot(act.astype(BF16), wd_bf_ref[...])

    @pl.when(jnp.logical_not(in_use))
    def _():
        y_ref[...] = jnp.zeros_like(y_ref)


def _experts(x_buf, block_e, n_used, w_gu, w_down):
    n_rows = x_buf.shape[0]
    hidden, d = w_down.shape[1:]
    tm = MOE_BLOCK

    def live(i, used):
        return jnp.minimum(i, used[0] - 1)

    grid_spec = pltpu.PrefetchScalarGridSpec(
        num_scalar_prefetch=2,
        grid=(n_rows // tm,),
        in_specs=[
            pl.BlockSpec((tm, d // 2), lambda i, be, used: (live(i, used), 0)),
            pl.BlockSpec((1, d, 2 * hidden), lambda i, be, used: (be[live(i, used)], 0, 0)),
            pl.BlockSpec((1, hidden, d), lambda i, be, used: (be[live(i, used)], 0, 0)),
        ],
        out_specs=pl.BlockSpec((tm, d), lambda i, be, used: (i, 0)),
        scratch_shapes=[pltpu.VMEM((d, 2 * hidden), BF16), pltpu.VMEM((hidden, d), BF16)],
    )
    return pl.pallas_call(
        functools.partial(_expert_kernel, hidden=hidden),
        grid_spec=grid_spec,
        out_shape=jax.ShapeDtypeStruct((n_rows, d), F32),
        compiler_params=_params("arbitrary"),
        name="moe_experts",
    )(block_e, n_used, x_buf, w_gu, w_down)


def _combine_kernel(dest_ref, dest_next_ref, y_ref, gate_ref, x_ref, lg_ref, lb_ref, o_ref, rows_ref, sems, *, alpha):
    i = pl.program_id(0)
    last = pl.num_programs(0) - 1
    tm = x_ref.shape[0]
    slot = i % 2

    def copies(d_ref, r, to_slot):
        return [_row_copy(y_ref, d_ref[0, 0, MOE_TOPK * r + s], rows_ref.at[to_slot, s], r, sems.at[to_slot])
                for s in range(MOE_TOPK)]

    def start_all(d_ref, to_slot):
        def body(r, carry):
            for cp in copies(d_ref, r, to_slot):
                cp.start()
            return carry
        lax.fori_loop(0, tm, body, 0, unroll=DMA_UNROLL)

    def wait_all(d_ref, to_slot):
        def body(r, carry):
            for cp in copies(d_ref, r, to_slot):
                cp.wait()
            return carry
        lax.fori_loop(0, tm, body, 0, unroll=DMA_UNROLL)

    @pl.when(i == 0)
    def _():
        start_all(dest_ref, slot)

    wait_all(dest_ref, slot)
    group = tm // COMBINE_GROUPS
    for g in range(COMBINE_GROUPS):
        for r in range(g * group, (g + 1) * group):
            for cp in copies(dest_next_ref, r, 1 - slot):
                cp.start()
        rows = pl.ds(g * group, group)
        gate = gate_ref[rows, :]
        f = sum(gate[:, s:s + 1] * rows_ref[slot, s, rows, :] for s in range(MOE_TOPK))
        o_ref[rows, :] = _layer_norm_rows(alpha * x_ref[rows, :] + f, lg_ref[...], lb_ref[...])

    @pl.when(i == last)
    def _():
        wait_all(dest_next_ref, 1 - slot)


def _combine(y, dest_tiles, gates, x1, ln_g, ln_b, alpha):
    n, d = x1.shape
    tm = MOE_TOKEN_TILE
    n_tiles = n // tm
    row = lambda i: (i, 0)
    dest_spec = lambda index: pl.BlockSpec((1, 1, MOE_TOPK * tm), index, memory_space=pltpu.SMEM)
    return pl.pallas_call(
        functools.partial(_combine_kernel, alpha=alpha),
        grid=(n_tiles,),
        in_specs=[
            dest_spec(lambda i: (i, 0, 0)),
            dest_spec(lambda i: (jnp.minimum(i + 1, n_tiles - 1), 0, 0)),
            pl.BlockSpec(memory_space=pl.ANY),
            pl.BlockSpec((tm, LANES), row), pl.BlockSpec((tm, d), row), _const_spec((1, d)), _const_spec((1, d)),
        ],
        out_specs=pl.BlockSpec((tm, d), row),
        out_shape=jax.ShapeDtypeStruct((n, d), F32),
        scratch_shapes=[pltpu.VMEM((2, MOE_TOPK, tm, d), F32), pltpu.SemaphoreType.DMA((2,))],
        compiler_params=_params("arbitrary"),
        name="moe_combine",
    )(dest_tiles, dest_tiles, y, gates, x1, ln_g.reshape(1, d), ln_b.reshape(1, d))


def _hier_moe_block(x1, x1_packed, logits, w_gu, w_down, layer, ln_g, ln_b, alpha):
    n_tok, d = x1.shape
    idx, gates, counts = _router(logits)
    counts = counts[0, :MOE_EXPERTS]
    padded = (counts + MOE_BLOCK - 1) // MOE_BLOCK * MOE_BLOCK
    pad_end = jnp.cumsum(padded)
    pad_start = pad_end - padded
    expert = idx[:, :MOE_TOPK]
    pos = idx[:, MOE_TOPK:2 * MOE_TOPK]
    e_iota = jnp.arange(MOE_EXPERTS, dtype=jnp.int32)
    base = jnp.sum(jnp.where(expert[:, :, None] == e_iota, pad_start, 0), axis=-1)
    dest = (base + pos).astype(jnp.int32)
    n_blocks = n_tok * MOE_TOPK // MOE_BLOCK + MOE_EXPERTS
    n_rows = n_blocks * MOE_BLOCK
    block_start = jnp.arange(n_blocks, dtype=jnp.int32) * MOE_BLOCK
    block_e = jnp.minimum(jnp.sum((pad_end[None, :] <= block_start[:, None]).astype(jnp.int32), axis=1),
                          MOE_EXPERTS - 1) + layer * MOE_EXPERTS
    dest_tiles = dest.reshape(n_tok // MOE_TOKEN_TILE, 1, MOE_TOPK * MOE_TOKEN_TILE)
    x_buf = _dispatch(x1_packed, dest_tiles, n_rows)
    n_used = (pad_end[MOE_EXPERTS - 1:] // MOE_BLOCK).astype(jnp.int32)
    y = _experts(x_buf, block_e, n_used, w_gu, w_down)
    return _combine(y, dest_tiles, gates, x1, ln_g, ln_b, alpha)


def _rwkv_proj_kernel(*refs, seq, has_mix):
    if has_mix:
        (x_ref, xp_ref, mu_ref, wrkv_ref, w0_ref, w1_ref, w2_ref, a0_ref, a1_ref, a2_ref, g1_ref, g2_ref,
         v0_ref, v1_ref, v2_ref, vf_ref, r_ref, k_ref, v_ref, z_ref, a_ref, g_ref) = refs
    else:
        (x_ref, xp_ref, mu_ref, wrkv_ref, w0_ref, w1_ref, w2_ref, a0_ref, a1_ref, a2_ref, g1_ref, g2_ref,
         r_ref, k_ref, v_ref, z_ref, a_ref, g_ref) = refs
    x = x_ref[...]
    tm = x.shape[0]
    row = lax.broadcasted_iota(jnp.int32, x.shape, 0)
    at_seq_start = (pl.program_id(0) * tm) % seq == 0
    prev_last = jnp.where(at_seq_start, 0.0, xp_ref[7:8, :])
    shifted = jnp.where(row == 0, prev_last, pltpu.roll(x, 1, axis=0))
    xx = shifted - x

    def mixed(n):
        return (x + xx * mu_ref[n:n + 1, :]).astype(BF16)

    def sigmoid(t):
        return 0.5 * (1.0 + jnp.tanh(0.5 * t))

    r_ref[...] = _dot(mixed(0), wrkv_ref[0])
    k_ref[...] = _dot(mixed(1), wrkv_ref[1])
    xv = mixed(2)
    v = _dot(xv, wrkv_ref[2])
    if has_mix:
        mix = sigmoid(v0_ref[...] + _dot(_dot(xv, v1_ref[...]).astype(BF16), v2_ref[...]))
        v = v + (vf_ref[...] - v) * mix
    v_ref[...] = v
    z_ref[...] = w0_ref[...] + _dot(jnp.tanh(_dot(mixed(3), w1_ref[...])).astype(BF16), w2_ref[...])
    a_ref[...] = sigmoid(a0_ref[...] + _dot(_dot(mixed(4), a1_ref[...]).astype(BF16), a2_ref[...]))
    g_ref[...] = _dot(sigmoid(_dot(mixed(5), g1_ref[...])).astype(BF16), g2_ref[...])


def _pad_cols(w, width):
    return jnp.pad(w, ((0, 0), (0, width - w.shape[1])))


def _pad_rows(w, height):
    return jnp.pad(w, ((0, height - w.shape[0]), (0, 0)))


def _rwkv_proj(x, seq, mu, w_rkv, w0, w1, w2, a0, a1, a2, g1, g2, value_mix, v_first):
    n, d = x.shape
    tm = ROW_TILE
    row = lambda i: (i, 0)
    lora = lambda w_in, w_out, width: (_pad_cols(w_in, width).astype(BF16), _pad_rows(w_out, width).astype(BF16))
    w1p, w2p = lora(w1, w2, LANES)
    a1p, a2p = lora(a1, a2, LANES)
    g1p, g2p = lora(g1, g2, 2 * LANES)
    mu8 = _pad_rows(mu, 8)
    ins = [x, x, mu8, w_rkv.astype(BF16), w0.reshape(1, d), w1p, w2p, a0.reshape(1, d), a1p, a2p, g1p, g2p]
    specs = [
        pl.BlockSpec((tm, d), row),
        pl.BlockSpec((8, d), lambda i: (jnp.maximum(i * (tm // 8) - 1, 0), 0)),
        _const_spec((8, d)), _const_spec((3, d, d)), _const_spec((1, d)),
        _const_spec((d, LANES)), _const_spec((LANES, d)), _const_spec((1, d)),
        _const_spec((d, LANES)), _const_spec((LANES, d)),
        _const_spec((d, 2 * LANES)), _const_spec((2 * LANES, d)),
    ]
    has_mix = value_mix is not None
    if has_mix:
        v0, v1, v2 = value_mix
        v1p, v2p = lora(v1, v2, LANES)
        ins += [v0.reshape(1, d), v1p, v2p, v_first]
        specs += [_const_spec((1, d)), _const_spec((d, LANES)), _const_spec((LANES, d)), pl.BlockSpec((tm, d), row)]
    out = jax.ShapeDtypeStruct((n, d), F32)
    return pl.pallas_call(
        functools.partial(_rwkv_proj_kernel, seq=seq, has_mix=has_mix),
        grid=(n // tm,),
        in_specs=specs,
        out_specs=[pl.BlockSpec((tm, d), row)] * 6,
        out_shape=[out] * 6,
        compiler_params=_params("parallel"),
        name="rwkv_proj",
    )(*ins)


def _wkv_kernel(r_ref, k_ref, v_ref, z_ref, a_ref, g_ref, kk_ref, ka_ref, rk_ref, lg_ref, lb_ref, o_ref, s_ref,
                *, tblock, npair):
    c = WKV_CHUNK
    hn = RW_HEAD
    w = 2 * hn
    wt = npair * w

    @pl.when(pl.program_id(2) == 0)
    def _():
        s_ref[...] = jnp.zeros_like(s_ref)

    def iota2(shape, dim):
        return lax.broadcasted_iota(jnp.int32, shape, dim)

    lane_c = iota2((c, w), 1)
    head0 = lane_c < hn
    rw, cw = iota2((w, w), 0), iota2((w, w), 1)
    same_head = (rw // hn) == (cw // hn)
    t_row, s_col = iota2((c, w), 0), lane_c % hn
    strict = s_col < t_row
    incl = s_col <= t_row
    blk16 = (t_row // 16) == (s_col // 16)
    blk32 = (t_row // 32) == (s_col // 32)
    eye = (t_row == s_col).astype(F32)
    r2, c2 = iota2((2 * w, w), 0), iota2((2 * w, w), 1)
    seg_ones2 = (((r2 % w) // hn) == (c2 // hn)).astype(BF16)
    tri3 = (iota2((c, 3 * c), 1) % c <= iota2((c, 3 * c), 0)).astype(BF16)

    def split(t, pieces):
        out = []
        for _ in range(pieces - 1):
            out.append(t.astype(BF16))
            t = t - out[-1].astype(F32)
        return out + [t.astype(BF16)]

    def seg_sum(t):
        return jnp.concatenate(
            [_dot(jnp.concatenate(split(t[:, p * w:(p + 1) * w], 2), axis=1), seg_ones2) for p in range(npair)],
            axis=1)

    def stack(t):
        zero = jnp.zeros_like(t)
        return jnp.concatenate([jnp.where(head0, t, zero), jnp.where(head0, zero, t)], axis=0)

    def bf(t):
        return t.astype(BF16)

    nch = tblock // c
    chunks = range(nch)
    items = [(ci, p) for ci in chunks for p in range(npair)]

    def rows(t, ci):
        return t[ci * c:(ci + 1) * c]

    def sub(t, it):
        ci, p = it
        return t[ci * c:(ci + 1) * c, p * w:(p + 1) * w]

    r, k, v, z, a, g = (ref[0] for ref in (r_ref, k_ref, v_ref, z_ref, a_ref, g_ref))
    kk = k * kk_ref[...]
    kk = kk * jnp.minimum(lax.rsqrt(seg_sum(kk * kk)), 1e12)
    kmod = k * (1.0 + (a - 1.0) * ka_ref[...])
    bvec = kk * a
    lw = (-0.5 * math.exp(-0.5)) * (1.0 + jnp.tanh(0.5 * z))
    cum = jnp.concatenate([_dot(tri3, jnp.concatenate(split(rows(lw, ci), 3), axis=0)) for ci in chunks], axis=0)
    at = -kk * jnp.exp(cum - lw)
    rt = r * jnp.exp(cum)
    inv = jnp.exp(-cum)
    bt = bvec * inv
    kt = kmod * inv
    totals = [cum[(ci + 1) * c - 1:(ci + 1) * c, :] for ci in chunks]
    to_end = jnp.concatenate([jnp.exp(totals[ci] - rows(cum, ci)) for ci in chunks], axis=0)
    bh = bvec * to_end
    kh = kmod * to_end

    n_it = range(len(items))
    atb, rtb = bf(at), bf(rt)
    ast = [stack(sub(atb, it)) for it in items]
    bk = [jnp.concatenate([stack(bf(sub(bt, it))), stack(bf(sub(kt, it)))], axis=0) for it in items]
    prods = [_dot_nt(jnp.concatenate([sub(atb, items[i]), sub(rtb, items[i])], axis=0), bk[i]) for i in n_it]
    l_full = [jnp.where(strict, prods[i][:c, :w], 0.0) for i in n_it]
    a_ak = [bf(jnp.where(strict, prods[i][:c, w:], 0.0)) for i in n_it]
    a_rb = [bf(jnp.where(incl, prods[i][c:, :w], 0.0)) for i in n_it]
    a_rk = [bf(jnp.where(incl, prods[i][c:, w:], 0.0)) for i in n_it]
    vst = [stack(bf(sub(v, it))) for it in items]
    x_loc = [_dot(a_ak[i], vst[i]) for i in n_it]
    y_loc = [_dot(a_rk[i], vst[i]) for i in n_it]

    xf = [bf(jnp.where(blk16, l_full[i], 0.0)) for i in n_it]
    xs = [stack(xf[i]) for i in n_it]
    tm = [eye + jnp.where(blk16, l_full[i], 0.0) for i in n_it]
    for _ in range(3):
        xf = [bf(_dot(xf[i], xs[i])) for i in n_it]
        xs = [stack(xf[i]) for i in n_it]
        tm = [tm[i] + _dot(bf(tm[i]), xs[i]) for i in n_it]
    for inside, outside in ((blk32, blk16), (None, blk32)):
        keep = jnp.logical_not(outside) if inside is None else inside & jnp.logical_not(outside)
        off = [stack(bf(jnp.where(keep, l_full[i], 0.0))) for i in n_it]
        tmb = [bf(tm[i]) for i in n_it]
        half = [bf(_dot(tmb[i], off[i])) for i in n_it]
        tm = [tm[i] + _dot(half[i], stack(tmb[i])) for i in n_it]
    t_fold = [bf(tm[i]) for i in n_it]

    au = [_dot(t_fold[i], jnp.concatenate([ast[i], bf(stack(x_loc[i]))], axis=1)) for i in n_it]
    ah = [au[i][:, :w] for i in n_it]
    ul = [au[i][:, w:] for i in n_it]
    ry = [_dot(a_rb[i], jnp.concatenate([bf(stack(ah[i])), bf(stack(ul[i]))], axis=1)) for i in n_it]
    rh = [bf(sub(rt, items[i]) + ry[i][:, :w]) for i in n_it]
    yl = [y_loc[i] + ry[i][:, w:] for i in n_it]
    w_mat = [bf(jnp.where(same_head, _dot_tn(bf(ah[i]), bf(sub(bh, items[i]))), 0.0)) for i in n_it]
    g_mat = [jnp.where(same_head,
                       _dot_tn(bf(jnp.concatenate([ul[i], sub(v, items[i])], axis=0)),
                               bf(jnp.concatenate([sub(bh, items[i]), sub(kh, items[i])], axis=0))), 0.0)
             for i in n_it]
    decay_c = [jnp.exp(totals[ci][:, p * w:(p + 1) * w]) for ci, p in items]

    s = [s_ref[p] for p in range(npair)]
    ys = [[None] * npair for _ in chunks]
    for i, (ci, p) in enumerate(items):
        sb = bf(s[p])
        ys[ci][p] = _dot_nt(rh[i], sb) + yl[i]
        s[p] = s[p] * decay_c[i] + _dot(sb, w_mat[i]) + g_mat[i]
    for p in range(npair):
        s_ref[p] = s[p]

    y = jnp.concatenate([jnp.concatenate(ys[ci], axis=1) if npair > 1 else ys[ci][0] for ci in chunks], axis=0)
    mean = seg_sum(y) * (1.0 / hn)
    yc = y - mean
    var = seg_sum(yc * yc) * (1.0 / hn)
    yn = yc * lax.rsqrt(var + RW_GN_EPS) * lg_ref[...] + lb_ref[...]
    yn = yn + seg_sum(r * kmod * rk_ref[...]) * v
    o_ref[0] = (yn * g).astype(o_ref.dtype)


def _wkv(r, k, v, z, a, g, k_k, k_a, r_k, lnx_g, lnx_b, batch, seq):
    n, d = r.shape
    w = 2 * RW_HEAD
    npair = WKV_PAIRS
    wt = npair * w
    tb = WKV_TBLOCK
    seq_spec = pl.BlockSpec((1, tb, wt), lambda b, h, t: (b, t, h))
    par_spec = pl.BlockSpec((1, wt), lambda b, h, t: (0, h))
    r3 = lambda t: t.reshape(batch, seq, d)
    out = pl.pallas_call(
        functools.partial(_wkv_kernel, tblock=tb, npair=npair),
        grid=(batch, d // wt, seq // tb),
        in_specs=[seq_spec] * 6 + [par_spec] * 5,
        out_specs=seq_spec,
        out_shape=jax.ShapeDtypeStruct((batch, seq, d), BF16),
        scratch_shapes=[pltpu.VMEM((npair, w, w), F32)],
        compiler_params=_params("parallel", "parallel", "arbitrary"),
        name="wkv7_chunked",
    )(r3(r), r3(k), r3(v), r3(z), r3(a), r3(g), k_k.reshape(1, d), k_a.reshape(1, d), r_k.reshape(1, d),
      lnx_g.reshape(1, d), lnx_b.reshape(1, d))
    return out.reshape(n, d)


def kernel(x, ln1_g, ln1_b, ln2_g, ln2_b, attn_w_qkv, attn_w_o, attn_lambda, attn_subln_g, rw_mu, rw_w_rkv, rw_w_o, rw_w0, rw_w1, rw_w2, rw_a0, rw_a1, rw_a2, rw_g1, rw_g2, rw_k_k, rw_k_a, rw_r_k, rw_lnx_g, rw_lnx_b, rw_v0, rw_v1, rw_v2, moe_rg_w, moe_rg_b, moe_re_w, moe_re_b, moe_w_gu, moe_w_down):
    batch, seq, d = x.shape
    depth = ln1_g.shape[0]
    n = batch * seq
    alpha = (2 * depth) ** 0.25
    x = x.reshape(n, d)
    v_first = None
    for i in range(depth):
        j = i // N_MIXERS
        if i % N_MIXERS == 0:
            lambda_init = 0.8 - 0.6 * math.exp(-0.3 * i)
            lam = attn_lambda[j]
            lam_full = jnp.exp(jnp.sum(lam[0] * lam[1])) - jnp.exp(jnp.sum(lam[2] * lam[3])) + lambda_init
            lam_row = jnp.full((1, 2 * DA_HEAD_DIM), lam_full, F32)
            q_scale = DA_HEAD_DIM ** -0.5 * math.log2(math.e)
            col_scale = jnp.concatenate([jnp.full((d,), q_scale, F32), jnp.ones((2 * d,), F32)])
            w_qkv = (attn_w_qkv[j] * col_scale[None, :]).astype(BF16)
            qkv = _proj(x, w_qkv, BF16)
            mixed = _diff_attention(qkv, lam_row, attn_subln_g[j], lambda_init, batch, seq)
            w_o = attn_w_o[j]
        else:
            value_mix = None if j == 0 else (rw_v0[j - 1], rw_v1[j - 1], rw_v2[j - 1])
            r, k, v, z, a, g = _rwkv_proj(x, seq, rw_mu[j], rw_w_rkv[j], rw_w0[j], rw_w1[j], rw_w2[j], rw_a0[j],
                                          rw_a1[j], rw_a2[j], rw_g1[j], rw_g2[j], value_mix, v_first)
            if value_mix is None:
                v_first = v
            mixed = _wkv(r, k, v, z, a, g, rw_k_k[j], rw_k_a[j], rw_r_k[j].reshape(d), rw_lnx_g[j], rw_lnx_b[j],
                         batch, seq)
            w_o = rw_w_o[j]
        n_router = MOE_GROUPS + MOE_EXPERTS
        w_router = _pad_cols(jnp.concatenate([moe_rg_w[i], moe_re_w[i]], axis=1), LANES)
        wr_hi = w_router.astype(BF16)
        wr_lo = (w_router - wr_hi.astype(F32)).astype(BF16)
        w_router = jnp.concatenate([wr_hi, wr_hi, wr_lo], axis=0)
        b_router = _pad_cols(jnp.concatenate([moe_rg_b[i], moe_re_b[i]]).reshape(1, n_router), LANES)
        x1, x1_packed, logits = _post_mixer(mixed, x, w_o.astype(BF16), ln1_g[i], ln1_b[i], w_router, b_router,
                                            alpha)
        x = _hier_moe_block(x1, x1_packed, logits, moe_w_gu.reshape((-1,) + moe_w_gu.shape[2:]),
                            moe_w_down.reshape((-1,) + moe_w_down.shape[2:]), i, ln2_g[i], ln2_b[i], alpha)
    return x.reshape(batch, seq, d)
```
